```python
import math
import jax, jax.numpy as jnp
from jax import lax
import numpy as np

D_MODEL = 1024
BATCH = 8
SEQ = 4096
DEPTH = 1


GRID_W = 64
CTX_LEN = 256
N_MOD = 6
RMS_EPS = 1e-6

D_RNN = D_MODEL
RNN_HEADS = 16
RNN_HEAD_DIM = D_RNN // RNN_HEADS
RNN_CONV_W = 4
RNN_CONV_PAD = (2, 1)
RG_C = 8.0

D_HY = D_MODEL
HY_ORDER = 2
HY_CONV_W = 3
HY_CONV_PAD = (1, 1)
HY_EMB = 33
HY_BANDS = (HY_EMB - 1) // 2
HY_FILTER_ORDER = 64
HY_FAST_DECAY = 0.3
HY_SLOW_DECAY = 1.5
HY_TARGET = 1e-2

D_FF = ((8 * D_MODEL // 3 + 255) // 256) * 256

D_IN = 2 * D_RNN + 3 * D_HY + 2 * D_MODEL

kernel_name = 'hybrid_rglru_hyena_dit_block'


def rms_norm(x, g):
    xf = x.astype(jnp.float32)
    y = xf * lax.rsqrt(jnp.mean(xf * xf, axis=-1, keepdims=True) + RMS_EPS)
    return (y * g.astype(jnp.float32)).astype(x.dtype)


def modulate(h, shift, scale):
    return h * (1.0 + scale) + shift


def short_conv(u, w, b, pad):
    L = u.shape[-2]
    widths = [(0, 0)] * (u.ndim - 2) + [pad, (0, 0)]
    up = jnp.pad(u, widths)
    out = b
    for k in range(w.shape[0]):
        out = out + up[..., k:k + L, :] * w[k]
    return out


def seq_conv(u, w, b, pad, grid):
    if grid:
        B, T, C = u.shape
        rows = T // GRID_W
        return short_conv(u.reshape(B, rows, GRID_W, C), w, b, pad).reshape(B, T, C)
    return short_conv(u, w, b, pad)


def linear_scan(a, b, h0):
    def combine(e1, e2):
        a1, b1 = e1
        a2, b2 = e2
        return a1 * a2, a2 * b1 + b2
    A, H = lax.associative_scan(combine, (a, b), axis=1)
    return H + A * h0[:, None, :]


def rglru(u, wa, ba, wx, bx, lam, h0_f, h0_b):
    B, L, _ = u.shape
    uh = u.reshape(B, L, RNN_HEADS, RNN_HEAD_DIM)
    r = jax.nn.sigmoid(jnp.einsum('blhi,dhij->dblhj', uh, wa.astype(jnp.float32)).reshape(2, B, L, D_RNN)
                       + ba.astype(jnp.float32)[:, None, None, :])
    i = jax.nn.sigmoid(jnp.einsum('blhi,dhij->dblhj', uh, wx.astype(jnp.float32)).reshape(2, B, L, D_RNN)
                       + bx.astype(jnp.float32)[:, None, None, :])
    log_a = -RG_C * r * jax.nn.softplus(-lam.astype(jnp.float32))[:, None, None, :]
    a = jnp.exp(log_a)
    bb = jnp.sqrt(-jnp.expm1(2.0 * log_a)) * (i * u[None])
    h_f = linear_scan(a[0], bb[0], h0_f)
    h_b = jnp.flip(linear_scan(jnp.flip(a[1], 1), jnp.flip(bb[1], 1), h0_b), 1)
    return h_f, h_b


def hyena_filters(L, lp):
    f32 = jnp.float32
    t = jnp.linspace(0.0, 1.0, L, dtype=f32)[:, None]
    w = 2.0 * math.pi * jnp.arange(L, dtype=f32)[:, None] / L
    f = jnp.linspace(1e-4, HY_BANDS - 1, HY_BANDS, dtype=f32)[None, :]
    z = jnp.concatenate([t, jnp.cos(f * w), -jnp.sin(f * w)], axis=-1)
    freq = lp['hy_freq'].astype(f32)
    hdn = jnp.sin(freq * (z @ lp['hy_w1'].astype(f32) + lp['hy_b1'].astype(f32)))
    hdn = jnp.sin(freq * (hdn @ lp['hy_w2'].astype(f32) + lp['hy_b2'].astype(f32)))
    hdn = jnp.sin(freq * (hdn @ lp['hy_w3'].astype(f32) + lp['hy_b3'].astype(f32)))
    k = (hdn @ lp['hy_w4'].astype(f32)).reshape(L, HY_ORDER, 2, D_HY)
    max_decay = math.log(HY_TARGET) / HY_FAST_DECAY
    min_decay = math.log(HY_TARGET) / HY_SLOW_DECAY
    deltas = jnp.abs(jnp.linspace(min_decay, max_decay, D_HY, dtype=f32))
    k = k * jnp.exp(-t[:, :, None, None] * deltas)
    fwd, bwd = k[:, :, 0], k[:, :, 1]
    full = jnp.concatenate([fwd, jnp.zeros((1, HY_ORDER, D_HY), f32), jnp.flip(bwd[1:], 0)], axis=0)
    return full / jnp.sum(jnp.abs(full), axis=0, keepdims=True)


def hyena(phy, lp, grid):
    q = seq_conv(phy, lp['hy_conv_w'], lp['hy_conv_b'], HY_CONV_PAD, grid).astype(jnp.float32)
    v, x1, x2 = jnp.split(q, 3, axis=-1)
    L = q.shape[1]
    k_f = jnp.fft.rfft(hyena_filters(L, lp), axis=0)
    skip = lp['hy_skip'].astype(jnp.float32)
    z = v
    for o, gate in enumerate((x1, x2)):
        z_f = jnp.fft.rfft(z, n=2 * L, axis=1)
        conv = jnp.fft.irfft(z_f * k_f[None, :, o], n=2 * L, axis=1)[:, :L]
        z = gate * (conv + skip[o] * z)
    return z


def rnn_states(h, lp, grid, h0_f, h0_b):
    rx = h @ lp['w_in'][:, :D_RNN] + lp['b_in'][:D_RNN]
    u = seq_conv(rx, lp['rnn_conv_w'], lp['rnn_conv_b'], RNN_CONV_PAD, grid).astype(jnp.float32)
    return rglru(u, lp['rg_wa'], lp['rg_ba'], lp['rg_wx'], lp['rg_bx'], lp['rg_lambda'], h0_f, h0_b)


def mixer(h, lp, grid, h0_f, h0_b):
    h_f, h_b = rnn_states(h, lp, grid, h0_f, h0_b)
    rest = h @ lp['w_in'][:, D_RNN:] + lp['b_in'][D_RNN:]
    rg = rest[..., :D_RNN]
    phy = rest[..., D_RNN:D_RNN + 3 * D_HY]
    bg = rest[..., D_RNN + 3 * D_HY:]
    y_rnn = ((h_f + h_b) * jax.nn.gelu(rg.astype(jnp.float32))).astype(h.dtype)
    y_hy = hyena(phy, lp, grid).astype(h.dtype)
    g_a, g_b = jnp.split(jax.nn.sigmoid(bg), 2, axis=-1)
    merged = g_a * (y_rnn @ lp['w_a_out']) + g_b * (y_hy @ lp['w_b_out'])
    return merged @ lp['w_out'], h_f[:, -1], h_b[:, 0]


def swiglu(h, lp):
    g, u = jnp.split(h @ lp['w_ffn_in'], 2, axis=-1)
    return (jax.nn.silu(g) * u) @ lp['w_ffn_out']


def block(x, ctx, c, c_ctx, lp, update_ctx):
    mod = jax.nn.silu(c) @ lp['w_mod'] + lp['b_mod']
    mod_c = jax.nn.silu(c_ctx) @ lp['w_mod'] + lp['b_mod']
    sh1, sc1, g1, sh2, sc2, g2 = jnp.split(mod[:, None, :], N_MOD, axis=-1)
    csh1, csc1, cg1, csh2, csc2, cg2 = jnp.split(mod_c, N_MOD, axis=-1)
    B = x.shape[0]
    zeros = jnp.zeros((B, D_RNN), jnp.float32)
    hc = modulate(rms_norm(ctx, lp['norm1_g']), csh1, csc1)
    if update_ctx:
        yc, cf, cb = mixer(hc, lp, False, zeros, zeros)
        ctx = ctx + cg1 * yc
        ctx = ctx + cg2 * swiglu(modulate(rms_norm(ctx, lp['norm2_g']), csh2, csc2), lp)
    else:
        hcf, hcb = rnn_states(hc, lp, False, zeros, zeros)
        cf, cb = hcf[:, -1], hcb[:, 0]
    hx = modulate(rms_norm(x, lp['norm1_g']), sh1, sc1)
    yx, _, _ = mixer(hx, lp, True, cf, cb)
    x = x + g1 * yx
    x = x + g2 * swiglu(modulate(rms_norm(x, lp['norm2_g']), sh2, sc2), lp)
    return x, ctx


def setup_inputs(seed: int = 0) -> dict:
    key = jax.random.key(seed)
    ks = jax.random.split(key, 40)
    f32 = jnp.float32

    def nrm(k, shape, scale):
        return jax.random.normal(k, shape, f32) * scale

    u = jax.random.uniform(ks[16], (DEPTH, 2, D_RNN), f32, minval=0.9, maxval=0.999)
    a0 = u ** (1.0 / RG_C)
    return {
        'x': nrm(ks[0], (BATCH, SEQ, D_MODEL), 1.0),
        'c': nrm(ks[1], (BATCH, D_MODEL), 1.0),
        'ctx': nrm(ks[2], (BATCH, CTX_LEN, D_MODEL), 1.0),
        'c_ctx': nrm(ks[3], (D_MODEL,), 1.0),
        'w_mod': nrm(ks[4], (DEPTH, D_MODEL, N_MOD * D_MODEL), D_MODEL ** -0.5),
        'b_mod': nrm(ks[5], (DEPTH, N_MOD * D_MODEL), 0.02),
        'norm1_g': 1.0 + nrm(ks[6], (DEPTH, D_MODEL), 0.02),
        'norm2_g': 1.0 + nrm(ks[7], (DEPTH, D_MODEL), 0.02),
        'w_in': nrm(ks[8], (DEPTH, D_MODEL, D_IN), D_MODEL ** -0.5),
        'b_in': nrm(ks[9], (DEPTH, D_IN), 0.02),
        'rnn_conv_w': nrm(ks[10], (DEPTH, RNN_CONV_W, D_RNN), RNN_CONV_W ** -0.5),
        'rnn_conv_b': nrm(ks[11], (DEPTH, D_RNN), 0.02),
        'rg_wa': nrm(ks[12], (DEPTH, 2, RNN_HEADS, RNN_HEAD_DIM, RNN_HEAD_DIM), RNN_HEAD_DIM ** -0.5),
        'rg_ba': nrm(ks[13], (DEPTH, 2, D_RNN), 0.02),
        'rg_wx': nrm(ks[14], (DEPTH, 2, RNN_HEADS, RNN_HEAD_DIM, RNN_HEAD_DIM), RNN_HEAD_DIM ** -0.5),
        'rg_bx': nrm(ks[15], (DEPTH, 2, D_RNN), 0.02),
        'rg_lambda': jnp.log(a0) - jnp.log1p(-a0),
        'hy_conv_w': nrm(ks[17], (DEPTH, HY_CONV_W, 3 * D_HY), HY_CONV_W ** -0.5),
        'hy_conv_b': nrm(ks[18], (DEPTH, 3 * D_HY), 0.02),
        'hy_w1': nrm(ks[19], (DEPTH, HY_EMB, HY_FILTER_ORDER), HY_EMB ** -0.5),
        'hy_b1': nrm(ks[20], (DEPTH, HY_FILTER_ORDER), 0.1),
        'hy_w2': nrm(ks[21], (DEPTH, HY_FILTER_ORDER, HY_FILTER_ORDER), HY_FILTER_ORDER ** -0.5),
        'hy_b2': nrm(ks[22], (DEPTH, HY_FILTER_ORDER), 0.1),
        'hy_w3': nrm(ks[23], (DEPTH, HY_FILTER_ORDER, HY_FILTER_ORDER), HY_FILTER_ORDER ** -0.5),
        'hy_b3': nrm(ks[24], (DEPTH, HY_FILTER_ORDER), 0.1),
        'hy_freq': 1.0 + nrm(ks[25], (DEPTH, HY_FILTER_ORDER), 0.01),
        'hy_w4': nrm(ks[26], (DEPTH, HY_FILTER_ORDER, HY_ORDER * 2 * D_HY), HY_FILTER_ORDER ** -0.5),
        'hy_skip': nrm(ks[27], (DEPTH, HY_ORDER, D_HY), 1.0),
        'w_a_out': nrm(ks[28], (DEPTH, D_RNN, D_MODEL), D_RNN ** -0.5),
        'w_b_out': nrm(ks[29], (DEPTH, D_HY, D_MODEL), D_HY ** -0.5),
        'w_out': nrm(ks[30], (DEPTH, D_MODEL, D_MODEL), D_MODEL ** -0.5),
        'w_ffn_in': nrm(ks[31], (DEPTH, D_MODEL, 2 * D_FF), D_MODEL ** -0.5),
        'w_ffn_out': nrm(ks[32], (DEPTH, D_FF, D_MODEL), D_FF ** -0.5),
        'final_g': 1.0 + nrm(ks[33], (D_MODEL,), 0.02),
    }


def reference(x, c, ctx, c_ctx, w_mod, b_mod, norm1_g, norm2_g, w_in, b_in, rnn_conv_w, rnn_conv_b,
              rg_wa, rg_ba, rg_wx, rg_bx, rg_lambda, hy_conv_w, hy_conv_b, hy_w1, hy_b1, hy_w2, hy_b2,
              hy_w3, hy_b3, hy_freq, hy_w4, hy_skip, w_a_out, w_b_out, w_out, w_ffn_in, w_ffn_out, final_g):
    for l in range(DEPTH):
        lp = dict(w_mod=w_mod[l], b_mod=b_mod[l], norm1_g=norm1_g[l], norm2_g=norm2_g[l],
                  w_in=w_in[l], b_in=b_in[l], rnn_conv_w=rnn_conv_w[l], rnn_conv_b=rnn_conv_b[l],
                  rg_wa=rg_wa[l], rg_ba=rg_ba[l], rg_wx=rg_wx[l], rg_bx=rg_bx[l], rg_lambda=rg_lambda[l],
                  hy_conv_w=hy_conv_w[l], hy_conv_b=hy_conv_b[l], hy_w1=hy_w1[l], hy_b1=hy_b1[l],
                  hy_w2=hy_w2[l], hy_b2=hy_b2[l], hy_w3=hy_w3[l], hy_b3=hy_b3[l], hy_freq=hy_freq[l],
                  hy_w4=hy_w4[l], hy_skip=hy_skip[l], w_a_out=w_a_out[l], w_b_out=w_b_out[l],
                  w_out=w_out[l], w_ffn_in=w_ffn_in[l], w_ffn_out=w_ffn_out[l])
        x, ctx = block(x, ctx, c, c_ctx, lp, l < DEPTH - 1)
    return rms_norm(x, final_g)
```

```python
import functools
import math

import numpy as np
import jax
import jax.numpy as jnp
from jax import lax
from jax.experimental import pallas as pl
from jax.experimental.pallas import tpu as pltpu

D_MODEL = 1024
GRID_W = 64
N_MOD = 6
RMS_EPS = 1e-6
RNN_HEADS = 16
RNN_HEAD_DIM = D_MODEL // RNN_HEADS
RNN_CONV_PAD_LEFT = 2
HY_CONV_PAD_LEFT = 1
RG_C = 8.0
HY_ORDER = 2
HY_EMB = 33
HY_BANDS = (HY_EMB - 1) // 2
HY_FAST_DECAY = 0.3
HY_SLOW_DECAY = 1.5
HY_TARGET = 1e-2
D_FF = ((8 * D_MODEL // 3 + 255) // 256) * 256

SUBLANES = 8
LANE_TILE = 256
VMEM_LIMIT = 56 * 1024 * 1024

FFT_N1 = 64
FFT_N2 = 128
FFT_N = FFT_N1 * FFT_N2
FFT_H = FFT_N1 // 2

_f32 = jnp.float32
_bf16 = jnp.bfloat16


def _params(**kw):
    return pltpu.CompilerParams(vmem_limit_bytes=VMEM_LIMIT, **kw)


def _dot(a, b):
    return jnp.dot(a, b, preferred_element_type=_f32)


def _mod_kernel(c_ref, w_ref, b_ref, o_ref):
    c = c_ref[...]
    s = c * jax.nn.sigmoid(c)
    o_ref[...] = jnp.dot(s, w_ref[...], preferred_element_type=_f32,
                         precision=lax.Precision.HIGHEST) + b_ref[...]


def _mod_vectors(c_all, w_mod, b_mod):
    rows, d = c_all.shape
    n = w_mod.shape[1]
    tn = 1024
    return pl.pallas_call(
        _mod_kernel,
        grid=(n // tn,),
        in_specs=[pl.BlockSpec((rows, d), lambda j: (0, 0)),
                  pl.BlockSpec((d, tn), lambda j: (0, j)),
                  pl.BlockSpec((1, tn), lambda j: (0, j))],
        out_specs=pl.BlockSpec((rows, tn), lambda j: (0, j)),
        out_shape=jax.ShapeDtypeStruct((rows, n), _f32),
        compiler_params=_params(),
        name="mod_vectors",
    )(c_all, w_mod, b_mod.reshape(1, n))


def _norm_mod_kernel(x_ref, g_ref, sh_ref, sc_ref, o_ref):
    x = x_ref[...]
    y = x * lax.rsqrt(jnp.mean(x * x, axis=-1, keepdims=True) + RMS_EPS)
    y = y * g_ref[...]
    o_ref[...] = (y * (1.0 + sc_ref[...]) + sh_ref[...]).astype(o_ref.dtype)


def _norm_mod(x, g, sh, sc, tm):
    b, l, d = x.shape
    vec = pl.BlockSpec((None, 1, d), lambda i, j: (i, 0, 0))
    return pl.pallas_call(
        _norm_mod_kernel,
        grid=(b, l // tm),
        in_specs=[pl.BlockSpec((None, tm, d), lambda i, j: (i, j, 0)),
                  pl.BlockSpec((1, d), lambda i, j: (0, 0)), vec, vec],
        out_specs=pl.BlockSpec((None, tm, d), lambda i, j: (i, j, 0)),
        out_shape=jax.ShapeDtypeStruct((b, l, d), _bf16),
        compiler_params=_params(),
        name="norm_mod",
    )(x, g.reshape(1, d), sh, sc)


def _short_conv(y, w_ref, cb_ref, pad_left, period):
    rows = y.shape[0]
    pos = lax.broadcasted_iota(jnp.int32, y.shape, 0) % period
    out = jnp.broadcast_to(cb_ref[...], y.shape)
    for k in range(w_ref.shape[0]):
        off = k - pad_left
        if off == 0:
            tap = y
        else:
            tap = pltpu.roll(y, (-off) % rows, axis=0)
            valid = (pos + off >= 0) & (pos + off < period)
            tap = jnp.where(valid, tap, 0.0)
        out = out + tap * w_ref[k:k + 1, :]
    return out


def _proj_kernel(x_ref, w_ref, b_ref, *rest, mode, pad_left, period):
    o_ref = rest[-1]
    y = _dot(x_ref[...], w_ref[...]) + b_ref[...]
    if mode == "conv":
        y = _short_conv(y, rest[0], rest[1], pad_left, period)
    elif mode == "gelu":
        y = jax.nn.gelu(y)
    o_ref[...] = y.astype(o_ref.dtype)


def _proj(x2d, w, bias, *, mode, conv_w=None, conv_b=None, pad_left=0, period=GRID_W,
          tm=512, tn=1024, out_dtype=_f32):
    m, k = x2d.shape
    n = w.shape[1]
    in_specs = [pl.BlockSpec((tm, k), lambda j, i: (i, 0)),
                pl.BlockSpec((k, tn), lambda j, i: (0, j)),
                pl.BlockSpec((1, tn), lambda j, i: (0, j))]
    args = [x2d, w, bias.reshape(1, n)]
    if mode == "conv":
        taps = conv_w.shape[0]
        in_specs += [pl.BlockSpec((taps, tn), lambda j, i: (0, j)),
                     pl.BlockSpec((1, tn), lambda j, i: (0, j))]
        args += [conv_w, conv_b.reshape(1, n)]
    return pl.pallas_call(
        functools.partial(_proj_kernel, mode=mode, pad_left=pad_left, period=period),
        grid=(n // tn, m // tm),
        in_specs=in_specs,
        out_specs=pl.BlockSpec((tm, tn), lambda j, i: (i, j)),
        out_shape=jax.ShapeDtypeStruct((m, n), out_dtype),
        compiler_params=_params(),
        name="proj_" + mode,
    )(*args)


RNN_CHUNK = 256


def _scan8(a, b, reverse):
    rows = a.shape[0]
    pos = lax.broadcasted_iota(jnp.int32, a.shape, 0) % SUBLANES
    for s in (1, 2, 4):
        if reverse:
            a_s = pltpu.roll(a, rows - s, axis=0)
            b_s = pltpu.roll(b, rows - s, axis=0)
            m = pos < SUBLANES - s
        else:
            a_s = pltpu.roll(a, s, axis=0)
            b_s = pltpu.roll(b, s, axis=0)
            m = pos >= s
        b = jnp.where(m, b + a * b_s, b)
        a = jnp.where(m, a * a_s, a)
    return a, b


def _rnn_kernel(u_ref, grg_ref, wf_ref, wb_ref, bf_ref, bb_ref, lam_ref, h0f_ref, h0b_ref,
                y_ref, hf_last_ref, hb_first_ref, hf_scr, a_scr, b_scr):
    seq, tl = u_ref.shape
    c = min(RNN_CHUNK, seq)
    n_chunks = seq // c
    n_groups = c // SUBLANES
    decay = -RG_C * jax.nn.softplus(-lam_ref[...])

    def gates(uc, w_ref, bias_ref, dvec):
        g = _dot(uc.astype(_bf16), w_ref[...]) + bias_ref[...]
        r = jax.nn.sigmoid(g[:, :tl])
        i = jax.nn.sigmoid(g[:, tl:])
        log_a = dvec * r
        a = jnp.exp(log_a)
        b = jnp.sqrt(-jnp.tanh(log_a) * (a * a + 1.0)) * (i * uc)
        return a, b

    def run_direction(reverse, w_ref, bias_ref, dvec, h0, emit):
        def chunk_body(ci, hb):
            cidx = (n_chunks - 1 - ci) if reverse else ci
            row0 = pl.multiple_of(cidx * c, c)
            uc = u_ref[pl.ds(row0, c), :]
            a, b = gates(uc, w_ref, bias_ref, dvec)
            a, b = _scan8(a, b, reverse)
            a_scr[...] = a
            b_scr[...] = b

            def group_body(gi, hb):
                gidx = (n_groups - 1 - gi) if reverse else gi
                r0 = pl.multiple_of(gidx * SUBLANES, SUBLANES)
                h = b_scr[pl.ds(r0, SUBLANES), :] + a_scr[pl.ds(r0, SUBLANES), :] * hb
                b_scr[pl.ds(r0, SUBLANES), :] = h
                edge = h[0:1, :] if reverse else h[SUBLANES - 1:SUBLANES, :]
                return jnp.broadcast_to(edge, h.shape)

            hb = lax.fori_loop(0, n_groups, group_body, hb)
            emit(row0, c)
            return hb

        hb0 = jnp.broadcast_to(h0, (SUBLANES, tl))
        return lax.fori_loop(0, n_chunks, chunk_body, hb0)

    def emit_fwd(row0, c):
        hf_scr[pl.ds(row0, c), :] = b_scr[...]

    def emit_bwd(row0, c):
        h = hf_scr[pl.ds(row0, c), :] + b_scr[...]
        y_ref[pl.ds(row0, c), :] = (h * grg_ref[pl.ds(row0, c), :]).astype(y_ref.dtype)

    hf = run_direction(False, wf_ref, bf_ref, decay[0:1, :], h0f_ref[...], emit_fwd)
    hf_last_ref[...] = hf[0:1, :]
    hb = run_direction(True, wb_ref, bb_ref, decay[1:2, :], h0b_ref[...], emit_bwd)
    hb_first_ref[...] = hb[0:1, :]


def _rglru(u, grg, wf, wb, bias_f, bias_b, lam, h0f, h0b):
    b, l, d = u.shape
    tl = LANE_TILE
    c = min(RNN_CHUNK, l)
    seq_spec = pl.BlockSpec((None, l, tl), lambda i, j: (i, 0, j))
    w_spec = pl.BlockSpec((None, tl, 2 * tl), lambda i, j: (j, 0, 0))
    bias_spec = pl.BlockSpec((None, 1, 2 * tl), lambda i, j: (j, 0, 0))
    vec_spec = pl.BlockSpec((None, 1, tl), lambda i, j: (i, 0, j))
    return pl.pallas_call(
        _rnn_kernel,
        grid=(b, d // tl),
        in_specs=[seq_spec, seq_spec, w_spec, w_spec, bias_spec, bias_spec,
                  pl.BlockSpec((2, tl), lambda i, j: (0, j)), vec_spec, vec_spec],
        out_specs=[seq_spec, vec_spec, vec_spec],
        out_shape=[jax.ShapeDtypeStruct((b, l, d), _bf16),
                   jax.ShapeDtypeStruct((b, 1, d), _f32),
                   jax.ShapeDtypeStruct((b, 1, d), _f32)],
        scratch_shapes=[pltpu.VMEM((l, tl), _f32), pltpu.VMEM((c, tl), _f32),
                        pltpu.VMEM((c, tl), _f32)],
        compiler_params=_params(),
        name="rglru",
    )(u, grg, wf, wb, bias_f, bias_b, lam, h0f, h0b)


def _gate_weights(wa, wx, ba, bx):
    heads_per_tile = LANE_TILE // RNN_HEAD_DIM
    n_tiles = RNN_HEADS // heads_per_tile

    def tile_blockdiag(w):
        w = w.reshape(n_tiles, heads_per_tile, RNN_HEAD_DIM, RNN_HEAD_DIM)
        eye = jnp.eye(heads_per_tile, dtype=w.dtype)
        full = jnp.einsum('thij,hg->thigj', w, eye)
        return full.reshape(n_tiles, LANE_TILE, LANE_TILE)

    w = jnp.concatenate([tile_blockdiag(wa), tile_blockdiag(wx)], axis=-1).astype(_bf16)
    bias = jnp.concatenate([ba.reshape(n_tiles, 1, LANE_TILE), bx.reshape(n_tiles, 1, LANE_TILE)], axis=-1)
    return w, bias.astype(_f32)


def _dft_constants():
    n1 = np.arange(FFT_H)
    k1 = np.arange(FFT_H + 1)
    ang = 2.0 * np.pi * np.outer(k1, n1) / FFT_N1
    fa = np.concatenate([np.cos(ang), -np.sin(ang)[1:FFT_H]], axis=0)
    fa_k = np.kron(fa, np.eye(SUBLANES))
    weight = np.where((k1 == 0) | (k1 == FFT_H), 1.0, 2.0)[:, None] / FFT_N
    fai = np.concatenate([weight * np.cos(ang), (-2.0 / FFT_N) * np.sin(ang)[1:FFT_H]], axis=0).T
    fai_k = np.kron(fai, np.eye(SUBLANES))

    n2 = np.arange(FFT_N2)
    k2 = np.arange(FFT_N2)

    def cs(k1v):
        idx = (np.outer(k2, n2) * FFT_N1 + k1v * n2[None, :]) % FFT_N
        phi = 2.0 * np.pi * idx / FFT_N
        return np.cos(phi), np.sin(phi)

    fb, fbi = [], []
    for k1v in range(1, FFT_H):
        c, s = cs(k1v)
        fb.append(np.block([[c, s], [-s, c]]))
        fbi.append(np.block([[c.T, -s.T], [s.T, c.T]]))
    c0, s0 = cs(0)
    ch, sh = cs(FFT_H)
    z = np.zeros_like(c0)
    fbs = np.block([[c0, z], [-s0, z], [z, ch], [z, -sh]])
    fbsi = np.block([[c0.T, -s0.T, z, z], [z, z, ch.T, -sh.T]])
    as32 = lambda a: np.asarray(a, np.float32)
    return as32(fa_k), as32(fai_k), as32(np.stack(fb)), as32(np.stack(fbi)), as32(fbs), as32(fbsi)


def _hyena_kernel(z_ref, gate_ref, kf_ref, skip_ref, fa_ref, fai_ref, fb_ref, fbi_ref, fbs_ref, fbsi_ref,
                  o_ref, w_scr):
    tl = z_ref.shape[-1]
    n_groups = FFT_N2 // SUBLANES
    half = FFT_N2

    def stage_a(g, carry):
        r0 = pl.multiple_of(g * SUBLANES, SUBLANES)
        zin = z_ref[:, pl.ds(r0, SUBLANES), :].reshape(FFT_H * SUBLANES, tl)
        y = _dot(fa_ref[...], zin.astype(_bf16))
        w_scr[:, pl.ds(r0, SUBLANES), :] = y.reshape(FFT_N1, SUBLANES, tl)
        return carry

    lax.fori_loop(0, n_groups, stage_a, 0)

    yin = jnp.concatenate([w_scr[0], w_scr[FFT_H]], axis=0).astype(_bf16)
    x = _dot(fbs_ref[...], yin)
    pieces = []
    for q in range(2):
        xr = x[(2 * q) * half:(2 * q + 1) * half]
        xi = x[(2 * q + 1) * half:(2 * q + 2) * half]
        kr = kf_ref[pl.ds((2 * q) * half, half), :]
        ki = kf_ref[pl.ds((2 * q + 1) * half, half), :]
        pieces += [xr * kr - xi * ki, xr * ki + xi * kr]
    v = _dot(fbsi_ref[...], jnp.concatenate(pieces, axis=0).astype(_bf16))
    w_scr[0] = v[:half]
    w_scr[FFT_H] = v[half:]

    def stage_b(k1, carry):
        yin = jnp.concatenate([w_scr[k1], w_scr[FFT_H + k1]], axis=0).astype(_bf16)
        x = _dot(fb_ref[k1 - 1], yin)
        xr, xi = x[:half], x[half:]
        base = pl.multiple_of(2 * half * (k1 + 1), 2 * half)
        kr = kf_ref[pl.ds(base, half), :]
        ki = kf_ref[pl.ds(base + half, half), :]
        p = jnp.concatenate([xr * kr - xi * ki, xr * ki + xi * kr], axis=0).astype(_bf16)
        v = _dot(fbi_ref[k1 - 1], p)
        w_scr[k1] = v[:half]
        w_scr[FFT_H + k1] = v[half:]
        return carry

    lax.fori_loop(1, FFT_H, stage_b, 0)

    skip = skip_ref[...]

    def stage_c(g, carry):
        r0 = pl.multiple_of(g * SUBLANES, SUBLANES)
        vin = w_scr[:, pl.ds(r0, SUBLANES), :].reshape(FFT_N1 * SUBLANES, tl)
        conv = _dot(fai_ref[...], vin.astype(_bf16)).reshape(FFT_H, SUBLANES, tl)
        zin = z_ref[:, pl.ds(r0, SUBLANES), :]
        out = gate_ref[:, pl.ds(r0, SUBLANES), :] * (conv + skip * zin)
        o_ref[:, pl.ds(r0, SUBLANES), :] = out.astype(o_ref.dtype)
        return carry

    lax.fori_loop(0, n_groups, stage_c, 0)


def _single(shape, index_map):
    return pl.BlockSpec(shape, index_map, pipeline_mode=pl.Buffered(1))


def _hyena_order(z_arr, z_col, gate_arr, gate_col, kf, skip, consts, out_dtype):
    fa_k, fai_k, fb, fbi, fbs, fbsi = consts
    b = z_arr.shape[0]
    tl = LANE_TILE
    d_out = kf.shape[1]
    seq_block = (None, FFT_H, FFT_N2, tl)
    const2 = lambda shape: _single(shape, lambda j, i: (0, 0))
    const3 = lambda shape: _single(shape, lambda j, i: (0, 0, 0))
    return pl.pallas_call(
        _hyena_kernel,
        grid=(d_out // tl, b),
        in_specs=[pl.BlockSpec(seq_block, lambda j, i: (i, 0, 0, z_col + j)),
                  pl.BlockSpec(seq_block, lambda j, i: (i, 0, 0, gate_col + j)),
                  _single((kf.shape[0], tl), lambda j, i: (0, j)),
                  pl.BlockSpec((1, tl), lambda j, i: (0, j)),
                  const2(fa_k.shape), const2(fai_k.shape), const3(fb.shape), const3(fbi.shape),
                  const2(fbs.shape), const2(fbsi.shape)],
        out_specs=pl.BlockSpec(seq_block, lambda j, i: (i, 0, 0, j)),
        out_shape=jax.ShapeDtypeStruct((b, FFT_H, FFT_N2, d_out), out_dtype),
        scratch_shapes=[pltpu.VMEM((FFT_N1, FFT_N2, tl), _f32)],
        compiler_params=_params(),
        name="hyena_conv",
    )(z_arr, gate_arr, kf, skip, fa_k, fai_k, fb, fbi, fbs, fbsi)


def _hyena_filter_spectra(seq, hy_w1, hy_b1, hy_w2, hy_b2, hy_w3, hy_b3, hy_freq, hy_w4):
    t = jnp.linspace(0.0, 1.0, seq, dtype=_f32)[:, None]
    w = 2.0 * math.pi * jnp.arange(seq, dtype=_f32)[:, None] / seq
    f = jnp.linspace(1e-4, HY_BANDS - 1, HY_BANDS, dtype=_f32)[None, :]
    z = jnp.concatenate([t, jnp.cos(f * w), -jnp.sin(f * w)], axis=-1)
    hi = lax.Precision.HIGHEST
    hdn = jnp.sin(hy_freq * (jnp.dot(z, hy_w1, precision=hi) + hy_b1))
    hdn = jnp.sin(hy_freq * (jnp.dot(hdn, hy_w2, precision=hi) + hy_b2))
    hdn = jnp.sin(hy_freq * (jnp.dot(hdn, hy_w3, precision=hi) + hy_b3))
    k = jnp.dot(hdn, hy_w4, precision=hi).reshape(seq, HY_ORDER, 2, D_MODEL)
    max_decay = math.log(HY_TARGET) / HY_FAST_DECAY
    min_decay = math.log(HY_TARGET) / HY_SLOW_DECAY
    deltas = jnp.abs(jnp.linspace(min_decay, max_decay, D_MODEL, dtype=_f32))
    k = k * jnp.exp(-t[:, :, None, None] * deltas)
    fwd, bwd = k[:, :, 0], k[:, :, 1]
    full = jnp.concatenate([fwd, jnp.zeros((1, HY_ORDER, D_MODEL), _f32), jnp.flip(bwd[1:], 0)], axis=0)
    full = full / jnp.sum(jnp.abs(full), axis=0, keepdims=True)
    spec = jnp.fft.fft(full, axis=0)
    k2 = np.arange(FFT_N2)
    rows = []
    for k1 in (0, FFT_H):
        rows += [(k1 + FFT_N1 * k2, False), (k1 + FFT_N1 * k2, True)]
    for k1 in range(1, FFT_H):
        rows += [(k1 + FFT_N1 * k2, False), (k1 + FFT_N1 * k2, True)]
    parts = [(jnp.imag(spec[idx]) if im else jnp.real(spec[idx])) for idx, im in rows]
    kf = jnp.concatenate(parts, axis=0).astype(_f32)
    return kf[:, 0], kf[:, 1]


def _merge_kernel(hx_ref, yr_ref, yh_ref, x_ref, wbg_ref, bbg_ref, wa_ref, wb_ref, wo_ref, g1_ref,
                  g_ref, sh_ref, sc_ref, x1_ref, h2_ref):
    d = x_ref.shape[-1]
    gates = jax.nn.sigmoid(_dot(hx_ref[...], wbg_ref[...]) + bbg_ref[...])
    ya = _dot(yr_ref[...], wa_ref[...])
    yb = _dot(yh_ref[...].astype(_bf16), wb_ref[...])
    merged = gates[:, :d] * ya + gates[:, d:] * yb
    x1 = x_ref[...] + g1_ref[...] * _dot(merged.astype(_bf16), wo_ref[...])
    x1_ref[...] = x1
    y = x1 * lax.rsqrt(jnp.mean(x1 * x1, axis=-1, keepdims=True) + RMS_EPS) * g_ref[...]
    h2_ref[...] = (y * (1.0 + sc_ref[...]) + sh_ref[...]).astype(h2_ref.dtype)


def _merge(hx, y_rnn, y_hy, x, w_bg, b_bg, w_a, w_b, w_o, g1, norm2_g, sh2, sc2, tm=512):
    b, l, d = x.shape
    row = lambda width: pl.BlockSpec((None, tm, width), lambda i, j: (i, j, 0))
    vec = pl.BlockSpec((None, 1, d), lambda i, j: (i, 0, 0))
    full = lambda a: pl.BlockSpec(a.shape, lambda i, j: (0,) * a.ndim)
    b_bg = b_bg.reshape(1, -1)
    norm2_g = norm2_g.reshape(1, d)
    return pl.pallas_call(
        _merge_kernel,
        grid=(b, l // tm),
        in_specs=[row(d), row(d), row(d), row(d), full(w_bg), full(b_bg), full(w_a), full(w_b), full(w_o),
                  vec, full(norm2_g), vec, vec],
        out_specs=[row(d), row(d)],
        out_shape=[jax.ShapeDtypeStruct((b, l, d), _f32), jax.ShapeDtypeStruct((b, l, d), _bf16)],
        compiler_params=_params(),
        name="merge",
    )(hx, y_rnn, y_hy, x, w_bg, b_bg, w_a, w_b, w_o, g1, norm2_g, sh2, sc2)


FFN_CHUNK = 256


def _ffn_kernel(h_ref, x1_ref, wg_ref, wu_ref, wo_ref, g2_ref, fg_ref, o_ref, acc_ref):
    h = h_ref[...]
    n_chunks = wg_ref.shape[0]
    acc_ref[...] = jnp.zeros_like(acc_ref)

    def body(ci, carry):
        g = _dot(h, wg_ref[ci])
        u = _dot(h, wu_ref[ci])
        a = (g * jax.nn.sigmoid(g) * u).astype(_bf16)
        acc_ref[...] += _dot(a, wo_ref[ci])
        return carry

    lax.fori_loop(0, n_chunks, body, 0)
    x2 = x1_ref[...] + g2_ref[...] * acc_ref[...]
    y = x2 * lax.rsqrt(jnp.mean(x2 * x2, axis=-1, keepdims=True) + RMS_EPS)
    o_ref[...] = y * fg_ref[...]


def _ffn(h2, x1, w_g, w_u, w_o, g2, final_g, tm=512):
    b, l, d = x1.shape
    n_chunks = w_g.shape[1] // FFN_CHUNK
    w_g = w_g.reshape(d, n_chunks, FFN_CHUNK).transpose(1, 0, 2)
    w_u = w_u.reshape(d, n_chunks, FFN_CHUNK).transpose(1, 0, 2)
    w_o = w_o.reshape(n_chunks, FFN_CHUNK, d)
    row = pl.BlockSpec((None, tm, d), lambda i, j: (i, j, 0))
    vec = pl.BlockSpec((None, 1, d), lambda i, j: (i, 0, 0))
    full = lambda a: pl.BlockSpec(a.shape, lambda i, j: (0,) * a.ndim)
    final_g = final_g.reshape(1, d)
    return pl.pallas_call(
        _ffn_kernel,
        grid=(b, l // tm),
        in_specs=[row, row, full(w_g), full(w_u), full(w_o), vec, full(final_g)],
        out_specs=row,
        out_shape=jax.ShapeDtypeStruct((b, l, d), _f32),
        scratch_shapes=[pltpu.VMEM((tm, d), _f32)],
        compiler_params=_params(),
        name="ffn",
    )(h2, x1, w_g, w_u, w_o, g2, final_g)


def kernel(x, c, ctx, c_ctx, w_mod, b_mod, norm1_g, norm2_g, w_in, b_in, rnn_conv_w, rnn_conv_b, rg_wa, rg_ba,
           rg_wx, rg_bx, rg_lambda, hy_conv_w, hy_conv_b, hy_w1, hy_b1, hy_w2, hy_b2, hy_w3, hy_b3, hy_freq,
           hy_w4, hy_skip, w_a_out, w_b_out, w_out, w_ffn_in, w_ffn_out, final_g):
    assert w_mod.shape[0] == 1, "single-layer block"
    b, seq, d = x.shape
    ctx_len = ctx.shape[1]
    assert seq == FFT_H * FFT_N2 and d == D_MODEL
    (w_mod, b_mod, norm1_g, norm2_g, w_in, b_in, rnn_conv_w, rnn_conv_b, rg_wa, rg_ba, rg_wx, rg_bx,
     rg_lambda, hy_conv_w, hy_conv_b, hy_w1, hy_b1, hy_w2, hy_b2, hy_w3, hy_b3, hy_freq, hy_w4, hy_skip,
     w_a_out, w_b_out, w_out, w_ffn_in, w_ffn_out) = [
        a[0] for a in (w_mod, b_mod, norm1_g, norm2_g, w_in, b_in, rnn_conv_w, rnn_conv_b, rg_wa, rg_ba,
                       rg_wx, rg_bx, rg_lambda, hy_conv_w, hy_conv_b, hy_w1, hy_b1, hy_w2, hy_b2, hy_w3,
                       hy_b3, hy_freq, hy_w4, hy_skip, w_a_out, w_b_out, w_out, w_ffn_in, w_ffn_out)]

    pad_rows = 2 * SUBLANES - b - 1
    c_all = jnp.concatenate([c, c_ctx[None, :], jnp.zeros((pad_rows, d), _f32)], axis=0)
    mod = _mod_vectors(c_all, w_mod, b_mod)
    sh1, sc1, g1, sh2, sc2, g2 = [m.reshape(b, 1, d) for m in jnp.split(mod[:b], N_MOD, axis=-1)]
    csh1, csc1 = [jnp.broadcast_to(m.reshape(1, 1, d), (b, 1, d))
                  for m in jnp.split(mod[b], N_MOD, axis=-1)[:2]]

    w_in_bf = w_in.astype(_bf16)
    w_rx, b_rx = w_in_bf[:, :d], b_in[:d]
    w_rg, b_rg = w_in_bf[:, d:2 * d], b_in[d:2 * d]
    w_hy, b_hy = w_in_bf[:, 2 * d:5 * d], b_in[2 * d:5 * d]
    w_bg, b_bg = w_in_bf[:, 5 * d:], b_in[5 * d:]
    gate_wf, gate_bf = _gate_weights(rg_wa[0], rg_wx[0], rg_ba[0], rg_bx[0])
    gate_wb, gate_bb = _gate_weights(rg_wa[1], rg_wx[1], rg_ba[1], rg_bx[1])
    zeros_state = jnp.zeros((b, 1, d), _f32)

    hc = _norm_mod(ctx, norm1_g, csh1, csc1, tm=ctx_len)
    u_c = _proj(hc.reshape(b * ctx_len, d), w_rx, b_rx, mode="conv", conv_w=rnn_conv_w, conv_b=rnn_conv_b,
                pad_left=RNN_CONV_PAD_LEFT, period=ctx_len, tm=ctx_len).reshape(b, ctx_len, d)
    _, cf, cb = _rglru(u_c, jnp.zeros_like(u_c), gate_wf, gate_wb, gate_bf, gate_bb, rg_lambda,
                       zeros_state, zeros_state)

    hx = _norm_mod(x, norm1_g, sh1, sc1, tm=1024)
    hx2d = hx.reshape(b * seq, d)
    u = _proj(hx2d, w_rx, b_rx, mode="conv", conv_w=rnn_conv_w, conv_b=rnn_conv_b,
              pad_left=RNN_CONV_PAD_LEFT).reshape(b, seq, d)
    grg = _proj(hx2d, w_rg, b_rg, mode="gelu").reshape(b, seq, d)
    q = _proj(hx2d, w_hy, b_hy, mode="conv", conv_w=hy_conv_w, conv_b=hy_conv_b,
              pad_left=HY_CONV_PAD_LEFT)
    y_rnn, _, _ = _rglru(u, grg, gate_wf, gate_wb, gate_bf, gate_bb, rg_lambda, cf, cb)

    consts = [jnp.asarray(a).astype(_bf16) for a in _dft_constants()]
    kf1, kf2 = _hyena_filter_spectra(seq, hy_w1, hy_b1, hy_w2, hy_b2, hy_w3, hy_b3, hy_freq, hy_w4)
    q4 = q.reshape(b, FFT_H, FFT_N2, 3 * d)
    n_tiles = d // LANE_TILE
    z1 = _hyena_order(q4, 0, q4, n_tiles, kf1, hy_skip[0:1], consts, _f32)
    y_hy = _hyena_order(z1, 0, q4, 2 * n_tiles, kf2, hy_skip[1:2], consts, _f32)
    y_hy = y_hy.reshape(b, seq, d)

    x1, h2 = _merge(hx, y_rnn, y_hy, x, w_bg, b_bg, w_a_out.astype(_bf16), w_b_out.astype(_bf16),
                    w_out.astype(_bf16), g1, norm2_g, sh2, sc2)
    w_ffn_bf = w_ffn_in.astype(_bf16)
    return _ffn(h2, x1, w_ffn_bf[:, :D_FF], w_ffn_bf[:, D_FF:], w_ffn_out.astype(_bf16), g2, final_g)
```

```python
import functools
import math

import numpy as np
import jax
import jax.numpy as jnp
from jax import lax
from jax.experimental import pallas as pl
from jax.experimental.pallas import tpu as pltpu

D_MODEL = 1024
GRID_W = 64
N_MOD = 6
RMS_EPS = 1e-6
RNN_HEADS = 16
RNN_HEAD_DIM = D_MODEL // RNN_HEADS
RNN_CONV_PAD_LEFT = 2
HY_CONV_PAD_LEFT = 1
RG_C = 8.0
HY_ORDER = 2
HY_EMB = 33
HY_BANDS = (HY_EMB - 1) // 2
HY_FAST_DECAY = 0.3
HY_SLOW_DECAY = 1.5
HY_TARGET = 1e-2
D_FF = ((8 * D_MODEL // 3 + 255) // 256) * 256

SUBLANES = 8
LANE_TILE = 256
VMEM_LIMIT = 56 * 1024 * 1024

FFT_N1 = 64
FFT_N2 = 128
FFT_N = FFT_N1 * FFT_N2
FFT_H = FFT_N1 // 2

_f32 = jnp.float32
_bf16 = jnp.bfloat16


def _params(**kw):
    return pltpu.CompilerParams(vmem_limit_bytes=VMEM_LIMIT, **kw)


def _dot(a, b):
    return jnp.dot(a, b, preferred_element_type=_f32)


def _sigmoid(x):
    return 0.5 * jnp.tanh(0.5 * x) + 0.5


def _mod_kernel(c_ref, w_ref, b_ref, o_ref):
    c = c_ref[...]
    s = c * _sigmoid(c)
    o_ref[...] = jnp.dot(s, w_ref[...], preferred_element_type=_f32,
                         precision=lax.Precision.HIGHEST) + b_ref[...]


def _mod_vectors(c_all, w_mod, b_mod):
    rows, d = c_all.shape
    n = w_mod.shape[1]
    tn = 1024
    return pl.pallas_call(
        _mod_kernel,
        grid=(n // tn,),
        in_specs=[pl.BlockSpec((rows, d), lambda j: (0, 0)),
                  pl.BlockSpec((d, tn), lambda j: (0, j)),
                  pl.BlockSpec((1, tn), lambda j: (0, j))],
        out_specs=pl.BlockSpec((rows, tn), lambda j: (0, j)),
        out_shape=jax.ShapeDtypeStruct((rows, n), _f32),
        compiler_params=_params(),
        name="mod_vectors",
    )(c_all, w_mod, b_mod.reshape(1, n))


def _norm_mod_kernel(x_ref, g_ref, sh_ref, sc_ref, o_ref):
    x = x_ref[...]
    y = x * lax.rsqrt(jnp.mean(x * x, axis=-1, keepdims=True) + RMS_EPS)
    y = y * g_ref[...]
    o_ref[...] = (y * (1.0 + sc_ref[...]) + sh_ref[...]).astype(o_ref.dtype)


def _norm_mod(x, g, sh, sc, tm):
    b, l, d = x.shape
    vec = pl.BlockSpec((None, 1, d), lambda i, j: (i, 0, 0))
    return pl.pallas_call(
        _norm_mod_kernel,
        grid=(b, l // tm),
        in_specs=[pl.BlockSpec((None, tm, d), lambda i, j: (i, j, 0)),
                  pl.BlockSpec((1, d), lambda i, j: (0, 0)), vec, vec],
        out_specs=pl.BlockSpec((None, tm, d), lambda i, j: (i, j, 0)),
        out_shape=jax.ShapeDtypeStruct((b, l, d), _bf16),
        compiler_params=_params(),
        name="norm_mod",
    )(x, g.reshape(1, d), sh, sc)


def _short_conv(y, w_ref, cb_ref, pad_left, period):
    rows = y.shape[0]
    pos = lax.broadcasted_iota(jnp.int32, y.shape, 0) % period
    out = jnp.broadcast_to(cb_ref[...], y.shape)
    for k in range(w_ref.shape[0]):
        off = k - pad_left
        if off == 0:
            tap = y
        else:
            tap = pltpu.roll(y, (-off) % rows, axis=0)
            valid = (pos + off >= 0) & (pos + off < period)
            tap = jnp.where(valid, tap, 0.0)
        out = out + tap * w_ref[k:k + 1, :]
    return out


def _proj_kernel(x_ref, w_ref, b_ref, *rest, mode, pad_left, period):
    o_ref = rest[-1]
    y = _dot(x_ref[...], w_ref[...]) + b_ref[...]
    if mode == "conv":
        y = _short_conv(y, rest[0], rest[1], pad_left, period)
    elif mode == "gelu":
        y = jax.nn.gelu(y)
    o_ref[...] = y.astype(o_ref.dtype)


def _proj(x2d, w, bias, *, mode, conv_w=None, conv_b=None, pad_left=0, period=GRID_W,
          tm=512, tn=1024, out_dtype=_f32):
    m, k = x2d.shape
    n = w.shape[1]
    in_specs = [pl.BlockSpec((tm, k), lambda j, i: (i, 0)),
                pl.BlockSpec((k, tn), lambda j, i: (0, j)),
                pl.BlockSpec((1, tn), lambda j, i: (0, j))]
    args = [x2d, w, bias.reshape(1, n)]
    if mode == "conv":
        taps = conv_w.shape[0]
        in_specs += [pl.BlockSpec((taps, tn), lambda j, i: (0, j)),
                     pl.BlockSpec((1, tn), lambda j, i: (0, j))]
        args += [conv_w, conv_b.reshape(1, n)]
    return pl.pallas_call(
        functools.partial(_proj_kernel, mode=mode, pad_left=pad_left, period=period),
        grid=(n // tn, m // tm),
        in_specs=in_specs,
        out_specs=pl.BlockSpec((tm, tn), lambda j, i: (i, j)),
        out_shape=jax.ShapeDtypeStruct((m, n), out_dtype),
        compiler_params=_params(),
        name="proj_" + mode,
    )(*args)


RNN_CHUNK = 256


def _scan8(a, b, reverse):
    rows = a.shape[0]
    pos = lax.broadcasted_iota(jnp.int32, a.shape, 0) % SUBLANES
    for s in (1, 2, 4):
        if reverse:
            a_s = pltpu.roll(a, rows - s, axis=0)
            b_s = pltpu.roll(b, rows - s, axis=0)
            m = pos < SUBLANES - s
        else:
            a_s = pltpu.roll(a, s, axis=0)
            b_s = pltpu.roll(b, s, axis=0)
            m = pos >= s
        b = jnp.where(m, b + a * b_s, b)
        a = jnp.where(m, a * a_s, a)
    return a, b


def _rnn_kernel(u_ref, grg_ref, wf_ref, wb_ref, bf_ref, bb_ref, lam_ref, h0f_ref, h0b_ref,
                y_ref, hf_last_ref, hb_first_ref, hf_scr, a_scr, b_scr):
    seq, tl = u_ref.shape
    c = min(RNN_CHUNK, seq)
    n_chunks = seq // c
    n_groups = c // SUBLANES
    decay = -RG_C * jax.nn.softplus(-lam_ref[...])

    def gates(uc, w_ref, bias_ref, dvec):
        g = _dot(uc.astype(_bf16), w_ref[...]) + bias_ref[...]
        r = _sigmoid(g[:, :tl])
        i = _sigmoid(g[:, tl:])
        log_a = dvec * r
        a = jnp.exp(log_a)
        b = jnp.sqrt(-jnp.tanh(log_a) * (a * a + 1.0)) * (i * uc)
        return a, b

    def run_direction(reverse, w_ref, bias_ref, dvec, h0, emit):
        def chunk_body(ci, hb):
            cidx = (n_chunks - 1 - ci) if reverse else ci
            row0 = pl.multiple_of(cidx * c, c)
            uc = u_ref[pl.ds(row0, c), :]
            a, b = gates(uc, w_ref, bias_ref, dvec)
            a, b = _scan8(a, b, reverse)
            a_scr[...] = a
            b_scr[...] = b

            def group_body(gi, hb):
                gidx = (n_groups - 1 - gi) if reverse else gi
                r0 = pl.multiple_of(gidx * SUBLANES, SUBLANES)
                h = b_scr[pl.ds(r0, SUBLANES), :] + a_scr[pl.ds(r0, SUBLANES), :] * hb
                b_scr[pl.ds(r0, SUBLANES), :] = h
                edge = h[0:1, :] if reverse else h[SUBLANES - 1:SUBLANES, :]
                return jnp.broadcast_to(edge, h.shape)

            hb = lax.fori_loop(0, n_groups, group_body, hb)
            emit(row0, c)
            return hb

        hb0 = jnp.broadcast_to(h0, (SUBLANES, tl))
        return lax.fori_loop(0, n_chunks, chunk_body, hb0)

    def emit_fwd(row0, c):
        hf_scr[pl.ds(row0, c), :] = b_scr[...]

    def emit_bwd(row0, c):
        h = hf_scr[pl.ds(row0, c), :] + b_scr[...]
        y_ref[pl.ds(row0, c), :] = (h * grg_ref[pl.ds(row0, c), :]).astype(y_ref.dtype)

    hf = run_direction(False, wf_ref, bf_ref, decay[0:1, :], h0f_ref[...], emit_fwd)
    hf_last_ref[...] = hf[0:1, :]
    hb = run_direction(True, wb_ref, bb_ref, decay[1:2, :], h0b_ref[...], emit_bwd)
    hb_first_ref[...] = hb[0:1, :]


def _rglru(u, grg, wf, wb, bias_f, bias_b, lam, h0f, h0b):
    b, l, d = u.shape
    tl = LANE_TILE
    c = min(RNN_CHUNK, l)
    seq_spec = pl.BlockSpec((None, l, tl), lambda i, j: (i, 0, j))
    w_spec = pl.BlockSpec((None, tl, 2 * tl), lambda i, j: (j, 0, 0))
    bias_spec = pl.BlockSpec((None, 1, 2 * tl), lambda i, j: (j, 0, 0))
    vec_spec = pl.BlockSpec((None, 1, tl), lambda i, j: (i, 0, j))
    return pl.pallas_call(
        _rnn_kernel,
        grid=(b, d // tl),
        in_specs=[seq_spec, seq_spec, w_spec, w_spec, bias_spec, bias_spec,
                  pl.BlockSpec((2, tl), lambda i, j: (0, j)), vec_spec, vec_spec],
        out_specs=[seq_spec, vec_spec, vec_spec],
        out_shape=[jax.ShapeDtypeStruct((b, l, d), _bf16),
                   jax.ShapeDtypeStruct((b, 1, d), _f32),
                   jax.ShapeDtypeStruct((b, 1, d), _f32)],
        scratch_shapes=[pltpu.VMEM((l, tl), _f32), pltpu.VMEM((c, tl), _f32),
                        pltpu.VMEM((c, tl), _f32)],
        compiler_params=_params(),
        name="rglru",
    )(u, grg, wf, wb, bias_f, bias_b, lam, h0f, h0b)


def _gate_weights(wa, wx, ba, bx):
    heads_per_tile = LANE_TILE // RNN_HEAD_DIM
    n_tiles = RNN_HEADS // heads_per_tile

    def tile_blockdiag(w):
        w = w.reshape(n_tiles, heads_per_tile, RNN_HEAD_DIM, RNN_HEAD_DIM)
        eye = jnp.eye(heads_per_tile, dtype=w.dtype)
        full = jnp.einsum('thij,hg->thigj', w, eye)
        return full.reshape(n_tiles, LANE_TILE, LANE_TILE)

    w = jnp.concatenate([tile_blockdiag(wa), tile_blockdiag(wx)], axis=-1).astype(_bf16)
    bias = jnp.concatenate([ba.reshape(n_tiles, 1, LANE_TILE), bx.reshape(n_tiles, 1, LANE_TILE)], axis=-1)
    return w, bias.astype(_f32)


HY_UNROLL_A = 2
HY_UNROLL_B = 4


def _dft_constants():
    n1 = np.arange(FFT_H)
    k1 = np.arange(FFT_H + 1)
    ang = 2.0 * np.pi * np.outer(k1, n1) / FFT_N1
    fa = np.concatenate([np.cos(ang), -np.sin(ang)[1:FFT_H]], axis=0)
    fa_k = np.kron(fa, np.eye(SUBLANES))
    weight = np.where((k1 == 0) | (k1 == FFT_H), 1.0, 2.0)[:, None] / FFT_N
    fai = np.concatenate([weight * np.cos(ang), (-2.0 / FFT_N) * np.sin(ang)[1:FFT_H]], axis=0).T
    fai_k = np.kron(fai, np.eye(SUBLANES))

    n2 = np.arange(FFT_N2)
    k2 = np.arange(FFT_N2)

    def cs(k1v):
        idx = (np.outer(k2, n2) * FFT_N1 + k1v * n2[None, :]) % FFT_N
        phi = 2.0 * np.pi * idx / FFT_N
        return np.cos(phi), np.sin(phi)

    fb, fbi = [], []
    for k1v in range(1, FFT_H):
        c, s = cs(k1v)
        fb.append(np.block([[c, s], [-s, c]]))
        fbi.append(np.block([[c.T, -s.T], [s.T, c.T]]))
    c0, s0 = cs(0)
    ch, sh = cs(FFT_H)
    z = np.zeros_like(c0)
    fbs = np.block([[c0, z], [-s0, z], [z, ch], [z, -sh]])
    fbsi = np.block([[c0.T, -s0.T, z, z], [z, z, ch.T, -sh.T]])
    as32 = lambda a: np.asarray(a, np.float32)
    return as32(fa_k), as32(fai_k), as32(np.stack(fb)), as32(np.stack(fbi)), as32(fbs), as32(fbsi)


def _hyena_kernel(z_ref, gate_ref, kf_ref, skip_ref, fa_ref, fai_ref, fb_ref, fbi_ref, fbs_ref, fbsi_ref,
                  o_ref, w_scr):
    tl = z_ref.shape[-1]
    n_groups = FFT_N2 // SUBLANES
    half = FFT_N2

    def stage_a(g, carry):
        r0 = pl.multiple_of(g * SUBLANES, SUBLANES)
        zin = z_ref[:, pl.ds(r0, SUBLANES), :].reshape(FFT_H * SUBLANES, tl)
        y = _dot(fa_ref[...], zin.astype(_bf16))
        w_scr[:, pl.ds(r0, SUBLANES), :] = y.reshape(FFT_N1, SUBLANES, tl)
        return carry

    lax.fori_loop(0, n_groups, stage_a, 0, unroll=HY_UNROLL_A)

    yin = jnp.concatenate([w_scr[0], w_scr[FFT_H]], axis=0).astype(_bf16)
    x = _dot(fbs_ref[...], yin)
    pieces = []
    for q in range(2):
        xr = x[(2 * q) * half:(2 * q + 1) * half]
        xi = x[(2 * q + 1) * half:(2 * q + 2) * half]
        kr = kf_ref[pl.ds((2 * q) * half, half), :]
        ki = kf_ref[pl.ds((2 * q + 1) * half, half), :]
        pieces += [xr * kr - xi * ki, xr * ki + xi * kr]
    v = _dot(fbsi_ref[...], jnp.concatenate(pieces, axis=0).astype(_bf16))
    w_scr[0] = v[:half]
    w_scr[FFT_H] = v[half:]

    def stage_b(k1, carry):
        yin = jnp.concatenate([w_scr[k1], w_scr[FFT_H + k1]], axis=0).astype(_bf16)
        x = _dot(fb_ref[k1 - 1], yin)
        xr, xi = x[:half], x[half:]
        base = pl.multiple_of(2 * half * (k1 + 1), 2 * half)
        kr = kf_ref[pl.ds(base, half), :]
        ki = kf_ref[pl.ds(base + half, half), :]
        p = jnp.concatenate([xr * kr - xi * ki, xr * ki + xi * kr], axis=0).astype(_bf16)
        v = _dot(fbi_ref[k1 - 1], p)
        w_scr[k1] = v[:half]
        w_scr[FFT_H + k1] = v[half:]
        return carry

    lax.fori_loop(1, FFT_H, stage_b, 0, unroll=HY_UNROLL_B)

    skip = skip_ref[...]

    def stage_c(g, carry):
        r0 = pl.multiple_of(g * SUBLANES, SUBLANES)
        vin = w_scr[:, pl.ds(r0, SUBLANES), :].reshape(FFT_N1 * SUBLANES, tl)
        conv = _dot(fai_ref[...], vin.astype(_bf16)).reshape(FFT_H, SUBLANES, tl)
        zin = z_ref[:, pl.ds(r0, SUBLANES), :]
        out = gate_ref[:, pl.ds(r0, SUBLANES), :] * (conv + skip * zin)
        o_ref[:, pl.ds(r0, SUBLANES), :] = out.astype(o_ref.dtype)
        return carry

    lax.fori_loop(0, n_groups, stage_c, 0, unroll=HY_UNROLL_A)


def _single(shape, index_map):
    return pl.BlockSpec(shape, index_map, pipeline_mode=pl.Buffered(1))


def _hyena_order(z_arr, z_col, gate_arr, gate_col, kf, order, skip, consts, out_dtype):
    fa_k, fai_k, fb, fbi, fbs, fbsi = consts
    b = z_arr.shape[0]
    tl = LANE_TILE
    d_out = kf.shape[-1]
    seq_block = (None, FFT_H, FFT_N2, tl)
    const2 = lambda shape: _single(shape, lambda j, i: (0, 0))
    const3 = lambda shape: _single(shape, lambda j, i: (0, 0, 0))
    return pl.pallas_call(
        _hyena_kernel,
        grid=(d_out // tl, b),
        in_specs=[pl.BlockSpec(seq_block, lambda j, i: (i, 0, 0, z_col + j)),
                  pl.BlockSpec(seq_block, lambda j, i: (i, 0, 0, gate_col + j)),
                  _single((None, kf.shape[1], tl), lambda j, i: (order, 0, j)),
                  pl.BlockSpec((1, tl), lambda j, i: (0, j)),
                  const2(fa_k.shape), const2(fai_k.shape), const3(fb.shape), const3(fbi.shape),
                  const2(fbs.shape), const2(fbsi.shape)],
        out_specs=pl.BlockSpec(seq_block, lambda j, i: (i, 0, 0, j)),
        out_shape=jax.ShapeDtypeStruct((b, FFT_H, FFT_N2, d_out), out_dtype),
        scratch_shapes=[pltpu.VMEM((FFT_N1, FFT_N2, tl), _f32)],
        compiler_params=_params(),
        name="hyena_conv",
    )(z_arr, gate_arr, kf, skip, fa_k, fai_k, fb, fbi, fbs, fbsi)


def _filter_kernel(hdn_ref, w4f_ref, w4b_ref, delta_ref, fa_ref, fb_ref, fbs_ref, o_ref, sig_scr, w_scr):
    seq = hdn_ref.shape[0]
    tl = o_ref.shape[-1]
    half = FFT_N2
    hi = lax.Precision.HIGHEST
    hdn = hdn_ref[...]
    row = lax.broadcasted_iota(jnp.int32, (seq, tl), 0)
    decay = jnp.exp(-(row.astype(_f32) * (1.0 / (seq - 1))) * delta_ref[...])
    f = jnp.dot(hdn, w4f_ref[...], preferred_element_type=_f32, precision=hi) * decay
    g = jnp.dot(hdn, w4b_ref[...], preferred_element_type=_f32, precision=hi) * decay
    g = jnp.where(row == 0, 0.0, g)
    norm = jnp.sum(jnp.abs(f), axis=0, keepdims=True) + jnp.sum(jnp.abs(g), axis=0, keepdims=True)
    f = f / norm
    g = g / norm

    for part, sig in enumerate((f + g, f - g)):
        sig_scr[...] = sig.reshape(FFT_H, FFT_N2, tl)

        def stage_a(gi, carry):
            r0 = pl.multiple_of(gi * SUBLANES, SUBLANES)
            zin = sig_scr[:, pl.ds(r0, SUBLANES), :].reshape(FFT_H * SUBLANES, tl)
            y = _dot(fa_ref[...], zin.astype(_bf16))
            w_scr[:, pl.ds(r0, SUBLANES), :] = y.reshape(FFT_N1, SUBLANES, tl)
            return carry

        lax.fori_loop(0, FFT_N2 // SUBLANES, stage_a, 0)

        yin = jnp.concatenate([w_scr[0], w_scr[FFT_H]], axis=0).astype(_bf16)
        o_ref[pl.ds(part * half, half), :] = _dot(fbs_ref[pl.ds(part * half, half), :], yin)
        o_ref[pl.ds((2 + part) * half, half), :] = _dot(fbs_ref[pl.ds((2 + part) * half, half), :], yin)

        def stage_b(k1, carry):
            yin = jnp.concatenate([w_scr[k1], w_scr[FFT_H + k1]], axis=0).astype(_bf16)
            base = pl.multiple_of(2 * half * (k1 + 1) + part * half, half)
            o_ref[pl.ds(base, half), :] = _dot(fb_ref[k1 - 1, pl.ds(part * half, half), :], yin)
            return carry

        lax.fori_loop(1, FFT_H, stage_b, 0)


def _filter_features(seq, hy_w1, hy_b1, hy_w2, hy_b2, hy_w3, hy_b3, hy_freq):
    t = jnp.linspace(0.0, 1.0, seq, dtype=_f32)[:, None]
    w = 2.0 * math.pi * jnp.arange(seq, dtype=_f32)[:, None] / seq
    f = jnp.linspace(1e-4, HY_BANDS - 1, HY_BANDS, dtype=_f32)[None, :]
    z = jnp.concatenate([t, jnp.cos(f * w), -jnp.sin(f * w)], axis=-1)
    hi = lax.Precision.HIGHEST
    hdn = jnp.sin(hy_freq * (jnp.dot(z, hy_w1, precision=hi) + hy_b1))
    hdn = jnp.sin(hy_freq * (jnp.dot(hdn, hy_w2, precision=hi) + hy_b2))
    return jnp.sin(hy_freq * (jnp.dot(hdn, hy_w3, precision=hi) + hy_b3))


def _filter_spectra(hdn, hy_w4, consts):
    fa_k, _, fb, _, fbs, _ = consts
    seq, width = hdn.shape
    d = hy_w4.shape[1] // (2 * HY_ORDER)
    tl = LANE_TILE
    w4 = hy_w4.reshape(width, 2 * HY_ORDER, d).transpose(1, 0, 2)
    max_decay = math.log(HY_TARGET) / HY_FAST_DECAY
    min_decay = math.log(HY_TARGET) / HY_SLOW_DECAY
    deltas = jnp.abs(jnp.linspace(min_decay, max_decay, d, dtype=_f32)).reshape(1, d)
    n_rows = (FFT_H + 1) * 2 * FFT_N2
    const2 = lambda shape: _single(shape, lambda o, j: (0, 0))
    return pl.pallas_call(
        _filter_kernel,
        grid=(HY_ORDER, d // tl),
        in_specs=[const2(hdn.shape),
                  pl.BlockSpec((None, width, tl), lambda o, j: (2 * o, 0, j)),
                  pl.BlockSpec((None, width, tl), lambda o, j: (2 * o + 1, 0, j)),
                  pl.BlockSpec((1, tl), lambda o, j: (0, j)),
                  const2(fa_k.shape), _single(fb.shape, lambda o, j: (0, 0, 0)), const2(fbs.shape)],
        out_specs=pl.BlockSpec((None, n_rows, tl), lambda o, j: (o, 0, j)),
        out_shape=jax.ShapeDtypeStruct((HY_ORDER, n_rows, d), _f32),
        scratch_shapes=[pltpu.VMEM((FFT_H, FFT_N2, tl), _f32), pltpu.VMEM((FFT_N1, FFT_N2, tl), _f32)],
        compiler_params=_params(),
        name="filter_spectra",
    )(hdn, w4, w4, deltas, fa_k, fb, fbs)


def _merge_kernel(hx_ref, yr_ref, yh_ref, x_ref, wbg_ref, bbg_ref, wa_ref, wb_ref, wo_ref, g1_ref,
                  g_ref, sh_ref, sc_ref, x1_ref, h2_ref):
    d = x_ref.shape[-1]
    gates = _sigmoid(_dot(hx_ref[...], wbg_ref[...]) + bbg_ref[...])
    ya = _dot(yr_ref[...], wa_ref[...])
    yb = _dot(yh_ref[...].astype(_bf16), wb_ref[...])
    merged = gates[:, :d] * ya + gates[:, d:] * yb
    x1 = x_ref[...] + g1_ref[...] * _dot(merged.astype(_bf16), wo_ref[...])
    x1_ref[...] = x1
    y = x1 * lax.rsqrt(jnp.mean(x1 * x1, axis=-1, keepdims=True) + RMS_EPS) * g_ref[...]
    h2_ref[...] = (y * (1.0 + sc_ref[...]) + sh_ref[...]).astype(h2_ref.dtype)


def _merge(hx, y_rnn, y_hy, x, w_bg, b_bg, w_a, w_b, w_o, g1, norm2_g, sh2, sc2, tm=512):
    b, l, d = x.shape
    row = lambda width: pl.BlockSpec((None, tm, width), lambda i, j: (i, j, 0))
    vec = pl.BlockSpec((None, 1, d), lambda i, j: (i, 0, 0))
    full = lambda a: pl.BlockSpec(a.shape, lambda i, j: (0,) * a.ndim)
    b_bg = b_bg.reshape(1, -1)
    norm2_g = norm2_g.reshape(1, d)
    return pl.pallas_call(
        _merge_kernel,
        grid=(b, l // tm),
        in_specs=[row(d), row(d), row(d), row(d), full(w_bg), full(b_bg), full(w_a), full(w_b), full(w_o),
                  vec, full(norm2_g), vec, vec],
        out_specs=[row(d), row(d)],
        out_shape=[jax.ShapeDtypeStruct((b, l, d), _f32), jax.ShapeDtypeStruct((b, l, d), _bf16)],
        compiler_params=_params(),
        name="merge",
    )(hx, y_rnn, y_hy, x, w_bg, b_bg, w_a, w_b, w_o, g1, norm2_g, sh2, sc2)


FFN_CHUNK = 256


def _ffn_kernel(h_ref, x1_ref, wg_ref, wu_ref, wo_ref, g2_ref, fg_ref, o_ref, acc_ref):
    h = h_ref[...]
    n_chunks = wg_ref.shape[0]
    acc_ref[...] = jnp.zeros_like(acc_ref)

    def body(ci, carry):
        g = _dot(h, wg_ref[ci])
        u = _dot(h, wu_ref[ci])
        a = (g * _sigmoid(g) * u).astype(_bf16)
        acc_ref[...] += _dot(a, wo_ref[ci])
        return carry

    lax.fori_loop(0, n_chunks, body, 0)
    x2 = x1_ref[...] + g2_ref[...] * acc_ref[...]
    y = x2 * lax.rsqrt(jnp.mean(x2 * x2, axis=-1, keepdims=True) + RMS_EPS)
    o_ref[...] = y * fg_ref[...]


def _ffn(h2, x1, w_g, w_u, w_o, g2, final_g, tm=512):
    b, l, d = x1.shape
    n_chunks = w_g.shape[1] // FFN_CHUNK
    w_g = w_g.reshape(d, n_chunks, FFN_CHUNK).transpose(1, 0, 2)
    w_u = w_u.reshape(d, n_chunks, FFN_CHUNK).transpose(1, 0, 2)
    w_o = w_o.reshape(n_chunks, FFN_CHUNK, d)
    row = pl.BlockSpec((None, tm, d), lambda i, j: (i, j, 0))
    vec = pl.BlockSpec((None, 1, d), lambda i, j: (i, 0, 0))
    full = lambda a: pl.BlockSpec(a.shape, lambda i, j: (0,) * a.ndim)
    final_g = final_g.reshape(1, d)
    return pl.pallas_call(
        _ffn_kernel,
        grid=(b, l // tm),
        in_specs=[row, row, full(w_g), full(w_u), full(w_o), vec, full(final_g)],
        out_specs=row,
        out_shape=jax.ShapeDtypeStruct((b, l, d), _f32),
        scratch_shapes=[pltpu.VMEM((tm, d), _f32)],
        compiler_params=_params(),
        name="ffn",
    )(h2, x1, w_g, w_u, w_o, g2, final_g)


def kernel(x, c, ctx, c_ctx, w_mod, b_mod, norm1_g, norm2_g, w_in, b_in, rnn_conv_w, rnn_conv_b, rg_wa, rg_ba,
           rg_wx, rg_bx, rg_lambda, hy_conv_w, hy_conv_b, hy_w1, hy_b1, hy_w2, hy_b2, hy_w3, hy_b3, hy_freq,
           hy_w4, hy_skip, w_a_out, w_b_out, w_out, w_ffn_in, w_ffn_out, final_g):
    assert w_mod.shape[0] == 1, "single-layer block"
    b, seq, d = x.shape
    ctx_len = ctx.shape[1]
    assert seq == FFT_H * FFT_N2 and d == D_MODEL
    (w_mod, b_mod, norm1_g, norm2_g, w_in, b_in, rnn_conv_w, rnn_conv_b, rg_wa, rg_ba, rg_wx, rg_bx,
     rg_lambda, hy_conv_w, hy_conv_b, hy_w1, hy_b1, hy_w2, hy_b2, hy_w3, hy_b3, hy_freq, hy_w4, hy_skip,
     w_a_out, w_b_out, w_out, w_ffn_in, w_ffn_out) = [
        a[0] for a in (w_mod, b_mod, norm1_g, norm2_g, w_in, b_in, rnn_conv_w, rnn_conv_b, rg_wa, rg_ba,
                       rg_wx, rg_bx, rg_lambda, hy_conv_w, hy_conv_b, hy_w1, hy_b1, hy_w2, hy_b2, hy_w3,
                       hy_b3, hy_freq, hy_w4, hy_skip, w_a_out, w_b_out, w_out, w_ffn_in, w_ffn_out)]

    pad_rows = 2 * SUBLANES - b - 1
    c_all = jnp.concatenate([c, c_ctx[None, :], jnp.zeros((pad_rows, d), _f32)], axis=0)
    mod = _mod_vectors(c_all, w_mod, b_mod)
    sh1, sc1, g1, sh2, sc2, g2 = [m.reshape(b, 1, d) for m in jnp.split(mod[:b], N_MOD, axis=-1)]
    csh1, csc1 = [jnp.broadcast_to(m.reshape(1, 1, d), (b, 1, d))
                  for m in jnp.split(mod[b], N_MOD, axis=-1)[:2]]

    w_in_bf = w_in.astype(_bf16)
    w_rx, b_rx = w_in_bf[:, :d], b_in[:d]
    w_rg, b_rg = w_in_bf[:, d:2 * d], b_in[d:2 * d]
    w_hy, b_hy = w_in_bf[:, 2 * d:5 * d], b_in[2 * d:5 * d]
    w_bg, b_bg = w_in_bf[:, 5 * d:], b_in[5 * d:]
    gate_wf, gate_bf = _gate_weights(rg_wa[0], rg_wx[0], rg_ba[0], rg_bx[0])
    gate_wb, gate_bb = _gate_weights(rg_wa[1], rg_wx[1], rg_ba[1], rg_bx[1])
    zeros_state = jnp.zeros((b, 1, d), _f32)

    hc = _norm_mod(ctx, norm1_g, csh1, csc1, tm=ctx_len)
    u_c = _proj(hc.reshape(b * ctx_len, d), w_rx, b_rx, mode="conv", conv_w=rnn_conv_w, conv_b=rnn_conv_b,
                pad_left=RNN_CONV_PAD_LEFT, period=ctx_len, tm=ctx_len).reshape(b, ctx_len, d)
    _, cf, cb = _rglru(u_c, jnp.zeros_like(u_c), gate_wf, gate_wb, gate_bf, gate_bb, rg_lambda,
                       zeros_state, zeros_state)

    hx = _norm_mod(x, norm1_g, sh1, sc1, tm=1024)
    hx2d = hx.reshape(b * seq, d)
    u = _proj(hx2d, w_rx, b_rx, mode="conv", conv_w=rnn_conv_w, conv_b=rnn_conv_b,
              pad_left=RNN_CONV_PAD_LEFT).reshape(b, seq, d)
    grg = _proj(hx2d, w_rg, b_rg, mode="gelu").reshape(b, seq, d)
    q = _proj(hx2d, w_hy, b_hy, mode="conv", conv_w=hy_conv_w, conv_b=hy_conv_b,
              pad_left=HY_CONV_PAD_LEFT)
    y_rnn, _, _ = _rglru(u, grg, gate_wf, gate_wb, gate_bf, gate_bb, rg_lambda, cf, cb)

    consts = [jnp.asarray(a).astype(_bf16) for a in _dft_constants()]
    hdn = _filter_features(seq, hy_w1, hy_b1, hy_w2, hy_b2, hy_w3, hy_b3, hy_freq)
    kf = _filter_spectra(hdn, hy_w4, consts)
    q4 = q.reshape(b, FFT_H, FFT_N2, 3 * d)
    n_tiles = d // LANE_TILE
    z1 = _hyena_order(q4, 0, q4, n_tiles, kf, 0, hy_skip[0:1], consts, _f32)
    y_hy = _hyena_order(z1, 0, q4, 2 * n_tiles, kf, 1, hy_skip[1:2], consts, _f32)
    y_hy = y_hy.reshape(b, seq, d)

    x1, h2 = _merge(hx, y_rnn, y_hy, x, w_bg, b_bg, w_a_out.astype(_bf16), w_b_out.astype(_bf16),
                    w_out.astype(_bf16), g1, norm2_g, sh2, sc2)
    w_ffn_bf = w_ffn_in.astype(_bf16)
    return _ffn(h2, x1, w_ffn_bf[:, :D_FF], w_ffn_bf[:, D_FF:], w_ffn_out.astype(_bf16), g2, final_g)
```

```python
import functools
import math

import numpy as np
import jax
import jax.numpy as jnp
from jax import lax
from jax.experimental import pallas as pl
from jax.experimental.pallas import tpu as pltpu

D_MODEL = 1024
GRID_W = 64
N_MOD = 6
RMS_EPS = 1e-6
RNN_HEADS = 16
RNN_HEAD_DIM = D_MODEL // RNN_HEADS
RNN_CONV_PAD_LEFT = 2
HY_CONV_PAD_LEFT = 1
RG_C = 8.0
HY_ORDER = 2
HY_EMB = 33
HY_BANDS = (HY_EMB - 1) // 2
HY_FAST_DECAY = 0.3
HY_SLOW_DECAY = 1.5
HY_TARGET = 1e-2
D_FF = ((8 * D_MODEL // 3 + 255) // 256) * 256

SUBLANES = 8
LANE_TILE = 256
VMEM_LIMIT = 56 * 1024 * 1024

FFT_N1 = 64
FFT_N2 = 128
FFT_N = FFT_N1 * FFT_N2
FFT_H = FFT_N1 // 2

_f32 = jnp.float32
_bf16 = jnp.bfloat16


def _params(**kw):
    return pltpu.CompilerParams(vmem_limit_bytes=VMEM_LIMIT, **kw)


def _dot(a, b):
    return jnp.dot(a, b, preferred_element_type=_f32)


def _sigmoid(x):
    return 0.5 * jnp.tanh(0.5 * x) + 0.5


def _mod_kernel(c_ref, w_ref, b_ref, o_ref):
    c = c_ref[...]
    s = c * _sigmoid(c)
    o_ref[...] = jnp.dot(s, w_ref[...], preferred_element_type=_f32,
                         precision=lax.Precision.HIGHEST) + b_ref[...]


def _mod_vectors(c_all, w_mod, b_mod):
    rows, d = c_all.shape
    n = w_mod.shape[1]
    tn = 1024
    return pl.pallas_call(
        _mod_kernel,
        grid=(n // tn,),
        in_specs=[pl.BlockSpec((rows, d), lambda j: (0, 0)),
                  pl.BlockSpec((d, tn), lambda j: (0, j)),
                  pl.BlockSpec((1, tn), lambda j: (0, j))],
        out_specs=pl.BlockSpec((rows, tn), lambda j: (0, j)),
        out_shape=jax.ShapeDtypeStruct((rows, n), _f32),
        compiler_params=_params(),
        name="mod_vectors",
    )(c_all, w_mod, b_mod.reshape(1, n))


LANES = 128
NORM_TT = 128


def _norm_mod_kernel(x_ref, g_ref, sh_ref, sc_ref, o_ref, otb_ref, tb_scr):
    nb, tt, d = x_ref.shape
    for b in range(nb):
        x = x_ref[b]
        y = x * lax.rsqrt(jnp.mean(x * x, axis=-1, keepdims=True) + RMS_EPS) * g_ref[...]
        y = y * (1.0 + sc_ref[b]) + sh_ref[b]
        o_ref[b] = y.astype(o_ref.dtype)
        for s in range(d // LANES):
            tb_scr.at[s][pl.ds(b, tt, stride=nb), :] = y[:, s * LANES:(s + 1) * LANES]
    for s in range(d // LANES):
        otb_ref[:, s * LANES:(s + 1) * LANES] = tb_scr[s].astype(otb_ref.dtype)


def _norm_mod(x, g, sh, sc):
    b, l, d = x.shape
    tt = NORM_TT
    vec = pl.BlockSpec((b, 1, d), lambda i: (0, 0, 0))
    return pl.pallas_call(
        _norm_mod_kernel,
        grid=(l // tt,),
        in_specs=[pl.BlockSpec((b, tt, d), lambda i: (0, i, 0)),
                  pl.BlockSpec((1, d), lambda i: (0, 0)), vec, vec],
        out_specs=[pl.BlockSpec((b, tt, d), lambda i: (0, i, 0)),
                   pl.BlockSpec((tt * b, d), lambda i: (i, 0))],
        out_shape=[jax.ShapeDtypeStruct((b, l, d), _bf16), jax.ShapeDtypeStruct((l * b, d), _bf16)],
        scratch_shapes=[pltpu.VMEM((d // LANES, tt * b, LANES), _f32)],
        compiler_params=_params(),
        name="norm_mod",
    )(x, g.reshape(1, d), sh, sc)


def _short_conv(y, w_ref, cb_ref, pad_left, period, row_stride):
    rows = y.shape[0]
    pos = (lax.broadcasted_iota(jnp.int32, y.shape, 0) // row_stride) % period
    out = jnp.broadcast_to(cb_ref[...], y.shape)
    for k in range(w_ref.shape[0]):
        off = k - pad_left
        if off == 0:
            tap = y
        else:
            tap = pltpu.roll(y, (-off * row_stride) % rows, axis=0)
            valid = (pos + off >= 0) & (pos + off < period)
            tap = jnp.where(valid, tap, 0.0)
        out = out + tap * w_ref[k:k + 1, :]
    return out


def _proj_kernel(x_ref, w_ref, b_ref, *rest, mode, pad_left, period, row_stride):
    o_ref = rest[-1]
    y = _dot(x_ref[...], w_ref[...]) + b_ref[...]
    if mode == "conv":
        y = _short_conv(y, rest[0], rest[1], pad_left, period, row_stride)
    elif mode == "gelu":
        y = jax.nn.gelu(y)
    o_ref[...] = y.astype(o_ref.dtype)


def _proj(x2d, w, bias, *, mode, conv_w=None, conv_b=None, pad_left=0, period=GRID_W, row_stride=1,
          tm=512, tn=1024, out_dtype=_f32):
    assert tm % (period * row_stride) == 0 or mode != "conv"
    m, k = x2d.shape
    n = w.shape[1]
    in_specs = [pl.BlockSpec((tm, k), lambda j, i: (i, 0)),
                pl.BlockSpec((k, tn), lambda j, i: (0, j)),
                pl.BlockSpec((1, tn), lambda j, i: (0, j))]
    args = [x2d, w, bias.reshape(1, n)]
    if mode == "conv":
        taps = conv_w.shape[0]
        in_specs += [pl.BlockSpec((taps, tn), lambda j, i: (0, j)),
                     pl.BlockSpec((1, tn), lambda j, i: (0, j))]
        args += [conv_w, conv_b.reshape(1, n)]
    return pl.pallas_call(
        functools.partial(_proj_kernel, mode=mode, pad_left=pad_left, period=period, row_stride=row_stride),
        grid=(n // tn, m // tm),
        in_specs=in_specs,
        out_specs=pl.BlockSpec((tm, tn), lambda j, i: (i, j)),
        out_shape=jax.ShapeDtypeStruct((m, n), out_dtype),
        compiler_params=_params(),
        name="proj_" + mode,
    )(*args)


RNN_TT = 256
RNN_UNROLL = 8


def _rnn_coeffs(u, w_ref, bias_ref, lam_ref):
    tl = u.shape[-1]
    decay = -RG_C * jax.nn.softplus(-lam_ref[...])
    g = _dot(u.astype(_bf16), w_ref[...]) + bias_ref[...]
    r = _sigmoid(g[:, :tl])
    i = _sigmoid(g[:, tl:])
    log_a = decay * r
    a = jnp.exp(log_a)
    b = jnp.sqrt(-jnp.tanh(log_a) * (a * a + 1.0)) * (i * u)
    return a, b


def _rnn_scan(a_scr, b_scr, h_ref, h0, reverse):
    nb = h0.shape[0]
    steps = a_scr.shape[0] // nb

    def step(t, h):
        tidx = (steps - 1 - t) if reverse else t
        r0 = pl.multiple_of(tidx * nb, nb)
        h = a_scr[pl.ds(r0, nb), :] * h + b_scr[pl.ds(r0, nb), :]
        h_ref[pl.ds(r0, nb), :] = h
        return h

    return lax.fori_loop(0, steps, step, h0, unroll=RNN_UNROLL)


def _rnn_fwd_kernel(u_ref, w_ref, bias_ref, lam_ref, h0_ref, hf_ref, hlast_ref, a_scr, b_scr, h_scr):
    @pl.when(pl.program_id(1) == 0)
    def _():
        h_scr[...] = h0_ref[...]

    a, b = _rnn_coeffs(u_ref[...], w_ref, bias_ref, lam_ref)
    a_scr[...] = a
    b_scr[...] = b
    h = _rnn_scan(a_scr, b_scr, hf_ref, h_scr[...], False)
    h_scr[...] = h
    hlast_ref[...] = h


def _rnn_bwd_kernel(u_ref, grg_ref, hf_ref, w_ref, bias_ref, lam_ref, h0_ref, y_ref, hfirst_ref,
                    a_scr, b_scr, h_scr, y_scr):
    @pl.when(pl.program_id(1) == 0)
    def _():
        h_scr[...] = h0_ref[...]

    nb, tt, tl = y_ref.shape
    a, b = _rnn_coeffs(u_ref[...], w_ref, bias_ref, lam_ref)
    a_scr[...] = a
    b_scr[...] = b
    h = _rnn_scan(a_scr, b_scr, b_scr, h_scr[...], True)
    h_scr[...] = h
    hfirst_ref[...] = h
    y = (hf_ref[...] + b_scr[...]) * grg_ref[...]
    for s in range(tl // LANES):
        y_scr[s] = y[:, s * LANES:(s + 1) * LANES]
    for bi in range(nb):
        for s in range(tl // LANES):
            piece = y_scr.at[s][pl.ds(bi, tt, stride=nb), :]
            y_ref[bi, :, s * LANES:(s + 1) * LANES] = piece.astype(y_ref.dtype)


def _rglru(u_tb, grg_tb, nb, gate_w, gate_bias, lam, h0f, h0b):
    rows, d = u_tb.shape
    seq = rows // nb
    tl = LANE_TILE
    tt = min(RNN_TT, seq)
    n_chunks = seq // tt
    chunk = tt * nb
    state = pl.BlockSpec((nb, tl), lambda j, c: (0, j))
    scratch = [pltpu.VMEM((chunk, tl), _f32), pltpu.VMEM((chunk, tl), _f32), pltpu.VMEM((nb, tl), _f32)]

    def param_specs(direction):
        return [pl.BlockSpec((None, None, tl, 2 * tl), lambda j, c: (direction, j, 0, 0)),
                pl.BlockSpec((None, None, 1, 2 * tl), lambda j, c: (direction, j, 0, 0)),
                pl.BlockSpec((None, 1, tl), lambda j, c: (direction, 0, j))]

    fwd_rows = pl.BlockSpec((chunk, tl), lambda j, c: (c, j))
    hf_tb, hf_last = pl.pallas_call(
        _rnn_fwd_kernel,
        grid=(d // tl, n_chunks),
        in_specs=[fwd_rows] + param_specs(0) + [state],
        out_specs=[fwd_rows, state],
        out_shape=[jax.ShapeDtypeStruct((rows, d), _f32), jax.ShapeDtypeStruct((nb, d), _f32)],
        scratch_shapes=scratch,
        compiler_params=_params(),
        name="rglru_fwd",
    )(u_tb, gate_w, gate_bias, lam, h0f)

    bwd_rows = pl.BlockSpec((chunk, tl), lambda j, c: (n_chunks - 1 - c, j))
    y, hb_first = pl.pallas_call(
        _rnn_bwd_kernel,
        grid=(d // tl, n_chunks),
        in_specs=[bwd_rows, bwd_rows, bwd_rows] + param_specs(1) + [state],
        out_specs=[pl.BlockSpec((nb, tt, tl), lambda j, c: (0, n_chunks - 1 - c, j)), state],
        out_shape=[jax.ShapeDtypeStruct((nb, seq, d), _bf16), jax.ShapeDtypeStruct((nb, d), _f32)],
        scratch_shapes=scratch + [pltpu.VMEM((tl // LANES, chunk, LANES), _f32)],
        compiler_params=_params(),
        name="rglru_bwd",
    )(u_tb, grg_tb, hf_tb, gate_w, gate_bias, lam, h0b)
    return y, hf_last, hb_first


def _gate_weights(wa, wx, ba, bx):
    heads_per_tile = LANE_TILE // RNN_HEAD_DIM
    n_tiles = RNN_HEADS // heads_per_tile

    def tile_blockdiag(w):
        w = w.reshape(n_tiles, heads_per_tile, RNN_HEAD_DIM, RNN_HEAD_DIM)
        eye = jnp.eye(heads_per_tile, dtype=w.dtype)
        full = jnp.einsum('thij,hg->thigj', w, eye)
        return full.reshape(n_tiles, LANE_TILE, LANE_TILE)

    w = jnp.concatenate([tile_blockdiag(wa), tile_blockdiag(wx)], axis=-1).astype(_bf16)
    bias = jnp.concatenate([ba.reshape(n_tiles, 1, LANE_TILE), bx.reshape(n_tiles, 1, LANE_TILE)], axis=-1)
    return w, bias.astype(_f32)


HY_UNROLL_A = 2
HY_UNROLL_B = 4


def _dft_constants():
    n1 = np.arange(FFT_H)
    k1 = np.arange(FFT_H + 1)
    ang = 2.0 * np.pi * np.outer(k1, n1) / FFT_N1
    fa = np.concatenate([np.cos(ang), -np.sin(ang)[1:FFT_H]], axis=0)
    fa_k = np.kron(fa, np.eye(SUBLANES))
    weight = np.where((k1 == 0) | (k1 == FFT_H), 1.0, 2.0)[:, None] / FFT_N
    fai = np.concatenate([weight * np.cos(ang), (-2.0 / FFT_N) * np.sin(ang)[1:FFT_H]], axis=0).T
    fai_k = np.kron(fai, np.eye(SUBLANES))

    n2 = np.arange(FFT_N2)
    k2 = np.arange(FFT_N2)

    def cs(k1v):
        idx = (np.outer(k2, n2) * FFT_N1 + k1v * n2[None, :]) % FFT_N
        phi = 2.0 * np.pi * idx / FFT_N
        return np.cos(phi), np.sin(phi)

    fb, fbi = [], []
    for k1v in range(1, FFT_H):
        c, s = cs(k1v)
        fb.append(np.block([[c, s], [-s, c]]))
        fbi.append(np.block([[c.T, -s.T], [s.T, c.T]]))
    c0, s0 = cs(0)
    ch, sh = cs(FFT_H)
    z = np.zeros_like(c0)
    fbs = np.block([[c0, z], [-s0, z], [z, ch], [z, -sh]])
    fbsi = np.block([[c0.T, -s0.T, z, z], [z, z, ch.T, -sh.T]])
    as32 = lambda a: np.asarray(a, np.float32)
    return as32(fa_k), as32(fai_k), as32(np.stack(fb)), as32(np.stack(fbi)), as32(fbs), as32(fbsi)


def _hyena_kernel(z_ref, gate_ref, kf_ref, skip_ref, fa_ref, fai_ref, fb_ref, fbi_ref, fbs_ref, fbsi_ref,
                  o_ref, w_scr):
    tl = z_ref.shape[-1]
    n_groups = FFT_N2 // SUBLANES
    half = FFT_N2

    def stage_a(g, carry):
        r0 = pl.multiple_of(g * SUBLANES, SUBLANES)
        zin = z_ref[:, pl.ds(r0, SUBLANES), :].reshape(FFT_H * SUBLANES, tl)
        y = _dot(fa_ref[...], zin.astype(_bf16))
        w_scr[:, pl.ds(r0, SUBLANES), :] = y.reshape(FFT_N1, SUBLANES, tl)
        return carry

    lax.fori_loop(0, n_groups, stage_a, 0, unroll=HY_UNROLL_A)

    yin = jnp.concatenate([w_scr[0], w_scr[FFT_H]], axis=0).astype(_bf16)
    x = _dot(fbs_ref[...], yin)
    pieces = []
    for q in range(2):
        xr = x[(2 * q) * half:(2 * q + 1) * half]
        xi = x[(2 * q + 1) * half:(2 * q + 2) * half]
        kr = kf_ref[pl.ds((2 * q) * half, half), :]
        ki = kf_ref[pl.ds((2 * q + 1) * half, half), :]
        pieces += [xr * kr - xi * ki, xr * ki + xi * kr]
    v = _dot(fbsi_ref[...], jnp.concatenate(pieces, axis=0).astype(_bf16))
    w_scr[0] = v[:half]
    w_scr[FFT_H] = v[half:]

    def stage_b(k1, carry):
        yin = jnp.concatenate([w_scr[k1], w_scr[FFT_H + k1]], axis=0).astype(_bf16)
        x = _dot(fb_ref[k1 - 1], yin)
        xr, xi = x[:half], x[half:]
        base = pl.multiple_of(2 * half * (k1 + 1), 2 * half)
        kr = kf_ref[pl.ds(base, half), :]
        ki = kf_ref[pl.ds(base + half, half), :]
        p = jnp.concatenate([xr * kr - xi * ki, xr * ki + xi * kr], axis=0).astype(_bf16)
        v = _dot(fbi_ref[k1 - 1], p)
        w_scr[k1] = v[:half]
        w_scr[FFT_H + k1] = v[half:]
        return carry

    lax.fori_loop(1, FFT_H, stage_b, 0, unroll=HY_UNROLL_B)

    skip = skip_ref[...]

    def stage_c(g, carry):
        r0 = pl.multiple_of(g * SUBLANES, SUBLANES)
        vin = w_scr[:, pl.ds(r0, SUBLANES), :].reshape(FFT_N1 * SUBLANES, tl)
        conv = _dot(fai_ref[...], vin.astype(_bf16)).reshape(FFT_H, SUBLANES, tl)
        zin = z_ref[:, pl.ds(r0, SUBLANES), :]
        out = gate_ref[:, pl.ds(r0, SUBLANES), :] * (conv + skip * zin)
        o_ref[:, pl.ds(r0, SUBLANES), :] = out.astype(o_ref.dtype)
        return carry

    lax.fori_loop(0, n_groups, stage_c, 0, unroll=HY_UNROLL_A)


def _single(shape, index_map):
    return pl.BlockSpec(shape, index_map, pipeline_mode=pl.Buffered(1))


def _hyena_order(z_arr, z_col, gate_arr, gate_col, kf, order, skip, consts, out_dtype):
    fa_k, fai_k, fb, fbi, fbs, fbsi = consts
    b = z_arr.shape[0]
    tl = LANE_TILE
    d_out = kf.shape[-1]
    seq_block = (None, FFT_H, FFT_N2, tl)
    const2 = lambda shape: _single(shape, lambda j, i: (0, 0))
    const3 = lambda shape: _single(shape, lambda j, i: (0, 0, 0))
    return pl.pallas_call(
        _hyena_kernel,
        grid=(d_out // tl, b),
        in_specs=[pl.BlockSpec(seq_block, lambda j, i: (i, 0, 0, z_col + j)),
                  pl.BlockSpec(seq_block, lambda j, i: (i, 0, 0, gate_col + j)),
                  _single((None, kf.shape[1], tl), lambda j, i: (order, 0, j)),
                  pl.BlockSpec((1, tl), lambda j, i: (0, j)),
                  const2(fa_k.shape), const2(fai_k.shape), const3(fb.shape), const3(fbi.shape),
                  const2(fbs.shape), const2(fbsi.shape)],
        out_specs=pl.BlockSpec(seq_block, lambda j, i: (i, 0, 0, j)),
        out_shape=jax.ShapeDtypeStruct((b, FFT_H, FFT_N2, d_out), out_dtype),
        scratch_shapes=[pltpu.VMEM((FFT_N1, FFT_N2, tl), _f32)],
        compiler_params=_params(),
        name="hyena_conv",
    )(z_arr, gate_arr, kf, skip, fa_k, fai_k, fb, fbi, fbs, fbsi)


def _filter_kernel(hdn_ref, w4f_ref, w4b_ref, delta_ref, fa_ref, fb_ref, fbs_ref, o_ref, sig_scr, w_scr):
    seq = hdn_ref.shape[0]
    tl = o_ref.shape[-1]
    half = FFT_N2
    hi = lax.Precision.HIGHEST
    hdn = hdn_ref[...]
    row = lax.broadcasted_iota(jnp.int32, (seq, tl), 0)
    decay = jnp.exp(-(row.astype(_f32) * (1.0 / (seq - 1))) * delta_ref[...])
    f = jnp.dot(hdn, w4f_ref[...], preferred_element_type=_f32, precision=hi) * decay
    g = jnp.dot(hdn, w4b_ref[...], preferred_element_type=_f32, precision=hi) * decay
    g = jnp.where(row == 0, 0.0, g)
    norm = jnp.sum(jnp.abs(f), axis=0, keepdims=True) + jnp.sum(jnp.abs(g), axis=0, keepdims=True)
    f = f / norm
    g = g / norm

    for part, sig in enumerate((f + g, f - g)):
        sig_scr[...] = sig.reshape(FFT_H, FFT_N2, tl)

        def stage_a(gi, carry):
            r0 = pl.multiple_of(gi * SUBLANES, SUBLANES)
            zin = sig_scr[:, pl.ds(r0, SUBLANES), :].reshape(FFT_H * SUBLANES, tl)
            y = _dot(fa_ref[...], zin.astype(_bf16))
            w_scr[:, pl.ds(r0, SUBLANES), :] = y.reshape(FFT_N1, SUBLANES, tl)
            return carry

        lax.fori_loop(0, FFT_N2 // SUBLANES, stage_a, 0)

        yin = jnp.concatenate([w_scr[0], w_scr[FFT_H]], axis=0).astype(_bf16)
        o_ref[pl.ds(part * half, half), :] = _dot(fbs_ref[pl.ds(part * half, half), :], yin)
        o_ref[pl.ds((2 + part) * half, half), :] = _dot(fbs_ref[pl.ds((2 + part) * half, half), :], yin)

        def stage_b(k1, carry):
            yin = jnp.concatenate([w_scr[k1], w_scr[FFT_H + k1]], axis=0).astype(_bf16)
            base = pl.multiple_of(2 * half * (k1 + 1) + part * half, half)
            o_ref[pl.ds(base, half), :] = _dot(fb_ref[k1 - 1, pl.ds(part * half, half), :], yin)
            return carry

        lax.fori_loop(1, FFT_H, stage_b, 0)


def _filter_features(seq, hy_w1, hy_b1, hy_w2, hy_b2, hy_w3, hy_b3, hy_freq):
    t = jnp.linspace(0.0, 1.0, seq, dtype=_f32)[:, None]
    w = 2.0 * math.pi * jnp.arange(seq, dtype=_f32)[:, None] / seq
    f = jnp.linspace(1e-4, HY_BANDS - 1, HY_BANDS, dtype=_f32)[None, :]
    z = jnp.concatenate([t, jnp.cos(f * w), -jnp.sin(f * w)], axis=-1)
    hi = lax.Precision.HIGHEST
    hdn = jnp.sin(hy_freq * (jnp.dot(z, hy_w1, precision=hi) + hy_b1))
    hdn = jnp.sin(hy_freq * (jnp.dot(hdn, hy_w2, precision=hi) + hy_b2))
    return jnp.sin(hy_freq * (jnp.dot(hdn, hy_w3, precision=hi) + hy_b3))


def _filter_spectra(hdn, hy_w4, consts):
    fa_k, _, fb, _, fbs, _ = consts
    seq, width = hdn.shape
    d = hy_w4.shape[1] // (2 * HY_ORDER)
    tl = LANE_TILE
    w4 = hy_w4.reshape(width, 2 * HY_ORDER, d).transpose(1, 0, 2)
    max_decay = math.log(HY_TARGET) / HY_FAST_DECAY
    min_decay = math.log(HY_TARGET) / HY_SLOW_DECAY
    deltas = jnp.abs(jnp.linspace(min_decay, max_decay, d, dtype=_f32)).reshape(1, d)
    n_rows = (FFT_H + 1) * 2 * FFT_N2
    const2 = lambda shape: _single(shape, lambda o, j: (0, 0))
    return pl.pallas_call(
        _filter_kernel,
        grid=(HY_ORDER, d // tl),
        in_specs=[const2(hdn.shape),
                  pl.BlockSpec((None, width, tl), lambda o, j: (2 * o, 0, j)),
                  pl.BlockSpec((None, width, tl), lambda o, j: (2 * o + 1, 0, j)),
                  pl.BlockSpec((1, tl), lambda o, j: (0, j)),
                  const2(fa_k.shape), _single(fb.shape, lambda o, j: (0, 0, 0)), const2(fbs.shape)],
        out_specs=pl.BlockSpec((None, n_rows, tl), lambda o, j: (o, 0, j)),
        out_shape=jax.ShapeDtypeStruct((HY_ORDER, n_rows, d), _f32),
        scratch_shapes=[pltpu.VMEM((FFT_H, FFT_N2, tl), _f32), pltpu.VMEM((FFT_N1, FFT_N2, tl), _f32)],
        compiler_params=_params(),
        name="filter_spectra",
    )(hdn, w4, w4, deltas, fa_k, fb, fbs)


def _merge_kernel(hx_ref, yr_ref, yh_ref, x_ref, wbg_ref, bbg_ref, wa_ref, wb_ref, wo_ref, g1_ref,
                  g_ref, sh_ref, sc_ref, x1_ref, h2_ref):
    d = x_ref.shape[-1]
    gates = _sigmoid(_dot(hx_ref[...], wbg_ref[...]) + bbg_ref[...])
    ya = _dot(yr_ref[...], wa_ref[...])
    yb = _dot(yh_ref[...].astype(_bf16), wb_ref[...])
    merged = gates[:, :d] * ya + gates[:, d:] * yb
    x1 = x_ref[...] + g1_ref[...] * _dot(merged.astype(_bf16), wo_ref[...])
    x1_ref[...] = x1
    y = x1 * lax.rsqrt(jnp.mean(x1 * x1, axis=-1, keepdims=True) + RMS_EPS) * g_ref[...]
    h2_ref[...] = (y * (1.0 + sc_ref[...]) + sh_ref[...]).astype(h2_ref.dtype)


def _merge(hx, y_rnn, y_hy, x, w_bg, b_bg, w_a, w_b, w_o, g1, norm2_g, sh2, sc2, tm=512):
    b, l, d = x.shape
    row = lambda width: pl.BlockSpec((None, tm, width), lambda i, j: (i, j, 0))
    vec = pl.BlockSpec((None, 1, d), lambda i, j: (i, 0, 0))
    full = lambda a: pl.BlockSpec(a.shape, lambda i, j: (0,) * a.ndim)
    b_bg = b_bg.reshape(1, -1)
    norm2_g = norm2_g.reshape(1, d)
    return pl.pallas_call(
        _merge_kernel,
        grid=(b, l // tm),
        in_specs=[row(d), row(d), row(d), row(d), full(w_bg), full(b_bg), full(w_a), full(w_b), full(w_o),
                  vec, full(norm2_g), vec, vec],
        out_specs=[row(d), row(d)],
        out_shape=[jax.ShapeDtypeStruct((b, l, d), _f32), jax.ShapeDtypeStruct((b, l, d), _bf16)],
        compiler_params=_params(),
        name="merge",
    )(hx, y_rnn, y_hy, x, w_bg, b_bg, w_a, w_b, w_o, g1, norm2_g, sh2, sc2)


FFN_CHUNK = 256


def _ffn_kernel(h_ref, x1_ref, wg_ref, wu_ref, wo_ref, g2_ref, fg_ref, o_ref, acc_ref):
    h = h_ref[...]
    n_chunks = wg_ref.shape[0]
    acc_ref[...] = jnp.zeros_like(acc_ref)

    def body(ci, carry):
        g = _dot(h, wg_ref[ci])
        u = _dot(h, wu_ref[ci])
        a = (g * _sigmoid(g) * u).astype(_bf16)
        acc_ref[...] += _dot(a, wo_ref[ci])
        return carry

    lax.fori_loop(0, n_chunks, body, 0)
    x2 = x1_ref[...] + g2_ref[...] * acc_ref[...]
    y = x2 * lax.rsqrt(jnp.mean(x2 * x2, axis=-1, keepdims=True) + RMS_EPS)
    o_ref[...] = y * fg_ref[...]


def _ffn(h2, x1, w_g, w_u, w_o, g2, final_g, tm=512):
    b, l, d = x1.shape
    n_chunks = w_g.shape[1] // FFN_CHUNK
    w_g = w_g.reshape(d, n_chunks, FFN_CHUNK).transpose(1, 0, 2)
    w_u = w_u.reshape(d, n_chunks, FFN_CHUNK).transpose(1, 0, 2)
    w_o = w_o.reshape(n_chunks, FFN_CHUNK, d)
    row = pl.BlockSpec((None, tm, d), lambda i, j: (i, j, 0))
    vec = pl.BlockSpec((None, 1, d), lambda i, j: (i, 0, 0))
    full = lambda a: pl.BlockSpec(a.shape, lambda i, j: (0,) * a.ndim)
    final_g = final_g.reshape(1, d)
    return pl.pallas_call(
        _ffn_kernel,
        grid=(b, l // tm),
        in_specs=[row, row, full(w_g), full(w_u), full(w_o), vec, full(final_g)],
        out_specs=row,
        out_shape=jax.ShapeDtypeStruct((b, l, d), _f32),
        scratch_shapes=[pltpu.VMEM((tm, d), _f32)],
        compiler_params=_params(),
        name="ffn",
    )(h2, x1, w_g, w_u, w_o, g2, final_g)


def kernel(x, c, ctx, c_ctx, w_mod, b_mod, norm1_g, norm2_g, w_in, b_in, rnn_conv_w, rnn_conv_b, rg_wa, rg_ba,
           rg_wx, rg_bx, rg_lambda, hy_conv_w, hy_conv_b, hy_w1, hy_b1, hy_w2, hy_b2, hy_w3, hy_b3, hy_freq,
           hy_w4, hy_skip, w_a_out, w_b_out, w_out, w_ffn_in, w_ffn_out, final_g):
    assert w_mod.shape[0] == 1, "single-layer block"
    b, seq, d = x.shape
    ctx_len = ctx.shape[1]
    assert seq == FFT_H * FFT_N2 and d == D_MODEL
    assert b == SUBLANES, "time-major rows put the batch on the sublanes of one register"
    (w_mod, b_mod, norm1_g, norm2_g, w_in, b_in, rnn_conv_w, rnn_conv_b, rg_wa, rg_ba, rg_wx, rg_bx,
     rg_lambda, hy_conv_w, hy_conv_b, hy_w1, hy_b1, hy_w2, hy_b2, hy_w3, hy_b3, hy_freq, hy_w4, hy_skip,
     w_a_out, w_b_out, w_out, w_ffn_in, w_ffn_out) = [
        a[0] for a in (w_mod, b_mod, norm1_g, norm2_g, w_in, b_in, rnn_conv_w, rnn_conv_b, rg_wa, rg_ba,
                       rg_wx, rg_bx, rg_lambda, hy_conv_w, hy_conv_b, hy_w1, hy_b1, hy_w2, hy_b2, hy_w3,
                       hy_b3, hy_freq, hy_w4, hy_skip, w_a_out, w_b_out, w_out, w_ffn_in, w_ffn_out)]

    pad_rows = 2 * SUBLANES - b - 1
    c_all = jnp.concatenate([c, c_ctx[None, :], jnp.zeros((pad_rows, d), _f32)], axis=0)
    mod = _mod_vectors(c_all, w_mod, b_mod)
    sh1, sc1, g1, sh2, sc2, g2 = [m.reshape(b, 1, d) for m in jnp.split(mod[:b], N_MOD, axis=-1)]
    csh1, csc1 = [jnp.broadcast_to(m.reshape(1, 1, d), (b, 1, d))
                  for m in jnp.split(mod[b], N_MOD, axis=-1)[:2]]

    w_in_bf = w_in.astype(_bf16)
    w_rx, b_rx = w_in_bf[:, :d], b_in[:d]
    w_rg, b_rg = w_in_bf[:, d:2 * d], b_in[d:2 * d]
    w_hy, b_hy = w_in_bf[:, 2 * d:5 * d], b_in[2 * d:5 * d]
    w_bg, b_bg = w_in_bf[:, 5 * d:], b_in[5 * d:]
    gate_w, gate_bias = zip(*[_gate_weights(rg_wa[i], rg_wx[i], rg_ba[i], rg_bx[i]) for i in range(2)])
    gate_w, gate_bias = jnp.stack(gate_w), jnp.stack(gate_bias)
    lam = rg_lambda.reshape(2, 1, d)
    zeros_state = jnp.zeros((b, d), _f32)

    _, hc_tb = _norm_mod(ctx, norm1_g, csh1, csc1)
    u_c = _proj(hc_tb, w_rx, b_rx, mode="conv", conv_w=rnn_conv_w, conv_b=rnn_conv_b,
                pad_left=RNN_CONV_PAD_LEFT, period=ctx_len, row_stride=b, tm=ctx_len * b)
    _, cf, cb = _rglru(u_c, jnp.zeros_like(u_c), b, gate_w, gate_bias, lam, zeros_state, zeros_state)

    hx, hx_tb = _norm_mod(x, norm1_g, sh1, sc1)
    u = _proj(hx_tb, w_rx, b_rx, mode="conv", conv_w=rnn_conv_w, conv_b=rnn_conv_b,
              pad_left=RNN_CONV_PAD_LEFT, row_stride=b)
    grg = _proj(hx_tb, w_rg, b_rg, mode="gelu")
    q = _proj(hx.reshape(b * seq, d), w_hy, b_hy, mode="conv", conv_w=hy_conv_w, conv_b=hy_conv_b,
              pad_left=HY_CONV_PAD_LEFT)
    y_rnn, _, _ = _rglru(u, grg, b, gate_w, gate_bias, lam, cf, cb)

    consts = [jnp.asarray(a).astype(_bf16) for a in _dft_constants()]
    hdn = _filter_features(seq, hy_w1, hy_b1, hy_w2, hy_b2, hy_w3, hy_b3, hy_freq)
    kf = _filter_spectra(hdn, hy_w4, consts)
    q4 = q.reshape(b, FFT_H, FFT_N2, 3 * d)
    n_tiles = d // LANE_TILE
    z1 = _hyena_order(q4, 0, q4, n_tiles, kf, 0, hy_skip[0:1], consts, _f32)
    y_hy = _hyena_order(z1, 0, q4, 2 * n_tiles, kf, 1, hy_skip[1:2], consts, _f32)
    y_hy = y_hy.reshape(b, seq, d)

    x1, h2 = _merge(hx, y_rnn, y_hy, x, w_bg, b_bg, w_a_out.astype(_bf16), w_b_out.astype(_bf16),
                    w_out.astype(_bf16), g1, norm2_g, sh2, sc2)
    w_ffn_bf = w_ffn_in.astype(_bf16)
    return _ffn(h2, x1, w_ffn_bf[:, :D_FF], w_ffn_bf[:, D_FF:], w_ffn_out.astype(_bf16), g2, final_g)
```

```python
import functools
import math

import numpy as np
import jax
import jax.numpy as jnp
from jax import lax
from jax.experimental import pallas as pl
from jax.experimental.pallas import tpu as pltpu

D_MODEL = 1024
GRID_W = 64
N_MOD = 6
RMS_EPS = 1e-6
RNN_HEADS = 16
RNN_HEAD_DIM = D_MODEL // RNN_HEADS
RNN_CONV_PAD_LEFT = 2
HY_CONV_PAD_LEFT = 1
RG_C = 8.0
HY_ORDER = 2
HY_EMB = 33
HY_BANDS = (HY_EMB - 1) // 2
HY_FAST_DECAY = 0.3
HY_SLOW_DECAY = 1.5
HY_TARGET = 1e-2
D_FF = ((8 * D_MODEL // 3 + 255) // 256) * 256

SUBLANES = 8
LANE_TILE = 256
VMEM_LIMIT = 56 * 1024 * 1024

FFT_N1 = 64
FFT_N2 = 128
FFT_N = FFT_N1 * FFT_N2
FFT_H = FFT_N1 // 2

_f32 = jnp.float32
_bf16 = jnp.bfloat16


def _params(**kw):
    return pltpu.CompilerParams(vmem_limit_bytes=VMEM_LIMIT, **kw)


def _dot(a, b):
    return jnp.dot(a, b, preferred_element_type=_f32)


def _sigmoid(x):
    return 0.5 * jnp.tanh(0.5 * x) + 0.5


def _mod_kernel(c_ref, w_ref, b_ref, o_ref):
    c = c_ref[...]
    s = c * _sigmoid(c)
    o_ref[...] = jnp.dot(s, w_ref[...], preferred_element_type=_f32,
                         precision=lax.Precision.HIGHEST) + b_ref[...]


def _mod_vectors(c_all, w_mod, b_mod):
    rows, d = c_all.shape
    n = w_mod.shape[1]
    tn = 1024
    return pl.pallas_call(
        _mod_kernel,
        grid=(n // tn,),
        in_specs=[pl.BlockSpec((rows, d), lambda j: (0, 0)),
                  pl.BlockSpec((d, tn), lambda j: (0, j)),
                  pl.BlockSpec((1, tn), lambda j: (0, j))],
        out_specs=pl.BlockSpec((rows, tn), lambda j: (0, j)),
        out_shape=jax.ShapeDtypeStruct((rows, n), _f32),
        compiler_params=_params(),
        name="mod_vectors",
    )(c_all, w_mod, b_mod.reshape(1, n))


LANES = 128
NORM_TT = 128


def _norm_mod_kernel(x_ref, g_ref, sh_ref, sc_ref, o_ref, otb_ref, tb_scr):
    nb, tt, d = x_ref.shape
    for b in range(nb):
        x = x_ref[b]
        y = x * lax.rsqrt(jnp.mean(x * x, axis=-1, keepdims=True) + RMS_EPS) * g_ref[...]
        y = y * (1.0 + sc_ref[b]) + sh_ref[b]
        o_ref[b] = y.astype(o_ref.dtype)
        for s in range(d // LANES):
            tb_scr.at[s][pl.ds(b, tt, stride=nb), :] = y[:, s * LANES:(s + 1) * LANES]
    for s in range(d // LANES):
        otb_ref[:, s * LANES:(s + 1) * LANES] = tb_scr[s].astype(otb_ref.dtype)


def _norm_mod(x, g, sh, sc):
    b, l, d = x.shape
    tt = NORM_TT
    vec = pl.BlockSpec((b, 1, d), lambda i: (0, 0, 0))
    return pl.pallas_call(
        _norm_mod_kernel,
        grid=(l // tt,),
        in_specs=[pl.BlockSpec((b, tt, d), lambda i: (0, i, 0)),
                  pl.BlockSpec((1, d), lambda i: (0, 0)), vec, vec],
        out_specs=[pl.BlockSpec((b, tt, d), lambda i: (0, i, 0)),
                   pl.BlockSpec((tt * b, d), lambda i: (i, 0))],
        out_shape=[jax.ShapeDtypeStruct((b, l, d), _bf16), jax.ShapeDtypeStruct((l * b, d), _bf16)],
        scratch_shapes=[pltpu.VMEM((d // LANES, tt * b, LANES), _f32)],
        compiler_params=_params(),
        name="norm_mod",
    )(x, g.reshape(1, d), sh, sc)


def _short_conv(y, w_ref, cb_ref, pad_left, period, row_stride):
    rows = y.shape[0]
    pos = (lax.broadcasted_iota(jnp.int32, y.shape, 0) // row_stride) % period
    out = jnp.broadcast_to(cb_ref[...], y.shape)
    for k in range(w_ref.shape[0]):
        off = k - pad_left
        if off == 0:
            tap = y
        else:
            tap = pltpu.roll(y, (-off * row_stride) % rows, axis=0)
            valid = (pos + off >= 0) & (pos + off < period)
            tap = jnp.where(valid, tap, 0.0)
        out = out + tap * w_ref[k:k + 1, :]
    return out


def _proj_kernel(x_ref, w_ref, b_ref, *rest, mode, pad_left, period, row_stride):
    o_ref = rest[-1]
    y = _dot(x_ref[...], w_ref[...]) + b_ref[...]
    if mode == "conv":
        y = _short_conv(y, rest[0], rest[1], pad_left, period, row_stride)
    elif mode == "gelu":
        y = jax.nn.gelu(y)
    o_ref[...] = y.astype(o_ref.dtype)


def _proj(x2d, w, bias, *, mode, conv_w=None, conv_b=None, pad_left=0, period=GRID_W, row_stride=1,
          tm=1024, tn=1024, out_dtype=_f32):
    assert tm % (period * row_stride) == 0 or mode != "conv"
    m, k = x2d.shape
    n = w.shape[1]
    in_specs = [pl.BlockSpec((tm, k), lambda j, i: (i, 0)),
                pl.BlockSpec((k, tn), lambda j, i: (0, j)),
                pl.BlockSpec((1, tn), lambda j, i: (0, j))]
    args = [x2d, w, bias.reshape(1, n)]
    if mode == "conv":
        taps = conv_w.shape[0]
        in_specs += [pl.BlockSpec((taps, tn), lambda j, i: (0, j)),
                     pl.BlockSpec((1, tn), lambda j, i: (0, j))]
        args += [conv_w, conv_b.reshape(1, n)]
    return pl.pallas_call(
        functools.partial(_proj_kernel, mode=mode, pad_left=pad_left, period=period, row_stride=row_stride),
        grid=(n // tn, m // tm),
        in_specs=in_specs,
        out_specs=pl.BlockSpec((tm, tn), lambda j, i: (i, j)),
        out_shape=jax.ShapeDtypeStruct((m, n), out_dtype),
        compiler_params=_params(),
        name="proj_" + mode,
    )(*args)


RNN_TT = 256
RNN_UNROLL = 8


def _rnn_coeffs(u, w_ref, bias_ref, lam_ref):
    tl = u.shape[-1]
    half_decay = (-0.5 * RG_C) * jax.nn.softplus(-lam_ref[...])
    g = _dot(u.astype(_bf16), w_ref[...]) + bias_ref[...]
    log_a = half_decay * jnp.tanh(g[:, :tl]) + half_decay
    i = 0.5 * jnp.tanh(g[:, tl:]) + 0.5
    a = jnp.exp(log_a)
    x = -jnp.tanh(log_a) * (a * a + 1.0)
    root = jnp.where(x > 0.0, x * lax.rsqrt(x), 0.0)
    return a, root * (i * u)


def _rnn_scan(a_scr, b_scr, h_ref, h0, reverse):
    nb = h0.shape[0]
    steps = a_scr.shape[0] // nb

    def step(t, h):
        tidx = (steps - 1 - t) if reverse else t
        r0 = pl.multiple_of(tidx * nb, nb)
        h = a_scr[pl.ds(r0, nb), :] * h + b_scr[pl.ds(r0, nb), :]
        h_ref[pl.ds(r0, nb), :] = h
        return h

    return lax.fori_loop(0, steps, step, h0, unroll=RNN_UNROLL)


def _rnn_fwd_kernel(u_ref, w_ref, bias_ref, lam_ref, h0_ref, hf_ref, hlast_ref, a_scr, b_scr, h_scr):
    @pl.when(pl.program_id(1) == 0)
    def _():
        h_scr[...] = h0_ref[...]

    a, b = _rnn_coeffs(u_ref[...], w_ref, bias_ref, lam_ref)
    a_scr[...] = a
    b_scr[...] = b
    h = _rnn_scan(a_scr, b_scr, hf_ref, h_scr[...], False)
    h_scr[...] = h
    hlast_ref[...] = h


def _rnn_bwd_kernel(u_ref, grg_ref, hf_ref, w_ref, bias_ref, lam_ref, h0_ref, y_ref, hfirst_ref,
                    a_scr, b_scr, h_scr, y_scr):
    @pl.when(pl.program_id(1) == 0)
    def _():
        h_scr[...] = h0_ref[...]

    nb, tt, tl = y_ref.shape
    a, b = _rnn_coeffs(u_ref[...], w_ref, bias_ref, lam_ref)
    a_scr[...] = a
    b_scr[...] = b
    h = _rnn_scan(a_scr, b_scr, b_scr, h_scr[...], True)
    h_scr[...] = h
    hfirst_ref[...] = h
    y = (hf_ref[...] + b_scr[...]) * grg_ref[...]
    for s in range(tl // LANES):
        y_scr[s] = y[:, s * LANES:(s + 1) * LANES]
    for bi in range(nb):
        for s in range(tl // LANES):
            piece = y_scr.at[s][pl.ds(bi, tt, stride=nb), :]
            y_ref[bi, :, s * LANES:(s + 1) * LANES] = piece.astype(y_ref.dtype)


def _rglru(u_tb, grg_tb, nb, gate_w, gate_bias, lam, h0f, h0b):
    rows, d = u_tb.shape
    seq = rows // nb
    tl = LANE_TILE
    tt = min(RNN_TT, seq)
    n_chunks = seq // tt
    chunk = tt * nb
    state = pl.BlockSpec((nb, tl), lambda j, c: (0, j))
    scratch = [pltpu.VMEM((chunk, tl), _f32), pltpu.VMEM((chunk, tl), _f32), pltpu.VMEM((nb, tl), _f32)]

    def param_specs(direction):
        return [pl.BlockSpec((None, None, tl, 2 * tl), lambda j, c: (direction, j, 0, 0)),
                pl.BlockSpec((None, None, 1, 2 * tl), lambda j, c: (direction, j, 0, 0)),
                pl.BlockSpec((None, 1, tl), lambda j, c: (direction, 0, j))]

    fwd_rows = pl.BlockSpec((chunk, tl), lambda j, c: (c, j))
    hf_tb, hf_last = pl.pallas_call(
        _rnn_fwd_kernel,
        grid=(d // tl, n_chunks),
        in_specs=[fwd_rows] + param_specs(0) + [state],
        out_specs=[fwd_rows, state],
        out_shape=[jax.ShapeDtypeStruct((rows, d), _f32), jax.ShapeDtypeStruct((nb, d), _f32)],
        scratch_shapes=scratch,
        compiler_params=_params(),
        name="rglru_fwd",
    )(u_tb, gate_w, gate_bias, lam, h0f)

    bwd_rows = pl.BlockSpec((chunk, tl), lambda j, c: (n_chunks - 1 - c, j))
    y, hb_first = pl.pallas_call(
        _rnn_bwd_kernel,
        grid=(d // tl, n_chunks),
        in_specs=[bwd_rows, bwd_rows, bwd_rows] + param_specs(1) + [state],
        out_specs=[pl.BlockSpec((nb, tt, tl), lambda j, c: (0, n_chunks - 1 - c, j)), state],
        out_shape=[jax.ShapeDtypeStruct((nb, seq, d), _bf16), jax.ShapeDtypeStruct((nb, d), _f32)],
        scratch_shapes=scratch + [pltpu.VMEM((tl // LANES, chunk, LANES), _f32)],
        compiler_params=_params(),
        name="rglru_bwd",
    )(u_tb, grg_tb, hf_tb, gate_w, gate_bias, lam, h0b)
    return y, hf_last, hb_first


def _gate_weights(wa, wx, ba, bx):
    heads_per_tile = LANE_TILE // RNN_HEAD_DIM
    n_tiles = RNN_HEADS // heads_per_tile

    def tile_blockdiag(w):
        w = w.reshape(n_tiles, heads_per_tile, RNN_HEAD_DIM, RNN_HEAD_DIM)
        eye = jnp.eye(heads_per_tile, dtype=w.dtype)
        full = jnp.einsum('thij,hg->thigj', w, eye)
        return full.reshape(n_tiles, LANE_TILE, LANE_TILE)

    w = (0.5 * jnp.concatenate([tile_blockdiag(wa), tile_blockdiag(wx)], axis=-1)).astype(_bf16)
    bias = jnp.concatenate([ba.reshape(n_tiles, 1, LANE_TILE), bx.reshape(n_tiles, 1, LANE_TILE)], axis=-1)
    return w, (0.5 * bias).astype(_f32)


HY_UNROLL_A = 4
HY_UNROLL_B = 8


def _dft_constants():
    n1 = np.arange(FFT_H)
    k1 = np.arange(FFT_H + 1)
    ang = 2.0 * np.pi * np.outer(k1, n1) / FFT_N1
    fa = np.concatenate([np.cos(ang), -np.sin(ang)[1:FFT_H]], axis=0)
    fa_k = np.kron(fa, np.eye(SUBLANES))
    weight = np.where((k1 == 0) | (k1 == FFT_H), 1.0, 2.0)[:, None] / FFT_N
    fai = np.concatenate([weight * np.cos(ang), (-2.0 / FFT_N) * np.sin(ang)[1:FFT_H]], axis=0).T
    fai_k = np.kron(fai, np.eye(SUBLANES))

    n2 = np.arange(FFT_N2)
    k2 = np.arange(FFT_N2)

    def cs(k1v):
        idx = (np.outer(k2, n2) * FFT_N1 + k1v * n2[None, :]) % FFT_N
        phi = 2.0 * np.pi * idx / FFT_N
        return np.cos(phi), np.sin(phi)

    fb, fbi = [], []
    for k1v in range(1, FFT_H):
        c, s = cs(k1v)
        fb.append(np.block([[c, s], [-s, c]]))
        fbi.append(np.block([[c.T, -s.T], [s.T, c.T]]))
    c0, s0 = cs(0)
    ch, sh = cs(FFT_H)
    z = np.zeros_like(c0)
    fbs = np.block([[c0, z], [-s0, z], [z, ch], [z, -sh]])
    fbsi = np.block([[c0.T, -s0.T, z, z], [z, z, ch.T, -sh.T]])
    as32 = lambda a: np.asarray(a, np.float32)
    return as32(fa_k), as32(fai_k), as32(np.stack(fb)), as32(np.stack(fbi)), as32(fbs), as32(fbsi)


def _hyena_kernel(z_ref, gate_ref, kf_ref, skip_ref, fa_ref, fai_ref, fb_ref, fbi_ref, fbs_ref, fbsi_ref,
                  o_ref, w_scr):
    tl = z_ref.shape[-1]
    n_groups = FFT_N2 // SUBLANES
    half = FFT_N2

    def stage_a(g, carry):
        r0 = pl.multiple_of(g * SUBLANES, SUBLANES)
        zin = z_ref[:, pl.ds(r0, SUBLANES), :].reshape(FFT_H * SUBLANES, tl)
        y = _dot(fa_ref[...], zin.astype(_bf16))
        w_scr[:, pl.ds(r0, SUBLANES), :] = y.reshape(FFT_N1, SUBLANES, tl)
        return carry

    lax.fori_loop(0, n_groups, stage_a, 0, unroll=HY_UNROLL_A)

    yin = jnp.concatenate([w_scr[0], w_scr[FFT_H]], axis=0).astype(_bf16)
    x = _dot(fbs_ref[...], yin)
    pieces = []
    for q in range(2):
        xr = x[(2 * q) * half:(2 * q + 1) * half]
        xi = x[(2 * q + 1) * half:(2 * q + 2) * half]
        kr = kf_ref[pl.ds((2 * q) * half, half), :]
        ki = kf_ref[pl.ds((2 * q + 1) * half, half), :]
        pieces += [xr * kr - xi * ki, xr * ki + xi * kr]
    v = _dot(fbsi_ref[...], jnp.concatenate(pieces, axis=0).astype(_bf16))
    w_scr[0] = v[:half]
    w_scr[FFT_H] = v[half:]

    def stage_b(k1, carry):
        yin = jnp.concatenate([w_scr[k1], w_scr[FFT_H + k1]], axis=0).astype(_bf16)
        x = _dot(fb_ref[k1 - 1], yin)
        xr, xi = x[:half], x[half:]
        base = pl.multiple_of(2 * half * (k1 + 1), 2 * half)
        kr = kf_ref[pl.ds(base, half), :]
        ki = kf_ref[pl.ds(base + half, half), :]
        p = jnp.concatenate([xr * kr - xi * ki, xr * ki + xi * kr], axis=0).astype(_bf16)
        v = _dot(fbi_ref[k1 - 1], p)
        w_scr[k1] = v[:half]
        w_scr[FFT_H + k1] = v[half:]
        return carry

    lax.fori_loop(1, FFT_H, stage_b, 0, unroll=HY_UNROLL_B)

    skip = skip_ref[...]

    def stage_c(g, carry):
        r0 = pl.multiple_of(g * SUBLANES, SUBLANES)
        vin = w_scr[:, pl.ds(r0, SUBLANES), :].reshape(FFT_N1 * SUBLANES, tl)
        conv = _dot(fai_ref[...], vin.astype(_bf16)).reshape(FFT_H, SUBLANES, tl)
        zin = z_ref[:, pl.ds(r0, SUBLANES), :]
        out = gate_ref[:, pl.ds(r0, SUBLANES), :] * (conv + skip * zin)
        o_ref[:, pl.ds(r0, SUBLANES), :] = out.astype(o_ref.dtype)
        return carry

    lax.fori_loop(0, n_groups, stage_c, 0, unroll=HY_UNROLL_A)


def _single(shape, index_map):
    return pl.BlockSpec(shape, index_map, pipeline_mode=pl.Buffered(1))


def _hyena_order(z_arr, z_col, gate_arr, gate_col, kf, order, skip, consts, out_dtype):
    fa_k, fai_k, fb, fbi, fbs, fbsi = consts
    b = z_arr.shape[0]
    tl = LANE_TILE
    d_out = kf.shape[-1]
    seq_block = (None, FFT_H, FFT_N2, tl)
    const2 = lambda shape: _single(shape, lambda j, i: (0, 0))
    const3 = lambda shape: _single(shape, lambda j, i: (0, 0, 0))
    return pl.pallas_call(
        _hyena_kernel,
        grid=(d_out // tl, b),
        in_specs=[pl.BlockSpec(seq_block, lambda j, i: (i, 0, 0, z_col + j)),
                  pl.BlockSpec(seq_block, lambda j, i: (i, 0, 0, gate_col + j)),
                  _single((None, kf.shape[1], tl), lambda j, i: (order, 0, j)),
                  pl.BlockSpec((1, tl), lambda j, i: (0, j)),
                  const2(fa_k.shape), const2(fai_k.shape), const3(fb.shape), const3(fbi.shape),
                  const2(fbs.shape), const2(fbsi.shape)],
        out_specs=pl.BlockSpec(seq_block, lambda j, i: (i, 0, 0, j)),
        out_shape=jax.ShapeDtypeStruct((b, FFT_H, FFT_N2, d_out), out_dtype),
        scratch_shapes=[pltpu.VMEM((FFT_N1, FFT_N2, tl), _f32)],
        compiler_params=_params(),
        name="hyena_conv",
    )(z_arr, gate_arr, kf, skip, fa_k, fai_k, fb, fbi, fbs, fbsi)


def _filter_kernel(hdn_ref, w4f_ref, w4b_ref, delta_ref, fa_ref, fb_ref, fbs_ref, o_ref, sig_scr, w_scr):
    seq = hdn_ref.shape[0]
    tl = o_ref.shape[-1]
    half = FFT_N2
    hdn = hdn_ref[...]
    hdn_hi = hdn.astype(_bf16)
    hdn_lo = (hdn - hdn_hi.astype(_f32)).astype(_bf16)

    def dot3(w):
        w_hi = w.astype(_bf16)
        w_lo = (w - w_hi.astype(_f32)).astype(_bf16)
        return _dot(hdn_hi, w_hi) + (_dot(hdn_hi, w_lo) + _dot(hdn_lo, w_hi))

    row = lax.broadcasted_iota(jnp.int32, (seq, tl), 0)
    decay = jnp.exp(-(row.astype(_f32) * (1.0 / (seq - 1))) * delta_ref[...])
    f = dot3(w4f_ref[...]) * decay
    g = dot3(w4b_ref[...]) * decay
    g = jnp.where(row == 0, 0.0, g)
    norm = jnp.sum(jnp.abs(f), axis=0, keepdims=True) + jnp.sum(jnp.abs(g), axis=0, keepdims=True)
    f = f / norm
    g = g / norm

    for part, sig in enumerate((f + g, f - g)):
        sig_scr[...] = sig.reshape(FFT_H, FFT_N2, tl)

        def stage_a(gi, carry):
            r0 = pl.multiple_of(gi * SUBLANES, SUBLANES)
            zin = sig_scr[:, pl.ds(r0, SUBLANES), :].reshape(FFT_H * SUBLANES, tl)
            y = _dot(fa_ref[...], zin.astype(_bf16))
            w_scr[:, pl.ds(r0, SUBLANES), :] = y.reshape(FFT_N1, SUBLANES, tl)
            return carry

        lax.fori_loop(0, FFT_N2 // SUBLANES, stage_a, 0, unroll=HY_UNROLL_A)

        yin = jnp.concatenate([w_scr[0], w_scr[FFT_H]], axis=0).astype(_bf16)
        o_ref[pl.ds(part * half, half), :] = _dot(fbs_ref[pl.ds(part * half, half), :], yin)
        o_ref[pl.ds((2 + part) * half, half), :] = _dot(fbs_ref[pl.ds((2 + part) * half, half), :], yin)

        def stage_b(k1, carry):
            yin = jnp.concatenate([w_scr[k1], w_scr[FFT_H + k1]], axis=0).astype(_bf16)
            base = pl.multiple_of(2 * half * (k1 + 1) + part * half, half)
            o_ref[pl.ds(base, half), :] = _dot(fb_ref[k1 - 1, pl.ds(part * half, half), :], yin)
            return carry

        lax.fori_loop(1, FFT_H, stage_b, 0, unroll=HY_UNROLL_B)


def _filter_features(seq, hy_w1, hy_b1, hy_w2, hy_b2, hy_w3, hy_b3, hy_freq):
    t = jnp.linspace(0.0, 1.0, seq, dtype=_f32)[:, None]
    w = 2.0 * math.pi * jnp.arange(seq, dtype=_f32)[:, None] / seq
    f = jnp.linspace(1e-4, HY_BANDS - 1, HY_BANDS, dtype=_f32)[None, :]
    z = jnp.concatenate([t, jnp.cos(f * w), -jnp.sin(f * w)], axis=-1)
    hi = lax.Precision.HIGHEST
    hdn = jnp.sin(hy_freq * (jnp.dot(z, hy_w1, precision=hi) + hy_b1))
    hdn = jnp.sin(hy_freq * (jnp.dot(hdn, hy_w2, precision=hi) + hy_b2))
    return jnp.sin(hy_freq * (jnp.dot(hdn, hy_w3, precision=hi) + hy_b3))


def _filter_spectra(hdn, hy_w4, consts):
    fa_k, _, fb, _, fbs, _ = consts
    seq, width = hdn.shape
    d = hy_w4.shape[1] // (2 * HY_ORDER)
    tl = LANE_TILE
    w4 = hy_w4.reshape(width, 2 * HY_ORDER, d).transpose(1, 0, 2)
    max_decay = math.log(HY_TARGET) / HY_FAST_DECAY
    min_decay = math.log(HY_TARGET) / HY_SLOW_DECAY
    deltas = jnp.abs(jnp.linspace(min_decay, max_decay, d, dtype=_f32)).reshape(1, d)
    n_rows = (FFT_H + 1) * 2 * FFT_N2
    const2 = lambda shape: _single(shape, lambda o, j: (0, 0))
    return pl.pallas_call(
        _filter_kernel,
        grid=(HY_ORDER, d // tl),
        in_specs=[const2(hdn.shape),
                  pl.BlockSpec((None, width, tl), lambda o, j: (2 * o, 0, j)),
                  pl.BlockSpec((None, width, tl), lambda o, j: (2 * o + 1, 0, j)),
                  pl.BlockSpec((1, tl), lambda o, j: (0, j)),
                  const2(fa_k.shape), _single(fb.shape, lambda o, j: (0, 0, 0)), const2(fbs.shape)],
        out_specs=pl.BlockSpec((None, n_rows, tl), lambda o, j: (o, 0, j)),
        out_shape=jax.ShapeDtypeStruct((HY_ORDER, n_rows, d), _f32),
        scratch_shapes=[pltpu.VMEM((FFT_H, FFT_N2, tl), _f32), pltpu.VMEM((FFT_N1, FFT_N2, tl), _f32)],
        compiler_params=_params(),
        name="filter_spectra",
    )(hdn, w4, w4, deltas, fa_k, fb, fbs)


def _merge_kernel(hx_ref, yr_ref, yh_ref, x_ref, wbg_ref, bbg_ref, wa_ref, wb_ref, wo_ref, g1_ref,
                  g_ref, sh_ref, sc_ref, x1_ref, h2_ref):
    d = x_ref.shape[-1]
    gates = _sigmoid(_dot(hx_ref[...], wbg_ref[...]) + bbg_ref[...])
    ya = _dot(yr_ref[...], wa_ref[...])
    yb = _dot(yh_ref[...].astype(_bf16), wb_ref[...])
    merged = gates[:, :d] * ya + gates[:, d:] * yb
    x1 = x_ref[...] + g1_ref[...] * _dot(merged.astype(_bf16), wo_ref[...])
    x1_ref[...] = x1
    y = x1 * lax.rsqrt(jnp.mean(x1 * x1, axis=-1, keepdims=True) + RMS_EPS) * g_ref[...]
    h2_ref[...] = (y * (1.0 + sc_ref[...]) + sh_ref[...]).astype(h2_ref.dtype)


def _merge(hx, y_rnn, y_hy, x, w_bg, b_bg, w_a, w_b, w_o, g1, norm2_g, sh2, sc2, tm=512):
    b, l, d = x.shape
    row = lambda width: pl.BlockSpec((None, tm, width), lambda i, j: (i, j, 0))
    vec = pl.BlockSpec((None, 1, d), lambda i, j: (i, 0, 0))
    full = lambda a: pl.BlockSpec(a.shape, lambda i, j: (0,) * a.ndim)
    b_bg = b_bg.reshape(1, -1)
    norm2_g = norm2_g.reshape(1, d)
    return pl.pallas_call(
        _merge_kernel,
        grid=(b, l // tm),
        in_specs=[row(d), row(d), row(d), row(d), full(w_bg), full(b_bg), full(w_a), full(w_b), full(w_o),
                  vec, full(norm2_g), vec, vec],
        out_specs=[row(d), row(d)],
        out_shape=[jax.ShapeDtypeStruct((b, l, d), _f32), jax.ShapeDtypeStruct((b, l, d), _bf16)],
        compiler_params=_params(),
        name="merge",
    )(hx, y_rnn, y_hy, x, w_bg, b_bg, w_a, w_b, w_o, g1, norm2_g, sh2, sc2)


def _ffn_kernel(h_ref, x1_ref, wg_ref, wu_ref, wo_ref, g2_ref, fg_ref, o_ref):
    h = h_ref[...]
    g = _dot(h, wg_ref[...])
    u = _dot(h, wu_ref[...])
    a = (g * _sigmoid(g) * u).astype(_bf16)
    x2 = x1_ref[...] + g2_ref[...] * _dot(a, wo_ref[...])
    y = x2 * lax.rsqrt(jnp.mean(x2 * x2, axis=-1, keepdims=True) + RMS_EPS)
    o_ref[...] = y * fg_ref[...]


def _ffn(h2, x1, w_g, w_u, w_o, g2, final_g, tm=512):
    b, l, d = x1.shape
    row = pl.BlockSpec((None, tm, d), lambda i, j: (i, j, 0))
    vec = pl.BlockSpec((None, 1, d), lambda i, j: (i, 0, 0))
    full = lambda a: _single(a.shape, lambda i, j: (0,) * a.ndim)
    final_g = final_g.reshape(1, d)
    return pl.pallas_call(
        _ffn_kernel,
        grid=(b, l // tm),
        in_specs=[row, row, full(w_g), full(w_u), full(w_o), vec, full(final_g)],
        out_specs=row,
        out_shape=jax.ShapeDtypeStruct((b, l, d), _f32),
        compiler_params=_params(),
        name="ffn",
    )(h2, x1, w_g, w_u, w_o, g2, final_g)


def kernel(x, c, ctx, c_ctx, w_mod, b_mod, norm1_g, norm2_g, w_in, b_in, rnn_conv_w, rnn_conv_b, rg_wa, rg_ba,
           rg_wx, rg_bx, rg_lambda, hy_conv_w, hy_conv_b, hy_w1, hy_b1, hy_w2, hy_b2, hy_w3, hy_b3, hy_freq,
           hy_w4, hy_skip, w_a_out, w_b_out, w_out, w_ffn_in, w_ffn_out, final_g):
    assert w_mod.shape[0] == 1, "single-layer block"
    b, seq, d = x.shape
    ctx_len = ctx.shape[1]
    assert seq == FFT_H * FFT_N2 and d == D_MODEL
    assert b == SUBLANES, "time-major rows put the batch on the sublanes of one register"
    (w_mod, b_mod, norm1_g, norm2_g, w_in, b_in, rnn_conv_w, rnn_conv_b, rg_wa, rg_ba, rg_wx, rg_bx,
     rg_lambda, hy_conv_w, hy_conv_b, hy_w1, hy_b1, hy_w2, hy_b2, hy_w3, hy_b3, hy_freq, hy_w4, hy_skip,
     w_a_out, w_b_out, w_out, w_ffn_in, w_ffn_out) = [
        a[0] for a in (w_mod, b_mod, norm1_g, norm2_g, w_in, b_in, rnn_conv_w, rnn_conv_b, rg_wa, rg_ba,
                       rg_wx, rg_bx, rg_lambda, hy_conv_w, hy_conv_b, hy_w1, hy_b1, hy_w2, hy_b2, hy_w3,
                       hy_b3, hy_freq, hy_w4, hy_skip, w_a_out, w_b_out, w_out, w_ffn_in, w_ffn_out)]

    pad_rows = 2 * SUBLANES - b - 1
    c_all = jnp.concatenate([c, c_ctx[None, :], jnp.zeros((pad_rows, d), _f32)], axis=0)
    mod = _mod_vectors(c_all, w_mod, b_mod)
    sh1, sc1, g1, sh2, sc2, g2 = [m.reshape(b, 1, d) for m in jnp.split(mod[:b], N_MOD, axis=-1)]
    csh1, csc1 = [jnp.broadcast_to(m.reshape(1, 1, d), (b, 1, d))
                  for m in jnp.split(mod[b], N_MOD, axis=-1)[:2]]

    w_in_bf = w_in.astype(_bf16)
    w_rx, b_rx = w_in_bf[:, :d], b_in[:d]
    w_rg, b_rg = w_in_bf[:, d:2 * d], b_in[d:2 * d]
    w_hy, b_hy = w_in_bf[:, 2 * d:5 * d], b_in[2 * d:5 * d]
    w_bg, b_bg = w_in_bf[:, 5 * d:], b_in[5 * d:]
    gate_w, gate_bias = zip(*[_gate_weights(rg_wa[i], rg_wx[i], rg_ba[i], rg_bx[i]) for i in range(2)])
    gate_w, gate_bias = jnp.stack(gate_w), jnp.stack(gate_bias)
    lam = rg_lambda.reshape(2, 1, d)
    zeros_state = jnp.zeros((b, d), _f32)

    _, hc_tb = _norm_mod(ctx, norm1_g, csh1, csc1)
    u_c = _proj(hc_tb, w_rx, b_rx, mode="conv", conv_w=rnn_conv_w, conv_b=rnn_conv_b,
                pad_left=RNN_CONV_PAD_LEFT, period=ctx_len, row_stride=b, tm=ctx_len * b)
    _, cf, cb = _rglru(u_c, jnp.zeros_like(u_c), b, gate_w, gate_bias, lam, zeros_state, zeros_state)

    hx, hx_tb = _norm_mod(x, norm1_g, sh1, sc1)
    u = _proj(hx_tb, w_rx, b_rx, mode="conv", conv_w=rnn_conv_w, conv_b=rnn_conv_b,
              pad_left=RNN_CONV_PAD_LEFT, row_stride=b)
    grg = _proj(hx_tb, w_rg, b_rg, mode="gelu")
    q = _proj(hx.reshape(b * seq, d), w_hy, b_hy, mode="conv", conv_w=hy_conv_w, conv_b=hy_conv_b,
              pad_left=HY_CONV_PAD_LEFT)
    y_rnn, _, _ = _rglru(u, grg, b, gate_w, gate_bias, lam, cf, cb)

    consts = [jnp.asarray(a).astype(_bf16) for a in _dft_constants()]
    hdn = _filter_features(seq, hy_w1, hy_b1, hy_w2, hy_b2, hy_w3, hy_b3, hy_freq)
    kf = _filter_spectra(hdn, hy_w4, consts)
    q4 = q.reshape(b, FFT_H, FFT_N2, 3 * d)
    n_tiles = d // LANE_TILE
    z1 = _hyena_order(q4, 0, q4, n_tiles, kf, 0, hy_skip[0:1], consts, _f32)
    y_hy = _hyena_order(z1, 0, q4, 2 * n_tiles, kf, 1, hy_skip[1:2], consts, _f32)
    y_hy = y_hy.reshape(b, seq, d)

    x1, h2 = _merge(hx, y_rnn, y_hy, x, w_bg, b_bg, w_a_out.astype(_bf16), w_b_out.astype(_bf16),
                    w_out.astype(_bf16), g1, norm2_g, sh2, sc2)
    w_ffn_bf = w_ffn_in.astype(_bf16)
    return _ffn(h2, x1, w_ffn_bf[:, :D_FF], w_ffn_bf[:, D_FF:], w_ffn_out.astype(_bf16), g2, final_g)
```

```python
import functools
import math

import numpy as np
import jax
import jax.numpy as jnp
from jax import lax
from jax.experimental import pallas as pl
from jax.experimental.pallas import tpu as pltpu

D_MODEL = 1024
GRID_W = 64
N_MOD = 6
RMS_EPS = 1e-6
RNN_HEADS = 16
RNN_HEAD_DIM = D_MODEL // RNN_HEADS
RNN_CONV_PAD_LEFT = 2
HY_CONV_PAD_LEFT = 1
RG_C = 8.0
HY_ORDER = 2
HY_EMB = 33
HY_BANDS = (HY_EMB - 1) // 2
HY_FAST_DECAY = 0.3
HY_SLOW_DECAY = 1.5
HY_TARGET = 1e-2
D_FF = ((8 * D_MODEL // 3 + 255) // 256) * 256

SUBLANES = 8
LANE_TILE = 256
VMEM_LIMIT = 56 * 1024 * 1024

FFT_N1 = 64
FFT_N2 = 128
FFT_N = FFT_N1 * FFT_N2
FFT_H = FFT_N1 // 2

_f32 = jnp.float32
_bf16 = jnp.bfloat16


def _params(**kw):
    return pltpu.CompilerParams(vmem_limit_bytes=VMEM_LIMIT, **kw)


def _dot(a, b):
    return jnp.dot(a, b, preferred_element_type=_f32)


def _sigmoid(x):
    return 0.5 * jnp.tanh(0.5 * x) + 0.5


def _mod_kernel(c_ref, w_ref, b_ref, o_ref):
    c = c_ref[...]
    s = c * _sigmoid(c)
    o_ref[...] = jnp.dot(s, w_ref[...], preferred_element_type=_f32,
                         precision=lax.Precision.HIGHEST) + b_ref[...]


def _mod_vectors(c_all, w_mod, b_mod):
    rows, d = c_all.shape
    n = w_mod.shape[1]
    tn = 1024
    return pl.pallas_call(
        _mod_kernel,
        grid=(n // tn,),
        in_specs=[pl.BlockSpec((rows, d), lambda j: (0, 0)),
                  pl.BlockSpec((d, tn), lambda j: (0, j)),
                  pl.BlockSpec((1, tn), lambda j: (0, j))],
        out_specs=pl.BlockSpec((rows, tn), lambda j: (0, j)),
        out_shape=jax.ShapeDtypeStruct((rows, n), _f32),
        compiler_params=_params(),
        name="mod_vectors",
    )(c_all, w_mod, b_mod.reshape(1, n))


LANES = 128
NORM_TT = 128


def _norm_mod_kernel(x_ref, g_ref, sh_ref, sc_ref, o_ref, otb_ref, tb_scr):
    nb, tt, d = x_ref.shape
    for b in range(nb):
        x = x_ref[b]
        gain = g_ref[...] * (1.0 + sc_ref[b])
        y = x * lax.rsqrt(jnp.mean(x * x, axis=-1, keepdims=True) + RMS_EPS) * gain + sh_ref[b]
        o_ref[b] = y.astype(o_ref.dtype)
        for s in range(d // LANES):
            tb_scr.at[s][pl.ds(b, tt, stride=nb), :] = y[:, s * LANES:(s + 1) * LANES]
    for s in range(d // LANES):
        otb_ref[:, s * LANES:(s + 1) * LANES] = tb_scr[s].astype(otb_ref.dtype)


def _norm_mod(x, g, sh, sc):
    b, l, d = x.shape
    tt = NORM_TT
    vec = pl.BlockSpec((b, 1, d), lambda i: (0, 0, 0))
    return pl.pallas_call(
        _norm_mod_kernel,
        grid=(l // tt,),
        in_specs=[pl.BlockSpec((b, tt, d), lambda i: (0, i, 0)),
                  pl.BlockSpec((1, d), lambda i: (0, 0)), vec, vec],
        out_specs=[pl.BlockSpec((b, tt, d), lambda i: (0, i, 0)),
                   pl.BlockSpec((tt * b, d), lambda i: (i, 0))],
        out_shape=[jax.ShapeDtypeStruct((b, l, d), _bf16), jax.ShapeDtypeStruct((l * b, d), _bf16)],
        scratch_shapes=[pltpu.VMEM((d // LANES, tt * b, LANES), _f32)],
        compiler_params=_params(),
        name="norm_mod",
    )(x, g.reshape(1, d), sh, sc)


def _conv_tap_weights(w, pad_left, rows, row_stride):
    if row_stride != 1:
        return [w[k:k + 1, :] for k in range(w.shape[0])]
    ridx = lax.broadcasted_iota(jnp.int32, (rows, w.shape[1]), 0)
    taps = []
    for k in range(w.shape[0]):
        off = k - pad_left
        valid = (ridx + off >= 0) & (ridx + off < rows)
        taps.append(jnp.where(valid, w[k:k + 1, :], 0.0))
    return taps


def _short_conv_period(y, taps, cb, pad_left, row_stride):
    rows = y.shape[0]
    n_taps = len(taps)
    if row_stride != 1:
        lanes = y.shape[1]
        ypad = jnp.concatenate([jnp.zeros((pad_left * row_stride, lanes), y.dtype), y,
                                jnp.zeros(((n_taps - 1 - pad_left) * row_stride, lanes), y.dtype)], axis=0)
        out = cb
        for k in range(n_taps):
            out = out + ypad[k * row_stride:k * row_stride + rows] * taps[k]
        return out
    out = y * taps[pad_left] + cb
    for k in range(n_taps):
        off = k - pad_left
        if off != 0:
            out = out + pltpu.roll(y, (-off) % rows, axis=0) * taps[k]
    return out


def _proj_kernel(x_ref, w_ref, b_ref, *rest, mode, pad_left, period, row_stride):
    o_ref = rest[-1]
    tm, tn = o_ref.shape
    x = x_ref[...]
    for j in range(tn // LANE_TILE):
        cols = slice(j * LANE_TILE, (j + 1) * LANE_TILE)
        y = _dot(x, w_ref[:, cols]) + b_ref[:, cols]
        if mode == "conv":
            prow = period * row_stride
            taps = _conv_tap_weights(rest[0][:, cols], pad_left, prow, row_stride)
            cb = rest[1][:, cols]
            for p in range(tm // prow):
                rows = slice(p * prow, (p + 1) * prow)
                o_ref[rows, cols] = _short_conv_period(y[rows], taps, cb, pad_left, row_stride).astype(o_ref.dtype)
        elif mode == "gelu":
            o_ref[:, cols] = jax.nn.gelu(y).astype(o_ref.dtype)
        else:
            o_ref[:, cols] = y.astype(o_ref.dtype)


def _proj(x2d, w, bias, *, mode, conv_w=None, conv_b=None, pad_left=0, period=GRID_W, row_stride=1,
          tm=1024, tn=1024, out_dtype=_f32):
    assert tm % (period * row_stride) == 0 or mode != "conv"
    m, k = x2d.shape
    n = w.shape[1]
    in_specs = [pl.BlockSpec((tm, k), lambda j, i: (i, 0)),
                pl.BlockSpec((k, tn), lambda j, i: (0, j)),
                pl.BlockSpec((1, tn), lambda j, i: (0, j))]
    args = [x2d, w, bias.reshape(1, n)]
    if mode == "conv":
        taps = conv_w.shape[0]
        in_specs += [pl.BlockSpec((taps, tn), lambda j, i: (0, j)),
                     pl.BlockSpec((1, tn), lambda j, i: (0, j))]
        args += [conv_w, conv_b.reshape(1, n)]
    return pl.pallas_call(
        functools.partial(_proj_kernel, mode=mode, pad_left=pad_left, period=period, row_stride=row_stride),
        grid=(n // tn, m // tm),
        in_specs=in_specs,
        out_specs=pl.BlockSpec((tm, tn), lambda j, i: (i, j)),
        out_shape=jax.ShapeDtypeStruct((m, n), out_dtype),
        compiler_params=_params(),
        name="proj_" + mode,
    )(*args)


RNN_TT = 256
RNN_SUBCHUNKS = 4


def _rnn_coeffs(u, w_ref, bias_ref, lam_ref):
    tl = u.shape[-1]
    half_decay = (-0.5 * RG_C) * jax.nn.softplus(-lam_ref[...])
    g = _dot(u.astype(_bf16), w_ref[...]) + bias_ref[...]
    log_a = half_decay * jnp.tanh(g[:, :tl]) + half_decay
    i = 0.5 * jnp.tanh(g[:, tl:]) + 0.5
    a = jnp.exp(log_a)
    x = -jnp.tanh(log_a) * (a * a + 1.0)
    root = jnp.where(x > 0.0, x * lax.rsqrt(x), 0.0)
    return a, root * (i * u)


def _rnn_scan(u_ref, w_ref, bias_ref, lam_ref, a_scr, b_scr, h_ref, h0, reverse):
    nb = h0.shape[0]
    rows = u_ref.shape[0]
    sub = rows // RNN_SUBCHUNKS
    order = list(range(RNN_SUBCHUNKS))[::-1] if reverse else list(range(RNN_SUBCHUNKS))

    def coeffs(k):
        sl = slice(k * sub, (k + 1) * sub)
        a, b = _rnn_coeffs(u_ref[sl, :], w_ref, bias_ref, lam_ref)
        a_scr[sl, :] = a
        b_scr[sl, :] = b

    coeffs(order[0])
    h = h0
    for pos, k in enumerate(order):
        if pos + 1 < RNN_SUBCHUNKS:
            coeffs(order[pos + 1])
        steps = list(range(k * sub, (k + 1) * sub, nb))
        for r0 in (steps[::-1] if reverse else steps):
            h = a_scr[r0:r0 + nb, :] * h + b_scr[r0:r0 + nb, :]
            h_ref[r0:r0 + nb, :] = h
    return h


def _rnn_fwd_kernel(u_ref, w_ref, bias_ref, lam_ref, h0_ref, hf_ref, hlast_ref, a_scr, b_scr, h_scr):
    @pl.when(pl.program_id(1) == 0)
    def _():
        h_scr[...] = h0_ref[...]

    h = _rnn_scan(u_ref, w_ref, bias_ref, lam_ref, a_scr, b_scr, hf_ref, h_scr[...], False)
    h_scr[...] = h
    hlast_ref[...] = h


def _rnn_bwd_kernel(u_ref, grg_ref, hf_ref, w_ref, bias_ref, lam_ref, h0_ref, y_ref, hfirst_ref,
                    a_scr, b_scr, h_scr, y_scr):
    @pl.when(pl.program_id(1) == 0)
    def _():
        h_scr[...] = h0_ref[...]

    nb, tt, tl = y_ref.shape
    h = _rnn_scan(u_ref, w_ref, bias_ref, lam_ref, a_scr, b_scr, b_scr, h_scr[...], True)
    h_scr[...] = h
    hfirst_ref[...] = h
    y = (hf_ref[...] + b_scr[...]) * grg_ref[...]
    for s in range(tl // LANES):
        y_scr[s] = y[:, s * LANES:(s + 1) * LANES]
    for bi in range(nb):
        for s in range(tl // LANES):
            piece = y_scr.at[s][pl.ds(bi, tt, stride=nb), :]
            y_ref[bi, :, s * LANES:(s + 1) * LANES] = piece.astype(y_ref.dtype)


def _rglru(u_tb, grg_tb, nb, gate_w, gate_bias, lam, h0f, h0b):
    rows, d = u_tb.shape
    seq = rows // nb
    tl = LANE_TILE
    tt = min(RNN_TT, seq)
    n_chunks = seq // tt
    chunk = tt * nb
    state = pl.BlockSpec((nb, tl), lambda j, c: (0, j))
    scratch = [pltpu.VMEM((chunk, tl), _f32), pltpu.VMEM((chunk, tl), _f32), pltpu.VMEM((nb, tl), _f32)]

    def param_specs(direction):
        return [pl.BlockSpec((None, None, tl, 2 * tl), lambda j, c: (direction, j, 0, 0)),
                pl.BlockSpec((None, None, 1, 2 * tl), lambda j, c: (direction, j, 0, 0)),
                pl.BlockSpec((None, 1, tl), lambda j, c: (direction, 0, j))]

    fwd_rows = pl.BlockSpec((chunk, tl), lambda j, c: (c, j))
    hf_tb, hf_last = pl.pallas_call(
        _rnn_fwd_kernel,
        grid=(d // tl, n_chunks),
        in_specs=[fwd_rows] + param_specs(0) + [state],
        out_specs=[fwd_rows, state],
        out_shape=[jax.ShapeDtypeStruct((rows, d), _f32), jax.ShapeDtypeStruct((nb, d), _f32)],
        scratch_shapes=scratch,
        compiler_params=_params(),
        name="rglru_fwd",
    )(u_tb, gate_w, gate_bias, lam, h0f)

    bwd_rows = pl.BlockSpec((chunk, tl), lambda j, c: (n_chunks - 1 - c, j))
    y, hb_first = pl.pallas_call(
        _rnn_bwd_kernel,
        grid=(d // tl, n_chunks),
        in_specs=[bwd_rows, bwd_rows, bwd_rows] + param_specs(1) + [state],
        out_specs=[pl.BlockSpec((nb, tt, tl), lambda j, c: (0, n_chunks - 1 - c, j)), state],
        out_shape=[jax.ShapeDtypeStruct((nb, seq, d), _bf16), jax.ShapeDtypeStruct((nb, d), _f32)],
        scratch_shapes=scratch + [pltpu.VMEM((tl // LANES, chunk, LANES), _f32)],
        compiler_params=_params(),
        name="rglru_bwd",
    )(u_tb, grg_tb, hf_tb, gate_w, gate_bias, lam, h0b)
    return y, hf_last, hb_first


def _gate_weights(wa, wx, ba, bx):
    heads_per_tile = LANE_TILE // RNN_HEAD_DIM
    n_tiles = RNN_HEADS // heads_per_tile

    def tile_blockdiag(w):
        w = w.reshape(n_tiles, heads_per_tile, RNN_HEAD_DIM, RNN_HEAD_DIM)
        eye = jnp.eye(heads_per_tile, dtype=w.dtype)
        full = jnp.einsum('thij,hg->thigj', w, eye)
        return full.reshape(n_tiles, LANE_TILE, LANE_TILE)

    w = (0.5 * jnp.concatenate([tile_blockdiag(wa), tile_blockdiag(wx)], axis=-1)).astype(_bf16)
    bias = jnp.concatenate([ba.reshape(n_tiles, 1, LANE_TILE), bx.reshape(n_tiles, 1, LANE_TILE)], axis=-1)
    return w, (0.5 * bias).astype(_f32)


HY_UNROLL_A = 16
HY_UNROLL_B = 31


def _dft_constants():
    n1 = np.arange(FFT_H)
    k1 = np.arange(FFT_H + 1)
    ang = 2.0 * np.pi * np.outer(k1, n1) / FFT_N1
    fa = np.concatenate([np.cos(ang), -np.sin(ang)[1:FFT_H]], axis=0)
    fa_k = np.kron(fa, np.eye(SUBLANES))
    weight = np.where((k1 == 0) | (k1 == FFT_H), 1.0, 2.0)[:, None] / FFT_N
    fai = np.concatenate([weight * np.cos(ang), (-2.0 / FFT_N) * np.sin(ang)[1:FFT_H]], axis=0).T
    fai_k = np.kron(fai, np.eye(SUBLANES))

    n2 = np.arange(FFT_N2)
    k2 = np.arange(FFT_N2)

    def cs(k1v):
        idx = (np.outer(k2, n2) * FFT_N1 + k1v * n2[None, :]) % FFT_N
        phi = 2.0 * np.pi * idx / FFT_N
        return np.cos(phi), np.sin(phi)

    fb, fbi = [], []
    for k1v in range(1, FFT_H):
        c, s = cs(k1v)
        fb.append(np.block([[c, s], [-s, c]]))
        fbi.append(np.block([[c.T, -s.T], [s.T, c.T]]))
    c0, s0 = cs(0)
    ch, sh = cs(FFT_H)
    z = np.zeros_like(c0)
    fbs = np.block([[c0, z], [-s0, z], [z, ch], [z, -sh]])
    fbsi = np.block([[c0.T, -s0.T, z, z], [z, z, ch.T, -sh.T]])
    as32 = lambda a: np.asarray(a, np.float32)
    return as32(fa_k), as32(fai_k), as32(np.stack(fb)), as32(np.stack(fbi)), as32(fbs), as32(fbsi)


def _hyena_kernel(z_ref, gate_ref, kf_ref, skip_ref, fa_ref, fai_ref, fb_ref, fbi_ref, fbs_ref, fbsi_ref,
                  o_ref, w_scr):
    tl = z_ref.shape[-1]
    n_groups = FFT_N2 // SUBLANES
    half = FFT_N2

    def stage_a(g, carry):
        r0 = pl.multiple_of(g * SUBLANES, SUBLANES)
        zin = z_ref[:, pl.ds(r0, SUBLANES), :].reshape(FFT_H * SUBLANES, tl)
        y = _dot(fa_ref[...], zin.astype(_bf16))
        w_scr[:, pl.ds(r0, SUBLANES), :] = y.reshape(FFT_N1, SUBLANES, tl)
        return carry

    lax.fori_loop(0, n_groups, stage_a, 0, unroll=HY_UNROLL_A)

    yin = jnp.concatenate([w_scr[0], w_scr[FFT_H]], axis=0).astype(_bf16)
    x = _dot(fbs_ref[...], yin)
    pieces = []
    for q in range(2):
        xr = x[(2 * q) * half:(2 * q + 1) * half]
        xi = x[(2 * q + 1) * half:(2 * q + 2) * half]
        kr = kf_ref[pl.ds((2 * q) * half, half), :]
        ki = kf_ref[pl.ds((2 * q + 1) * half, half), :]
        pieces += [xr * kr - xi * ki, xr * ki + xi * kr]
    v = _dot(fbsi_ref[...], jnp.concatenate(pieces, axis=0).astype(_bf16))
    w_scr[0] = v[:half]
    w_scr[FFT_H] = v[half:]

    def stage_b(k1, carry):
        yin = jnp.concatenate([w_scr[k1], w_scr[FFT_H + k1]], axis=0).astype(_bf16)
        x = _dot(fb_ref[k1 - 1], yin)
        xr, xi = x[:half], x[half:]
        base = pl.multiple_of(2 * half * (k1 + 1), 2 * half)
        kr = kf_ref[pl.ds(base, half), :]
        ki = kf_ref[pl.ds(base + half, half), :]
        p = jnp.concatenate([xr * kr - xi * ki, xr * ki + xi * kr], axis=0).astype(_bf16)
        v = _dot(fbi_ref[k1 - 1], p)
        w_scr[k1] = v[:half]
        w_scr[FFT_H + k1] = v[half:]
        return carry

    lax.fori_loop(1, FFT_H, stage_b, 0, unroll=HY_UNROLL_B)

    skip = skip_ref[...]

    def stage_c(g, carry):
        r0 = pl.multiple_of(g * SUBLANES, SUBLANES)
        vin = w_scr[:, pl.ds(r0, SUBLANES), :].reshape(FFT_N1 * SUBLANES, tl)
        conv = _dot(fai_ref[...], vin.astype(_bf16)).reshape(FFT_H, SUBLANES, tl)
        zin = z_ref[:, pl.ds(r0, SUBLANES), :]
        out = gate_ref[:, pl.ds(r0, SUBLANES), :] * (conv + skip * zin)
        o_ref[:, pl.ds(r0, SUBLANES), :] = out.astype(o_ref.dtype)
        return carry

    lax.fori_loop(0, n_groups, stage_c, 0, unroll=HY_UNROLL_A)


def _single(shape, index_map):
    return pl.BlockSpec(shape, index_map, pipeline_mode=pl.Buffered(1))


def _hyena_order(z_arr, z_col, gate_arr, gate_col, kf, order, skip, consts, out_dtype):
    fa_k, fai_k, fb, fbi, fbs, fbsi = consts
    b = z_arr.shape[0]
    tl = LANE_TILE
    d_out = kf.shape[-1]
    seq_block = (None, FFT_H, FFT_N2, tl)
    const2 = lambda shape: _single(shape, lambda j, i: (0, 0))
    const3 = lambda shape: _single(shape, lambda j, i: (0, 0, 0))
    return pl.pallas_call(
        _hyena_kernel,
        grid=(d_out // tl, b),
        in_specs=[pl.BlockSpec(seq_block, lambda j, i: (i, 0, 0, z_col + j)),
                  pl.BlockSpec(seq_block, lambda j, i: (i, 0, 0, gate_col + j)),
                  _single((None, kf.shape[1], tl), lambda j, i: (order, 0, j)),
                  pl.BlockSpec((1, tl), lambda j, i: (0, j)),
                  const2(fa_k.shape), const2(fai_k.shape), const3(fb.shape), const3(fbi.shape),
                  const2(fbs.shape), const2(fbsi.shape)],
        out_specs=pl.BlockSpec(seq_block, lambda j, i: (i, 0, 0, j)),
        out_shape=jax.ShapeDtypeStruct((b, FFT_H, FFT_N2, d_out), out_dtype),
        scratch_shapes=[pltpu.VMEM((FFT_N1, FFT_N2, tl), _f32)],
        compiler_params=_params(),
        name="hyena_conv",
    )(z_arr, gate_arr, kf, skip, fa_k, fai_k, fb, fbi, fbs, fbsi)


def _filter_kernel(hdn_ref, w4f_ref, w4b_ref, delta_ref, fa_ref, fb_ref, fbs_ref, o_ref, sig_scr, w_scr):
    seq = hdn_ref.shape[0]
    tl = o_ref.shape[-1]
    half = FFT_N2
    hdn = hdn_ref[...]
    hdn_hi = hdn.astype(_bf16)
    hdn_lo = (hdn - hdn_hi.astype(_f32)).astype(_bf16)

    def dot3(w):
        w_hi = w.astype(_bf16)
        w_lo = (w - w_hi.astype(_f32)).astype(_bf16)
        return _dot(hdn_hi, w_hi) + (_dot(hdn_hi, w_lo) + _dot(hdn_lo, w_hi))

    row = lax.broadcasted_iota(jnp.int32, (seq, tl), 0)
    decay = jnp.exp(-(row.astype(_f32) * (1.0 / (seq - 1))) * delta_ref[...])
    f = dot3(w4f_ref[...]) * decay
    g = dot3(w4b_ref[...]) * decay
    g = jnp.where(row == 0, 0.0, g)
    norm = jnp.sum(jnp.abs(f), axis=0, keepdims=True) + jnp.sum(jnp.abs(g), axis=0, keepdims=True)
    f = f / norm
    g = g / norm

    for part, sig in enumerate((f + g, f - g)):
        sig_scr[...] = sig.reshape(FFT_H, FFT_N2, tl)

        def stage_a(gi, carry):
            r0 = pl.multiple_of(gi * SUBLANES, SUBLANES)
            zin = sig_scr[:, pl.ds(r0, SUBLANES), :].reshape(FFT_H * SUBLANES, tl)
            y = _dot(fa_ref[...], zin.astype(_bf16))
            w_scr[:, pl.ds(r0, SUBLANES), :] = y.reshape(FFT_N1, SUBLANES, tl)
            return carry

        lax.fori_loop(0, FFT_N2 // SUBLANES, stage_a, 0, unroll=HY_UNROLL_A)

        yin = jnp.concatenate([w_scr[0], w_scr[FFT_H]], axis=0).astype(_bf16)
        o_ref[pl.ds(part * half, half), :] = _dot(fbs_ref[pl.ds(part * half, half), :], yin)
        o_ref[pl.ds((2 + part) * half, half), :] = _dot(fbs_ref[pl.ds((2 + part) * half, half), :], yin)

        def stage_b(k1, carry):
            yin = jnp.concatenate([w_scr[k1], w_scr[FFT_H + k1]], axis=0).astype(_bf16)
            base = pl.multiple_of(2 * half * (k1 + 1) + part * half, half)
            o_ref[pl.ds(base, half), :] = _dot(fb_ref[k1 - 1, pl.ds(part * half, half), :], yin)
            return carry

        lax.fori_loop(1, FFT_H, stage_b, 0, unroll=HY_UNROLL_B)


def _filter_features(seq, hy_w1, hy_b1, hy_w2, hy_b2, hy_w3, hy_b3, hy_freq):
    t = jnp.linspace(0.0, 1.0, seq, dtype=_f32)[:, None]
    w = 2.0 * math.pi * jnp.arange(seq, dtype=_f32)[:, None] / seq
    f = jnp.linspace(1e-4, HY_BANDS - 1, HY_BANDS, dtype=_f32)[None, :]
    z = jnp.concatenate([t, jnp.cos(f * w), -jnp.sin(f * w)], axis=-1)
    hi = lax.Precision.HIGHEST
    hdn = jnp.sin(hy_freq * (jnp.dot(z, hy_w1, precision=hi) + hy_b1))
    hdn = jnp.sin(hy_freq * (jnp.dot(hdn, hy_w2, precision=hi) + hy_b2))
    return jnp.sin(hy_freq * (jnp.dot(hdn, hy_w3, precision=hi) + hy_b3))


def _filter_spectra(hdn, hy_w4, consts):
    fa_k, _, fb, _, fbs, _ = consts
    seq, width = hdn.shape
    d = hy_w4.shape[1] // (2 * HY_ORDER)
    tl = LANE_TILE
    w4 = hy_w4.reshape(width, 2 * HY_ORDER, d).transpose(1, 0, 2)
    max_decay = math.log(HY_TARGET) / HY_FAST_DECAY
    min_decay = math.log(HY_TARGET) / HY_SLOW_DECAY
    deltas = jnp.abs(jnp.linspace(min_decay, max_decay, d, dtype=_f32)).reshape(1, d)
    n_rows = (FFT_H + 1) * 2 * FFT_N2
    const2 = lambda shape: _single(shape, lambda o, j: (0, 0))
    return pl.pallas_call(
        _filter_kernel,
        grid=(HY_ORDER, d // tl),
        in_specs=[const2(hdn.shape),
                  pl.BlockSpec((None, width, tl), lambda o, j: (2 * o, 0, j)),
                  pl.BlockSpec((None, width, tl), lambda o, j: (2 * o + 1, 0, j)),
                  pl.BlockSpec((1, tl), lambda o, j: (0, j)),
                  const2(fa_k.shape), _single(fb.shape, lambda o, j: (0, 0, 0)), const2(fbs.shape)],
        out_specs=pl.BlockSpec((None, n_rows, tl), lambda o, j: (o, 0, j)),
        out_shape=jax.ShapeDtypeStruct((HY_ORDER, n_rows, d), _f32),
        scratch_shapes=[pltpu.VMEM((FFT_H, FFT_N2, tl), _f32), pltpu.VMEM((FFT_N1, FFT_N2, tl), _f32)],
        compiler_params=_params(),
        name="filter_spectra",
    )(hdn, w4, w4, deltas, fa_k, fb, fbs)


def _merge_kernel(hx_ref, yr_ref, yh_ref, x_ref, wbg_ref, bbg_ref, wa_ref, wb_ref, wo_ref, g1_ref,
                  g_ref, sh_ref, sc_ref, x1_ref, h2_ref):
    d = x_ref.shape[-1]
    gates = _sigmoid(_dot(hx_ref[...], wbg_ref[...]) + bbg_ref[...])
    ya = _dot(yr_ref[...], wa_ref[...])
    yb = _dot(yh_ref[...].astype(_bf16), wb_ref[...])
    merged = gates[:, :d] * ya + gates[:, d:] * yb
    x1 = x_ref[...] + g1_ref[...] * _dot(merged.astype(_bf16), wo_ref[...])
    x1_ref[...] = x1
    y = x1 * lax.rsqrt(jnp.mean(x1 * x1, axis=-1, keepdims=True) + RMS_EPS) * g_ref[...]
    h2_ref[...] = (y * (1.0 + sc_ref[...]) + sh_ref[...]).astype(h2_ref.dtype)


def _merge(hx, y_rnn, y_hy, x, w_bg, b_bg, w_a, w_b, w_o, g1, norm2_g, sh2, sc2, tm=512):
    b, l, d = x.shape
    row = lambda width: pl.BlockSpec((None, tm, width), lambda i, j: (i, j, 0))
    vec = pl.BlockSpec((None, 1, d), lambda i, j: (i, 0, 0))
    full = lambda a: pl.BlockSpec(a.shape, lambda i, j: (0,) * a.ndim)
    b_bg = b_bg.reshape(1, -1)
    norm2_g = norm2_g.reshape(1, d)
    return pl.pallas_call(
        _merge_kernel,
        grid=(b, l // tm),
        in_specs=[row(d), row(d), row(d), row(d), full(w_bg), full(b_bg), full(w_a), full(w_b), full(w_o),
                  vec, full(norm2_g), vec, vec],
        out_specs=[row(d), row(d)],
        out_shape=[jax.ShapeDtypeStruct((b, l, d), _f32), jax.ShapeDtypeStruct((b, l, d), _bf16)],
        compiler_params=_params(),
        name="merge",
    )(hx, y_rnn, y_hy, x, w_bg, b_bg, w_a, w_b, w_o, g1, norm2_g, sh2, sc2)


def _ffn_kernel(h_ref, x1_ref, wg_ref, wu_ref, wo_ref, g2_ref, fg_ref, o_ref):
    h = h_ref[...]
    g = _dot(h, wg_ref[...])
    u = _dot(h, wu_ref[...])
    a = (g * _sigmoid(g) * u).astype(_bf16)
    x2 = x1_ref[...] + g2_ref[...] * _dot(a, wo_ref[...])
    y = x2 * lax.rsqrt(jnp.mean(x2 * x2, axis=-1, keepdims=True) + RMS_EPS)
    o_ref[...] = y * fg_ref[...]


def _ffn(h2, x1, w_g, w_u, w_o, g2, final_g, tm=512):
    b, l, d = x1.shape
    row = pl.BlockSpec((None, tm, d), lambda i, j: (i, j, 0))
    vec = pl.BlockSpec((None, 1, d), lambda i, j: (i, 0, 0))
    full = lambda a: _single(a.shape, lambda i, j: (0,) * a.ndim)
    final_g = final_g.reshape(1, d)
    return pl.pallas_call(
        _ffn_kernel,
        grid=(b, l // tm),
        in_specs=[row, row, full(w_g), full(w_u), full(w_o), vec, full(final_g)],
        out_specs=row,
        out_shape=jax.ShapeDtypeStruct((b, l, d), _f32),
        compiler_params=_params(),
        name="ffn",
    )(h2, x1, w_g, w_u, w_o, g2, final_g)


def kernel(x, c, ctx, c_ctx, w_mod, b_mod, norm1_g, norm2_g, w_in, b_in, rnn_conv_w, rnn_conv_b, rg_wa, rg_ba,
           rg_wx, rg_bx, rg_lambda, hy_conv_w, hy_conv_b, hy_w1, hy_b1, hy_w2, hy_b2, hy_w3, hy_b3, hy_freq,
           hy_w4, hy_skip, w_a_out, w_b_out, w_out, w_ffn_in, w_ffn_out, final_g):
    assert w_mod.shape[0] == 1, "single-layer block"
    b, seq, d = x.shape
    ctx_len = ctx.shape[1]
    assert seq == FFT_H * FFT_N2 and d == D_MODEL
    assert b == SUBLANES, "time-major rows put the batch on the sublanes of one register"
    (w_mod, b_mod, norm1_g, norm2_g, w_in, b_in, rnn_conv_w, rnn_conv_b, rg_wa, rg_ba, rg_wx, rg_bx,
     rg_lambda, hy_conv_w, hy_conv_b, hy_w1, hy_b1, hy_w2, hy_b2, hy_w3, hy_b3, hy_freq, hy_w4, hy_skip,
     w_a_out, w_b_out, w_out, w_ffn_in, w_ffn_out) = [
        a[0] for a in (w_mod, b_mod, norm1_g, norm2_g, w_in, b_in, rnn_conv_w, rnn_conv_b, rg_wa, rg_ba,
                       rg_wx, rg_bx, rg_lambda, hy_conv_w, hy_conv_b, hy_w1, hy_b1, hy_w2, hy_b2, hy_w3,
                       hy_b3, hy_freq, hy_w4, hy_skip, w_a_out, w_b_out, w_out, w_ffn_in, w_ffn_out)]

    pad_rows = 2 * SUBLANES - b - 1
    c_all = jnp.concatenate([c, c_ctx[None, :], jnp.zeros((pad_rows, d), _f32)], axis=0)
    mod = _mod_vectors(c_all, w_mod, b_mod)
    sh1, sc1, g1, sh2, sc2, g2 = [m.reshape(b, 1, d) for m in jnp.split(mod[:b], N_MOD, axis=-1)]
    csh1, csc1 = [jnp.broadcast_to(m.reshape(1, 1, d), (b, 1, d))
                  for m in jnp.split(mod[b], N_MOD, axis=-1)[:2]]

    w_in_bf = w_in.astype(_bf16)
    w_rx, b_rx = w_in_bf[:, :d], b_in[:d]
    w_rg, b_rg = w_in_bf[:, d:2 * d], b_in[d:2 * d]
    w_hy, b_hy = w_in_bf[:, 2 * d:5 * d], b_in[2 * d:5 * d]
    w_bg, b_bg = w_in_bf[:, 5 * d:], b_in[5 * d:]
    gate_w, gate_bias = zip(*[_gate_weights(rg_wa[i], rg_wx[i], rg_ba[i], rg_bx[i]) for i in range(2)])
    gate_w, gate_bias = jnp.stack(gate_w), jnp.stack(gate_bias)
    lam = rg_lambda.reshape(2, 1, d)
    zeros_state = jnp.zeros((b, d), _f32)

    _, hc_tb = _norm_mod(ctx, norm1_g, csh1, csc1)
    u_c = _proj(hc_tb, w_rx, b_rx, mode="conv", conv_w=rnn_conv_w, conv_b=rnn_conv_b,
                pad_left=RNN_CONV_PAD_LEFT, period=ctx_len, row_stride=b, tm=ctx_len * b)
    _, cf, cb = _rglru(u_c, jnp.zeros_like(u_c), b, gate_w, gate_bias, lam, zeros_state, zeros_state)

    hx, hx_tb = _norm_mod(x, norm1_g, sh1, sc1)
    u = _proj(hx_tb, w_rx, b_rx, mode="conv", conv_w=rnn_conv_w, conv_b=rnn_conv_b,
              pad_left=RNN_CONV_PAD_LEFT, row_stride=b)
    grg = _proj(hx_tb, w_rg, b_rg, mode="gelu")
    q = _proj(hx.reshape(b * seq, d), w_hy, b_hy, mode="conv", conv_w=hy_conv_w, conv_b=hy_conv_b,
              pad_left=HY_CONV_PAD_LEFT, tm=512, tn=3 * d)
    y_rnn, _, _ = _rglru(u, grg, b, gate_w, gate_bias, lam, cf, cb)

    consts = [jnp.asarray(a).astype(_bf16) for a in _dft_constants()]
    hdn = _filter_features(seq, hy_w1, hy_b1, hy_w2, hy_b2, hy_w3, hy_b3, hy_freq)
    kf = _filter_spectra(hdn, hy_w4, consts)
    q4 = q.reshape(b, FFT_H, FFT_N2, 3 * d)
    n_tiles = d // LANE_TILE
    z1 = _hyena_order(q4, 0, q4, n_tiles, kf, 0, hy_skip[0:1], consts, _f32)
    y_hy = _hyena_order(z1, 0, q4, 2 * n_tiles, kf, 1, hy_skip[1:2], consts, _f32)
    y_hy = y_hy.reshape(b, seq, d)

    x1, h2 = _merge(hx, y_rnn, y_hy, x, w_bg, b_bg, w_a_out.astype(_bf16), w_b_out.astype(_bf16),
                    w_out.astype(_bf16), g1, norm2_g, sh2, sc2)
    w_ffn_bf = w_ffn_in.astype(_bf16)
    return _ffn(h2, x1, w_ffn_bf[:, :D_FF], w_ffn_bf[:, D_FF:], w_ffn_out.astype(_bf16), g2, final_g)
```

```python
import functools
import math

import numpy as np
import jax
import jax.numpy as jnp
from jax import lax
from jax.experimental import pallas as pl
from jax.experimental.pallas import tpu as pltpu

D_MODEL = 1024
GRID_W = 64
N_MOD = 6
RMS_EPS = 1e-6
RNN_HEADS = 16
RNN_HEAD_DIM = D_MODEL // RNN_HEADS
RNN_CONV_PAD_LEFT = 2
HY_CONV_PAD_LEFT = 1
RG_C = 8.0
HY_ORDER = 2
HY_EMB = 33
HY_BANDS = (HY_EMB - 1) // 2
HY_FAST_DECAY = 0.3
HY_SLOW_DECAY = 1.5
HY_TARGET = 1e-2
D_FF = ((8 * D_MODEL // 3 + 255) // 256) * 256

SUBLANES = 8
LANE_TILE = 256
VMEM_LIMIT = 56 * 1024 * 1024

FFT_N1 = 64
FFT_N2 = 128
FFT_N = FFT_N1 * FFT_N2
FFT_H = FFT_N1 // 2

_f32 = jnp.float32
_bf16 = jnp.bfloat16


def _params(**kw):
    return pltpu.CompilerParams(vmem_limit_bytes=VMEM_LIMIT, **kw)


def _dot(a, b):
    return jnp.dot(a, b, preferred_element_type=_f32)


def _sigmoid(x):
    return 0.5 * jnp.tanh(0.5 * x) + 0.5


def _mod_kernel(c_ref, w_ref, b_ref, o_ref):
    c = c_ref[...]
    s = c * _sigmoid(c)
    o_ref[...] = jnp.dot(s, w_ref[...], preferred_element_type=_f32,
                         precision=lax.Precision.HIGHEST) + b_ref[...]


def _mod_vectors(c_all, w_mod, b_mod):
    rows, d = c_all.shape
    n = w_mod.shape[1]
    tn = 1024
    return pl.pallas_call(
        _mod_kernel,
        grid=(n // tn,),
        in_specs=[pl.BlockSpec((rows, d), lambda j: (0, 0)),
                  pl.BlockSpec((d, tn), lambda j: (0, j)),
                  pl.BlockSpec((1, tn), lambda j: (0, j))],
        out_specs=pl.BlockSpec((rows, tn), lambda j: (0, j)),
        out_shape=jax.ShapeDtypeStruct((rows, n), _f32),
        compiler_params=_params(),
        name="mod_vectors",
    )(c_all, w_mod, b_mod.reshape(1, n))


LANES = 128
NORM_TT = 128


def _norm_mod_kernel(x_ref, g_ref, sh_ref, sc_ref, o_ref, otb_ref, tb_scr):
    nb, tt, d = x_ref.shape
    for b in range(nb):
        x = x_ref[b]
        gain = g_ref[...] * (1.0 + sc_ref[b])
        y = x * lax.rsqrt(jnp.mean(x * x, axis=-1, keepdims=True) + RMS_EPS) * gain + sh_ref[b]
        o_ref[b] = y.astype(o_ref.dtype)
        for s in range(d // LANES):
            tb_scr.at[s][pl.ds(b, tt, stride=nb), :] = y[:, s * LANES:(s + 1) * LANES]
    for s in range(d // LANES):
        otb_ref[:, s * LANES:(s + 1) * LANES] = tb_scr[s].astype(otb_ref.dtype)


def _norm_mod(x, g, sh, sc):
    b, l, d = x.shape
    tt = NORM_TT
    vec = pl.BlockSpec((b, 1, d), lambda i: (0, 0, 0))
    return pl.pallas_call(
        _norm_mod_kernel,
        grid=(l // tt,),
        in_specs=[pl.BlockSpec((b, tt, d), lambda i: (0, i, 0)),
                  pl.BlockSpec((1, d), lambda i: (0, 0)), vec, vec],
        out_specs=[pl.BlockSpec((b, tt, d), lambda i: (0, i, 0)),
                   pl.BlockSpec((tt * b, d), lambda i: (i, 0))],
        out_shape=[jax.ShapeDtypeStruct((b, l, d), _bf16), jax.ShapeDtypeStruct((l * b, d), _bf16)],
        scratch_shapes=[pltpu.VMEM((d // LANES, tt * b, LANES), _f32)],
        compiler_params=_params(),
        name="norm_mod",
    )(x, g.reshape(1, d), sh, sc)


def _conv_tap_weights(w, pad_left, rows, row_stride):
    if row_stride != 1:
        return [w[k:k + 1, :] for k in range(w.shape[0])]
    ridx = lax.broadcasted_iota(jnp.int32, (rows, w.shape[1]), 0)
    taps = []
    for k in range(w.shape[0]):
        off = k - pad_left
        valid = (ridx + off >= 0) & (ridx + off < rows)
        taps.append(jnp.where(valid, w[k:k + 1, :], 0.0))
    return taps


def _short_conv_period(y, taps, cb, pad_left, row_stride):
    rows = y.shape[0]
    n_taps = len(taps)
    if row_stride != 1:
        lanes = y.shape[1]
        ypad = jnp.concatenate([jnp.zeros((pad_left * row_stride, lanes), y.dtype), y,
                                jnp.zeros(((n_taps - 1 - pad_left) * row_stride, lanes), y.dtype)], axis=0)
        out = cb
        for k in range(n_taps):
            out = out + ypad[k * row_stride:k * row_stride + rows] * taps[k]
        return out
    out = y * taps[pad_left] + cb
    for k in range(n_taps):
        off = k - pad_left
        if off != 0:
            out = out + pltpu.roll(y, (-off) % rows, axis=0) * taps[k]
    return out


def _proj_kernel(x_ref, w_ref, b_ref, *rest, mode, pad_left, period, row_stride):
    o_ref = rest[-1]
    tm, tn = o_ref.shape
    for j in range(tn // LANE_TILE):
        cols = slice(j * LANE_TILE, (j + 1) * LANE_TILE)
        y = _dot(x_ref[...], w_ref[:, cols]) + b_ref[:, cols]
        if mode == "conv":
            prow = period * row_stride
            taps = _conv_tap_weights(rest[0][:, cols], pad_left, prow, row_stride)
            cb = rest[1][:, cols]
            for p in range(tm // prow):
                rows = slice(p * prow, (p + 1) * prow)
                o_ref[rows, cols] = _short_conv_period(y[rows], taps, cb, pad_left, row_stride).astype(o_ref.dtype)
        elif mode == "gelu":
            o_ref[:, cols] = jax.nn.gelu(y).astype(o_ref.dtype)
        else:
            o_ref[:, cols] = y.astype(o_ref.dtype)


def _proj(x2d, w, bias, *, mode, conv_w=None, conv_b=None, pad_left=0, period=GRID_W, row_stride=1,
          tm=1024, tn=1024, out_dtype=_f32):
    assert tm % (period * row_stride) == 0 or mode != "conv"
    m, k = x2d.shape
    n = w.shape[1]
    in_specs = [pl.BlockSpec((tm, k), lambda j, i: (i, 0)),
                pl.BlockSpec((k, tn), lambda j, i: (0, j)),
                pl.BlockSpec((1, tn), lambda j, i: (0, j))]
    args = [x2d, w, bias.reshape(1, n)]
    if mode == "conv":
        taps = conv_w.shape[0]
        in_specs += [pl.BlockSpec((taps, tn), lambda j, i: (0, j)),
                     pl.BlockSpec((1, tn), lambda j, i: (0, j))]
        args += [conv_w, conv_b.reshape(1, n)]
    return pl.pallas_call(
        functools.partial(_proj_kernel, mode=mode, pad_left=pad_left, period=period, row_stride=row_stride),
        grid=(n // tn, m // tm),
        in_specs=in_specs,
        out_specs=pl.BlockSpec((tm, tn), lambda j, i: (i, j)),
        out_shape=jax.ShapeDtypeStruct((m, n), out_dtype),
        compiler_params=_params(),
        name="proj_" + mode,
    )(*args)


def _proj_conv_bt_kernel(x_ref, w_ref, b_ref, cw_ref, cb_ref, o_ref, tb_scr, *, pad_left):
    nb, tt, tn = o_ref.shape
    for j in range(tn // LANE_TILE):
        cols = slice(j * LANE_TILE, (j + 1) * LANE_TILE)
        y = _dot(x_ref[...], w_ref[:, cols]) + b_ref[:, cols]
        taps = _conv_tap_weights(cw_ref[:, cols], pad_left, tt * nb, nb)
        y = _short_conv_period(y, taps, cb_ref[:, cols], pad_left, nb)
        for s in range(LANE_TILE // LANES):
            slab = j * (LANE_TILE // LANES) + s
            tb_scr[slab] = y[:, s * LANES:(s + 1) * LANES]
            for bi in range(nb):
                piece = tb_scr.at[slab][pl.ds(bi, tt, stride=nb), :]
                o_ref[bi, :, slab * LANES:(slab + 1) * LANES] = piece.astype(o_ref.dtype)


def _proj_conv_bt(x_tb, nb, w, bias, conv_w, conv_b, pad_left, period, out_dtype=_f32):
    rows, k = x_tb.shape
    seq = rows // nb
    n = w.shape[1]
    taps = conv_w.shape[0]
    full = lambda shape: pl.BlockSpec(shape, lambda i: (0, 0))
    return pl.pallas_call(
        functools.partial(_proj_conv_bt_kernel, pad_left=pad_left),
        grid=(seq // period,),
        in_specs=[pl.BlockSpec((period * nb, k), lambda i: (i, 0)),
                  full((k, n)), full((1, n)), full((taps, n)), full((1, n))],
        out_specs=pl.BlockSpec((nb, period, n), lambda i: (0, i, 0)),
        out_shape=jax.ShapeDtypeStruct((nb, seq, n), out_dtype),
        scratch_shapes=[pltpu.VMEM((n // LANES, period * nb, LANES), _f32)],
        compiler_params=_params(),
        name="proj_conv_bt",
    )(x_tb, w, bias.reshape(1, n), conv_w, conv_b.reshape(1, n))


RNN_TT = 512
RNN_SUB_TT = 64


def _rnn_coeffs(u, w_ref, bias_ref, lam_ref):
    tl = u.shape[-1]
    half_decay = (-0.5 * RG_C) * jax.nn.softplus(-lam_ref[...])
    g = _dot(u.astype(_bf16), w_ref[...]) + bias_ref[...]
    log_a = half_decay * jnp.tanh(g[:, :tl]) + half_decay
    i = 0.5 * jnp.tanh(g[:, tl:]) + 0.5
    a = jnp.exp(log_a)
    x = -jnp.tanh(log_a) * (a * a + 1.0)
    root = jnp.where(x > 0.0, x * lax.rsqrt(x), 0.0)
    return a, root * (i * u.astype(_f32))


def _rnn_scan(u_ref, w_ref, bias_ref, lam_ref, a_scr, b_scr, h_ref, h0, reverse):
    nb = h0.shape[0]
    rows = u_ref.shape[0]
    sub = RNN_SUB_TT * nb
    n_sub = rows // sub
    order = list(range(n_sub))[::-1] if reverse else list(range(n_sub))

    def coeffs(k):
        sl = slice(k * sub, (k + 1) * sub)
        a, b = _rnn_coeffs(u_ref[sl, :], w_ref, bias_ref, lam_ref)
        a_scr[sl, :] = a
        b_scr[sl, :] = b

    coeffs(order[0])
    h = h0
    for pos, k in enumerate(order):
        if pos + 1 < n_sub:
            coeffs(order[pos + 1])
        steps = list(range(k * sub, (k + 1) * sub, nb))
        for r0 in (steps[::-1] if reverse else steps):
            h = a_scr[r0:r0 + nb, :] * h + b_scr[r0:r0 + nb, :]
            h_ref[r0:r0 + nb, :] = h
    return h


def _rnn_fwd_kernel(u_ref, w_ref, bias_ref, lam_ref, h0_ref, hf_ref, hlast_ref, a_scr, b_scr, h_scr):
    @pl.when(pl.program_id(1) == 0)
    def _():
        h_scr[...] = h0_ref[...]

    h = _rnn_scan(u_ref, w_ref, bias_ref, lam_ref, a_scr, b_scr, hf_ref, h_scr[...], False)
    h_scr[...] = h
    hlast_ref[...] = h


def _rnn_bwd_kernel(u_ref, grg_ref, hf_ref, w_ref, bias_ref, lam_ref, h0_ref, y_ref, hfirst_ref,
                    a_scr, b_scr, h_scr, y_scr):
    @pl.when(pl.program_id(1) == 0)
    def _():
        h_scr[...] = h0_ref[...]

    nb, tt, tl = y_ref.shape
    h = _rnn_scan(u_ref, w_ref, bias_ref, lam_ref, a_scr, b_scr, b_scr, h_scr[...], True)
    h_scr[...] = h
    hfirst_ref[...] = h
    y = (hf_ref[...] + b_scr[...]) * grg_ref[...].astype(_f32)
    for s in range(tl // LANES):
        y_scr[s] = y[:, s * LANES:(s + 1) * LANES]
    for bi in range(nb):
        for s in range(tl // LANES):
            piece = y_scr.at[s][pl.ds(bi, tt, stride=nb), :]
            y_ref[bi, :, s * LANES:(s + 1) * LANES] = piece.astype(y_ref.dtype)


def _rglru(u_tb, grg_tb, nb, gate_w, gate_bias, lam, h0f, h0b):
    rows, d = u_tb.shape
    seq = rows // nb
    tl = LANE_TILE
    tt = min(RNN_TT, seq)
    n_chunks = seq // tt
    chunk = tt * nb
    state = pl.BlockSpec((nb, tl), lambda j, c: (0, j))
    scratch = [pltpu.VMEM((chunk, tl), _f32), pltpu.VMEM((chunk, tl), _f32), pltpu.VMEM((nb, tl), _f32)]

    def param_specs(direction):
        return [pl.BlockSpec((None, None, tl, 2 * tl), lambda j, c: (direction, j, 0, 0)),
                pl.BlockSpec((None, None, 1, 2 * tl), lambda j, c: (direction, j, 0, 0)),
                pl.BlockSpec((None, 1, tl), lambda j, c: (direction, 0, j))]

    fwd_rows = pl.BlockSpec((chunk, tl), lambda j, c: (c, j))
    hf_tb, hf_last = pl.pallas_call(
        _rnn_fwd_kernel,
        grid=(d // tl, n_chunks),
        in_specs=[fwd_rows] + param_specs(0) + [state],
        out_specs=[fwd_rows, state],
        out_shape=[jax.ShapeDtypeStruct((rows, d), _f32), jax.ShapeDtypeStruct((nb, d), _f32)],
        scratch_shapes=scratch,
        compiler_params=_params(),
        name="rglru_fwd",
    )(u_tb, gate_w, gate_bias, lam, h0f)

    bwd_rows = pl.BlockSpec((chunk, tl), lambda j, c: (n_chunks - 1 - c, j))
    y, hb_first = pl.pallas_call(
        _rnn_bwd_kernel,
        grid=(d // tl, n_chunks),
        in_specs=[bwd_rows, bwd_rows, bwd_rows] + param_specs(1) + [state],
        out_specs=[pl.BlockSpec((nb, tt, tl), lambda j, c: (0, n_chunks - 1 - c, j)), state],
        out_shape=[jax.ShapeDtypeStruct((nb, seq, d), _bf16), jax.ShapeDtypeStruct((nb, d), _f32)],
        scratch_shapes=scratch + [pltpu.VMEM((tl // LANES, chunk, LANES), _f32)],
        compiler_params=_params(),
        name="rglru_bwd",
    )(u_tb, grg_tb, hf_tb, gate_w, gate_bias, lam, h0b)
    return y, hf_last, hb_first


def _gate_weights(wa, wx, ba, bx):
    heads_per_tile = LANE_TILE // RNN_HEAD_DIM
    n_tiles = RNN_HEADS // heads_per_tile

    def tile_blockdiag(w):
        w = w.reshape(n_tiles, heads_per_tile, RNN_HEAD_DIM, RNN_HEAD_DIM)
        eye = jnp.eye(heads_per_tile, dtype=w.dtype)
        full = jnp.einsum('thij,hg->thigj', w, eye)
        return full.reshape(n_tiles, LANE_TILE, LANE_TILE)

    w = (0.5 * jnp.concatenate([tile_blockdiag(wa), tile_blockdiag(wx)], axis=-1)).astype(_bf16)
    bias = jnp.concatenate([ba.reshape(n_tiles, 1, LANE_TILE), bx.reshape(n_tiles, 1, LANE_TILE)], axis=-1)
    return w, (0.5 * bias).astype(_f32)


HY_UNROLL_A = 16
HY_UNROLL_B = 31


def _dft_constants():
    n1 = np.arange(FFT_H)
    k1 = np.arange(FFT_H + 1)
    ang = 2.0 * np.pi * np.outer(k1, n1) / FFT_N1
    fa = np.concatenate([np.cos(ang), -np.sin(ang)[1:FFT_H]], axis=0)
    fa_k = np.kron(fa, np.eye(SUBLANES))
    weight = np.where((k1 == 0) | (k1 == FFT_H), 1.0, 2.0)[:, None] / FFT_N
    fai = np.concatenate([weight * np.cos(ang), (-2.0 / FFT_N) * np.sin(ang)[1:FFT_H]], axis=0).T
    fai_k = np.kron(fai, np.eye(SUBLANES))

    n2 = np.arange(FFT_N2)
    k2 = np.arange(FFT_N2)

    def cs(k1v):
        idx = (np.outer(k2, n2) * FFT_N1 + k1v * n2[None, :]) % FFT_N
        phi = 2.0 * np.pi * idx / FFT_N
        return np.cos(phi), np.sin(phi)

    fb, fbi = [], []
    for k1v in range(1, FFT_H):
        c, s = cs(k1v)
        fb.append(np.block([[c, s], [-s, c]]))
        fbi.append(np.block([[c.T, -s.T], [s.T, c.T]]))
    c0, s0 = cs(0)
    ch, sh = cs(FFT_H)
    z = np.zeros_like(c0)
    fbs = np.block([[c0, z], [-s0, z], [z, ch], [z, -sh]])
    fbsi = np.block([[c0.T, -s0.T, z, z], [z, z, ch.T, -sh.T]])
    as32 = lambda a: np.asarray(a, np.float32)
    return as32(fa_k), as32(fai_k), as32(np.stack(fb)), as32(np.stack(fbi)), as32(fbs), as32(fbsi)


def _hyena_kernel(z_ref, gate_ref, kf_ref, skip_ref, fa_ref, fai_ref, fb_ref, fbi_ref, fbs_ref, fbsi_ref,
                  o_ref, w_scr):
    tl = z_ref.shape[-1]
    n_groups = FFT_N2 // SUBLANES
    half = FFT_N2

    def stage_a(g, carry):
        r0 = pl.multiple_of(g * SUBLANES, SUBLANES)
        zin = z_ref[:, pl.ds(r0, SUBLANES), :].reshape(FFT_H * SUBLANES, tl)
        y = _dot(fa_ref[...], zin.astype(_bf16))
        w_scr[:, pl.ds(r0, SUBLANES), :] = y.reshape(FFT_N1, SUBLANES, tl)
        return carry

    lax.fori_loop(0, n_groups, stage_a, 0, unroll=HY_UNROLL_A)

    yin = jnp.concatenate([w_scr[0], w_scr[FFT_H]], axis=0).astype(_bf16)
    x = _dot(fbs_ref[...], yin)
    pieces = []
    for q in range(2):
        xr = x[(2 * q) * half:(2 * q + 1) * half]
        xi = x[(2 * q + 1) * half:(2 * q + 2) * half]
        kr = kf_ref[pl.ds((2 * q) * half, half), :]
        ki = kf_ref[pl.ds((2 * q + 1) * half, half), :]
        pieces += [xr * kr - xi * ki, xr * ki + xi * kr]
    v = _dot(fbsi_ref[...], jnp.concatenate(pieces, axis=0).astype(_bf16))
    w_scr[0] = v[:half]
    w_scr[FFT_H] = v[half:]

    def stage_b(k1, carry):
        yin = jnp.concatenate([w_scr[k1], w_scr[FFT_H + k1]], axis=0).astype(_bf16)
        x = _dot(fb_ref[k1 - 1], yin)
        xr, xi = x[:half], x[half:]
        base = pl.multiple_of(2 * half * (k1 + 1), 2 * half)
        kr = kf_ref[pl.ds(base, half), :]
        ki = kf_ref[pl.ds(base + half, half), :]
        p = jnp.concatenate([xr * kr - xi * ki, xr * ki + xi * kr], axis=0).astype(_bf16)
        v = _dot(fbi_ref[k1 - 1], p)
        w_scr[k1] = v[:half]
        w_scr[FFT_H + k1] = v[half:]
        return carry

    lax.fori_loop(1, FFT_H, stage_b, 0, unroll=HY_UNROLL_B)

    skip = skip_ref[...]

    def stage_c(g, carry):
        r0 = pl.multiple_of(g * SUBLANES, SUBLANES)
        vin = w_scr[:, pl.ds(r0, SUBLANES), :].reshape(FFT_N1 * SUBLANES, tl)
        conv = _dot(fai_ref[...], vin.astype(_bf16)).reshape(FFT_H, SUBLANES, tl)
        zin = z_ref[:, pl.ds(r0, SUBLANES), :]
        out = gate_ref[:, pl.ds(r0, SUBLANES), :] * (conv + skip * zin)
        o_ref[:, pl.ds(r0, SUBLANES), :] = out.astype(o_ref.dtype)
        return carry

    lax.fori_loop(0, n_groups, stage_c, 0, unroll=HY_UNROLL_A)


def _single(shape, index_map):
    return pl.BlockSpec(shape, index_map, pipeline_mode=pl.Buffered(1))


def _hyena_order(z_arr, z_col, gate_arr, gate_col, kf, order, skip, consts, out_dtype):
    fa_k, fai_k, fb, fbi, fbs, fbsi = consts
    b = z_arr.shape[0]
    tl = LANE_TILE
    d_out = kf.shape[-1]
    seq_block = (None, FFT_H, FFT_N2, tl)
    const2 = lambda shape: _single(shape, lambda j, i: (0, 0))
    const3 = lambda shape: _single(shape, lambda j, i: (0, 0, 0))
    return pl.pallas_call(
        _hyena_kernel,
        grid=(d_out // tl, b),
        in_specs=[pl.BlockSpec(seq_block, lambda j, i: (i, 0, 0, z_col + j)),
                  pl.BlockSpec(seq_block, lambda j, i: (i, 0, 0, gate_col + j)),
                  _single((None, kf.shape[1], tl), lambda j, i: (order, 0, j)),
                  pl.BlockSpec((1, tl), lambda j, i: (0, j)),
                  const2(fa_k.shape), const2(fai_k.shape), const3(fb.shape), const3(fbi.shape),
                  const2(fbs.shape), const2(fbsi.shape)],
        out_specs=pl.BlockSpec(seq_block, lambda j, i: (i, 0, 0, j)),
        out_shape=jax.ShapeDtypeStruct((b, FFT_H, FFT_N2, d_out), out_dtype),
        scratch_shapes=[pltpu.VMEM((FFT_N1, FFT_N2, tl), _f32)],
        compiler_params=_params(),
        name="hyena_conv",
    )(z_arr, gate_arr, kf, skip, fa_k, fai_k, fb, fbi, fbs, fbsi)


def _filter_kernel(hdn_ref, w4f_ref, w4b_ref, delta_ref, fa_ref, fb_ref, fbs_ref, o_ref, sig_scr, w_scr):
    seq = hdn_ref.shape[0]
    tl = o_ref.shape[-1]
    half = FFT_N2
    hdn = hdn_ref[...]
    hdn_hi = hdn.astype(_bf16)
    hdn_lo = (hdn - hdn_hi.astype(_f32)).astype(_bf16)

    def dot3(w):
        w_hi = w.astype(_bf16)
        w_lo = (w - w_hi.astype(_f32)).astype(_bf16)
        return _dot(hdn_hi, w_hi) + (_dot(hdn_hi, w_lo) + _dot(hdn_lo, w_hi))

    row = lax.broadcasted_iota(jnp.int32, (seq, tl), 0)
    decay = jnp.exp(-(row.astype(_f32) * (1.0 / (seq - 1))) * delta_ref[...])
    f = dot3(w4f_ref[...]) * decay
    g = dot3(w4b_ref[...]) * decay
    g = jnp.where(row == 0, 0.0, g)
    norm = jnp.sum(jnp.abs(f), axis=0, keepdims=True) + jnp.sum(jnp.abs(g), axis=0, keepdims=True)
    f = f / norm
    g = g / norm

    for part, sig in enumerate((f + g, f - g)):
        sig_scr[...] = sig.reshape(FFT_H, FFT_N2, tl)

        def stage_a(gi, carry):
            r0 = pl.multiple_of(gi * SUBLANES, SUBLANES)
            zin = sig_scr[:, pl.ds(r0, SUBLANES), :].reshape(FFT_H * SUBLANES, tl)
            y = _dot(fa_ref[...], zin.astype(_bf16))
            w_scr[:, pl.ds(r0, SUBLANES), :] = y.reshape(FFT_N1, SUBLANES, tl)
            return carry

        lax.fori_loop(0, FFT_N2 // SUBLANES, stage_a, 0, unroll=HY_UNROLL_A)

        yin = jnp.concatenate([w_scr[0], w_scr[FFT_H]], axis=0).astype(_bf16)
        o_ref[pl.ds(part * half, half), :] = _dot(fbs_ref[pl.ds(part * half, half), :], yin)
        o_ref[pl.ds((2 + part) * half, half), :] = _dot(fbs_ref[pl.ds((2 + part) * half, half), :], yin)

        def stage_b(k1, carry):
            yin = jnp.concatenate([w_scr[k1], w_scr[FFT_H + k1]], axis=0).astype(_bf16)
            base = pl.multiple_of(2 * half * (k1 + 1) + part * half, half)
            o_ref[pl.ds(base, half), :] = _dot(fb_ref[k1 - 1, pl.ds(part * half, half), :], yin)
            return carry

        lax.fori_loop(1, FFT_H, stage_b, 0, unroll=HY_UNROLL_B)


def _filter_features(seq, hy_w1, hy_b1, hy_w2, hy_b2, hy_w3, hy_b3, hy_freq):
    t = jnp.linspace(0.0, 1.0, seq, dtype=_f32)[:, None]
    w = 2.0 * math.pi * jnp.arange(seq, dtype=_f32)[:, None] / seq
    f = jnp.linspace(1e-4, HY_BANDS - 1, HY_BANDS, dtype=_f32)[None, :]
    z = jnp.concatenate([t, jnp.cos(f * w), -jnp.sin(f * w)], axis=-1)
    hi = lax.Precision.HIGHEST
    hdn = jnp.sin(hy_freq * (jnp.dot(z, hy_w1, precision=hi) + hy_b1))
    hdn = jnp.sin(hy_freq * (jnp.dot(hdn, hy_w2, precision=hi) + hy_b2))
    return jnp.sin(hy_freq * (jnp.dot(hdn, hy_w3, precision=hi) + hy_b3))


def _filter_spectra(hdn, hy_w4, consts):
    fa_k, _, fb, _, fbs, _ = consts
    seq, width = hdn.shape
    d = hy_w4.shape[1] // (2 * HY_ORDER)
    tl = LANE_TILE
    w4 = hy_w4.reshape(width, 2 * HY_ORDER, d).transpose(1, 0, 2)
    max_decay = math.log(HY_TARGET) / HY_FAST_DECAY
    min_decay = math.log(HY_TARGET) / HY_SLOW_DECAY
    deltas = jnp.abs(jnp.linspace(min_decay, max_decay, d, dtype=_f32)).reshape(1, d)
    n_rows = (FFT_H + 1) * 2 * FFT_N2
    const2 = lambda shape: _single(shape, lambda o, j: (0, 0))
    return pl.pallas_call(
        _filter_kernel,
        grid=(HY_ORDER, d // tl),
        in_specs=[const2(hdn.shape),
                  pl.BlockSpec((None, width, tl), lambda o, j: (2 * o, 0, j)),
                  pl.BlockSpec((None, width, tl), lambda o, j: (2 * o + 1, 0, j)),
                  pl.BlockSpec((1, tl), lambda o, j: (0, j)),
                  const2(fa_k.shape), _single(fb.shape, lambda o, j: (0, 0, 0)), const2(fbs.shape)],
        out_specs=pl.BlockSpec((None, n_rows, tl), lambda o, j: (o, 0, j)),
        out_shape=jax.ShapeDtypeStruct((HY_ORDER, n_rows, d), _f32),
        scratch_shapes=[pltpu.VMEM((FFT_H, FFT_N2, tl), _f32), pltpu.VMEM((FFT_N1, FFT_N2, tl), _f32)],
        compiler_params=_params(),
        name="filter_spectra",
    )(hdn, w4, w4, deltas, fa_k, fb, fbs)


def _merge_kernel(hx_ref, yr_ref, yh_ref, x_ref, wbg_ref, bbg_ref, wa_ref, wb_ref, wo_ref, g1_ref,
                  g_ref, sh_ref, sc_ref, x1_ref, h2_ref):
    d = x_ref.shape[-1]
    gates = _sigmoid(_dot(hx_ref[...], wbg_ref[...]) + bbg_ref[...])
    ya = _dot(yr_ref[...], wa_ref[...])
    yb = _dot(yh_ref[...].astype(_bf16), wb_ref[...])
    merged = gates[:, :d] * ya + gates[:, d:] * yb
    x1 = x_ref[...] + g1_ref[...] * _dot(merged.astype(_bf16), wo_ref[...])
    x1_ref[...] = x1
    y = x1 * lax.rsqrt(jnp.mean(x1 * x1, axis=-1, keepdims=True) + RMS_EPS) * g_ref[...]
    h2_ref[...] = (y * (1.0 + sc_ref[...]) + sh_ref[...]).astype(h2_ref.dtype)


def _merge(hx, y_rnn, y_hy, x, w_bg, b_bg, w_a, w_b, w_o, g1, norm2_g, sh2, sc2, tm=512):
    b, l, d = x.shape
    row = lambda width: pl.BlockSpec((None, tm, width), lambda i, j: (i, j, 0))
    vec = pl.BlockSpec((None, 1, d), lambda i, j: (i, 0, 0))
    full = lambda a: pl.BlockSpec(a.shape, lambda i, j: (0,) * a.ndim)
    b_bg = b_bg.reshape(1, -1)
    norm2_g = norm2_g.reshape(1, d)
    return pl.pallas_call(
        _merge_kernel,
        grid=(b, l // tm),
        in_specs=[row(d), row(d), row(d), row(d), full(w_bg), full(b_bg), full(w_a), full(w_b), full(w_o),
                  vec, full(norm2_g), vec, vec],
        out_specs=[row(d), row(d)],
        out_shape=[jax.ShapeDtypeStruct((b, l, d), _f32), jax.ShapeDtypeStruct((b, l, d), _bf16)],
        compiler_params=_params(),
        name="merge",
    )(hx, y_rnn, y_hy, x, w_bg, b_bg, w_a, w_b, w_o, g1, norm2_g, sh2, sc2)


def _ffn_kernel(h_ref, x1_ref, wg_ref, wu_ref, wo_ref, g2_ref, fg_ref, o_ref):
    h = h_ref[...]
    g = _dot(h, wg_ref[...])
    u = _dot(h, wu_ref[...])
    a = (g * _sigmoid(g) * u).astype(_bf16)
    x2 = x1_ref[...] + g2_ref[...] * _dot(a, wo_ref[...])
    y = x2 * lax.rsqrt(jnp.mean(x2 * x2, axis=-1, keepdims=True) + RMS_EPS)
    o_ref[...] = y * fg_ref[...]


def _ffn(h2, x1, w_g, w_u, w_o, g2, final_g, tm=512):
    b, l, d = x1.shape
    row = pl.BlockSpec((None, tm, d), lambda i, j: (i, j, 0))
    vec = pl.BlockSpec((None, 1, d), lambda i, j: (i, 0, 0))
    full = lambda a: _single(a.shape, lambda i, j: (0,) * a.ndim)
    final_g = final_g.reshape(1, d)
    return pl.pallas_call(
        _ffn_kernel,
        grid=(b, l // tm),
        in_specs=[row, row, full(w_g), full(w_u), full(w_o), vec, full(final_g)],
        out_specs=row,
        out_shape=jax.ShapeDtypeStruct((b, l, d), _f32),
        compiler_params=_params(),
        name="ffn",
    )(h2, x1, w_g, w_u, w_o, g2, final_g)


def kernel(x, c, ctx, c_ctx, w_mod, b_mod, norm1_g, norm2_g, w_in, b_in, rnn_conv_w, rnn_conv_b, rg_wa, rg_ba,
           rg_wx, rg_bx, rg_lambda, hy_conv_w, hy_conv_b, hy_w1, hy_b1, hy_w2, hy_b2, hy_w3, hy_b3, hy_freq,
           hy_w4, hy_skip, w_a_out, w_b_out, w_out, w_ffn_in, w_ffn_out, final_g):
    assert w_mod.shape[0] == 1, "single-layer block"
    b, seq, d = x.shape
    ctx_len = ctx.shape[1]
    assert seq == FFT_H * FFT_N2 and d == D_MODEL
    assert b == SUBLANES, "time-major rows put the batch on the sublanes of one register"
    (w_mod, b_mod, norm1_g, norm2_g, w_in, b_in, rnn_conv_w, rnn_conv_b, rg_wa, rg_ba, rg_wx, rg_bx,
     rg_lambda, hy_conv_w, hy_conv_b, hy_w1, hy_b1, hy_w2, hy_b2, hy_w3, hy_b3, hy_freq, hy_w4, hy_skip,
     w_a_out, w_b_out, w_out, w_ffn_in, w_ffn_out) = [
        a[0] for a in (w_mod, b_mod, norm1_g, norm2_g, w_in, b_in, rnn_conv_w, rnn_conv_b, rg_wa, rg_ba,
                       rg_wx, rg_bx, rg_lambda, hy_conv_w, hy_conv_b, hy_w1, hy_b1, hy_w2, hy_b2, hy_w3,
                       hy_b3, hy_freq, hy_w4, hy_skip, w_a_out, w_b_out, w_out, w_ffn_in, w_ffn_out)]

    pad_rows = 2 * SUBLANES - b - 1
    c_all = jnp.concatenate([c, c_ctx[None, :], jnp.zeros((pad_rows, d), _f32)], axis=0)
    mod = _mod_vectors(c_all, w_mod, b_mod)
    sh1, sc1, g1, sh2, sc2, g2 = [m.reshape(b, 1, d) for m in jnp.split(mod[:b], N_MOD, axis=-1)]
    csh1, csc1 = [jnp.broadcast_to(m.reshape(1, 1, d), (b, 1, d))
                  for m in jnp.split(mod[b], N_MOD, axis=-1)[:2]]

    w_in_bf = w_in.astype(_bf16)
    w_rx, b_rx = w_in_bf[:, :d], b_in[:d]
    w_rg, b_rg = w_in_bf[:, d:2 * d], b_in[d:2 * d]
    w_hy, b_hy = w_in_bf[:, 2 * d:5 * d], b_in[2 * d:5 * d]
    w_bg, b_bg = w_in_bf[:, 5 * d:], b_in[5 * d:]
    gate_w, gate_bias = zip(*[_gate_weights(rg_wa[i], rg_wx[i], rg_ba[i], rg_bx[i]) for i in range(2)])
    gate_w, gate_bias = jnp.stack(gate_w), jnp.stack(gate_bias)
    lam = rg_lambda.reshape(2, 1, d)
    zeros_state = jnp.zeros((b, d), _f32)

    _, hc_tb = _norm_mod(ctx, norm1_g, csh1, csc1)
    u_c = _proj(hc_tb, w_rx, b_rx, mode="conv", conv_w=rnn_conv_w, conv_b=rnn_conv_b,
                pad_left=RNN_CONV_PAD_LEFT, period=ctx_len, row_stride=b, tm=ctx_len * b, out_dtype=_bf16)
    _, cf, cb = _rglru(u_c, jnp.zeros_like(u_c), b, gate_w, gate_bias, lam, zeros_state, zeros_state)

    hx, hx_tb = _norm_mod(x, norm1_g, sh1, sc1)
    u = _proj(hx_tb, w_rx, b_rx, mode="conv", conv_w=rnn_conv_w, conv_b=rnn_conv_b,
              pad_left=RNN_CONV_PAD_LEFT, row_stride=b, out_dtype=_bf16)
    grg = _proj(hx_tb, w_rg, b_rg, mode="gelu", out_dtype=_bf16)
    q = _proj_conv_bt(hx_tb, b, w_hy, b_hy, hy_conv_w, hy_conv_b, HY_CONV_PAD_LEFT, GRID_W)
    y_rnn, _, _ = _rglru(u, grg, b, gate_w, gate_bias, lam, cf, cb)

    consts = [jnp.asarray(a).astype(_bf16) for a in _dft_constants()]
    hdn = _filter_features(seq, hy_w1, hy_b1, hy_w2, hy_b2, hy_w3, hy_b3, hy_freq)
    kf = _filter_spectra(hdn, hy_w4, consts)
    q4 = q.reshape(b, FFT_H, FFT_N2, 3 * d)
    n_tiles = d // LANE_TILE
    z1 = _hyena_order(q4, 0, q4, n_tiles, kf, 0, hy_skip[0:1], consts, _f32)
    y_hy = _hyena_order(z1, 0, q4, 2 * n_tiles, kf, 1, hy_skip[1:2], consts, _f32)
    y_hy = y_hy.reshape(b, seq, d)

    x1, h2 = _merge(hx, y_rnn, y_hy, x, w_bg, b_bg, w_a_out.astype(_bf16), w_b_out.astype(_bf16),
                    w_out.astype(_bf16), g1, norm2_g, sh2, sc2)
    w_ffn_bf = w_ffn_in.astype(_bf16)
    return _ffn(h2, x1, w_ffn_bf[:, :D_FF], w_ffn_bf[:, D_FF:], w_ffn_out.astype(_bf16), g2, final_g)
```

```python
import functools
import math

import numpy as np
import jax
import jax.numpy as jnp
from jax import lax
from jax.experimental import pallas as pl
from jax.experimental.pallas import tpu as pltpu

D_MODEL = 1024
GRID_W = 64
N_MOD = 6
RMS_EPS = 1e-6
RNN_HEADS = 16
RNN_HEAD_DIM = D_MODEL // RNN_HEADS
RNN_CONV_PAD_LEFT = 2
HY_CONV_PAD_LEFT = 1
RG_C = 8.0
HY_ORDER = 2
HY_EMB = 33
HY_BANDS = (HY_EMB - 1) // 2
HY_FAST_DECAY = 0.3
HY_SLOW_DECAY = 1.5
HY_TARGET = 1e-2
D_FF = ((8 * D_MODEL // 3 + 255) // 256) * 256

SUBLANES = 8
LANE_TILE = 256
VMEM_LIMIT = 56 * 1024 * 1024

FFT_N1 = 64
FFT_N2 = 128
FFT_N = FFT_N1 * FFT_N2
FFT_H = FFT_N1 // 2

_f32 = jnp.float32
_bf16 = jnp.bfloat16


def _params(**kw):
    return pltpu.CompilerParams(vmem_limit_bytes=VMEM_LIMIT, **kw)


def _dot(a, b):
    return jnp.dot(a, b, preferred_element_type=_f32)


def _sigmoid(x):
    return 0.5 * jnp.tanh(0.5 * x) + 0.5


def _mod_kernel(c_ref, w_ref, b_ref, o_ref):
    c = c_ref[...]
    s = c * _sigmoid(c)
    o_ref[...] = jnp.dot(s, w_ref[...], preferred_element_type=_f32,
                         precision=lax.Precision.HIGHEST) + b_ref[...]


def _mod_vectors(c_all, w_mod, b_mod):
    rows, d = c_all.shape
    n = w_mod.shape[1]
    tn = 1024
    return pl.pallas_call(
        _mod_kernel,
        grid=(n // tn,),
        in_specs=[pl.BlockSpec((rows, d), lambda j: (0, 0)),
                  pl.BlockSpec((d, tn), lambda j: (0, j)),
                  pl.BlockSpec((1, tn), lambda j: (0, j))],
        out_specs=pl.BlockSpec((rows, tn), lambda j: (0, j)),
        out_shape=jax.ShapeDtypeStruct((rows, n), _f32),
        compiler_params=_params(),
        name="mod_vectors",
    )(c_all, w_mod, b_mod.reshape(1, n))


LANES = 128
NORM_TT = 128


def _norm_mod_kernel(x_ref, g_ref, sh_ref, sc_ref, o_ref, otb_ref, tb_scr):
    nb, tt, d = x_ref.shape
    for b in range(nb):
        x = x_ref[b]
        gain = g_ref[...] * (1.0 + sc_ref[b])
        y = x * lax.rsqrt(jnp.mean(x * x, axis=-1, keepdims=True) + RMS_EPS) * gain + sh_ref[b]
        o_ref[b] = y.astype(o_ref.dtype)
        for s in range(d // LANES):
            tb_scr.at[s][pl.ds(b, tt, stride=nb), :] = y[:, s * LANES:(s + 1) * LANES]
    for s in range(d // LANES):
        otb_ref[:, s * LANES:(s + 1) * LANES] = tb_scr[s].astype(otb_ref.dtype)


def _norm_mod(x, g, sh, sc):
    b, l, d = x.shape
    tt = NORM_TT
    vec = pl.BlockSpec((b, 1, d), lambda i: (0, 0, 0))
    return pl.pallas_call(
        _norm_mod_kernel,
        grid=(l // tt,),
        in_specs=[pl.BlockSpec((b, tt, d), lambda i: (0, i, 0)),
                  pl.BlockSpec((1, d), lambda i: (0, 0)), vec, vec],
        out_specs=[pl.BlockSpec((b, tt, d), lambda i: (0, i, 0)),
                   pl.BlockSpec((tt * b, d), lambda i: (i, 0))],
        out_shape=[jax.ShapeDtypeStruct((b, l, d), _bf16), jax.ShapeDtypeStruct((l * b, d), _bf16)],
        scratch_shapes=[pltpu.VMEM((d // LANES, tt * b, LANES), _f32)],
        compiler_params=_params(),
        name="norm_mod",
    )(x, g.reshape(1, d), sh, sc)


def _conv_tap_weights(w, pad_left, rows, row_stride):
    if row_stride != 1:
        return [w[k:k + 1, :] for k in range(w.shape[0])]
    ridx = lax.broadcasted_iota(jnp.int32, (rows, w.shape[1]), 0)
    taps = []
    for k in range(w.shape[0]):
        off = k - pad_left
        valid = (ridx + off >= 0) & (ridx + off < rows)
        taps.append(jnp.where(valid, w[k:k + 1, :], 0.0))
    return taps


def _short_conv_period(y, taps, cb, pad_left, row_stride):
    rows = y.shape[0]
    n_taps = len(taps)
    if row_stride != 1:
        lanes = y.shape[1]
        ypad = jnp.concatenate([jnp.zeros((pad_left * row_stride, lanes), y.dtype), y,
                                jnp.zeros(((n_taps - 1 - pad_left) * row_stride, lanes), y.dtype)], axis=0)
        out = cb
        for k in range(n_taps):
            out = out + ypad[k * row_stride:k * row_stride + rows] * taps[k]
        return out
    out = y * taps[pad_left] + cb
    for k in range(n_taps):
        off = k - pad_left
        if off != 0:
            out = out + pltpu.roll(y, (-off) % rows, axis=0) * taps[k]
    return out


def _proj_kernel(x_ref, w_ref, b_ref, *rest, mode, pad_left, period, row_stride):
    o_ref = rest[-1]
    tm, tn = o_ref.shape
    if mode == "gelu":
        o_ref[...] = jax.nn.gelu(_dot(x_ref[...], w_ref[...]) + b_ref[...]).astype(o_ref.dtype)
        return
    for j in range(tn // LANE_TILE):
        cols = slice(j * LANE_TILE, (j + 1) * LANE_TILE)
        y = _dot(x_ref[...], w_ref[:, cols]) + b_ref[:, cols]
        if mode == "conv":
            prow = period * row_stride
            taps = _conv_tap_weights(rest[0][:, cols], pad_left, prow, row_stride)
            cb = rest[1][:, cols]
            for p in range(tm // prow):
                rows = slice(p * prow, (p + 1) * prow)
                o_ref[rows, cols] = _short_conv_period(y[rows], taps, cb, pad_left, row_stride).astype(o_ref.dtype)
        elif mode == "gelu":
            o_ref[:, cols] = jax.nn.gelu(y).astype(o_ref.dtype)
        else:
            o_ref[:, cols] = y.astype(o_ref.dtype)


def _proj(x2d, w, bias, *, mode, conv_w=None, conv_b=None, pad_left=0, period=GRID_W, row_stride=1,
          tm=1024, tn=1024, out_dtype=_f32):
    assert tm % (period * row_stride) == 0 or mode != "conv"
    m, k = x2d.shape
    n = w.shape[1]
    in_specs = [pl.BlockSpec((tm, k), lambda j, i: (i, 0)),
                pl.BlockSpec((k, tn), lambda j, i: (0, j)),
                pl.BlockSpec((1, tn), lambda j, i: (0, j))]
    args = [x2d, w, bias.reshape(1, n)]
    if mode == "conv":
        taps = conv_w.shape[0]
        in_specs += [pl.BlockSpec((taps, tn), lambda j, i: (0, j)),
                     pl.BlockSpec((1, tn), lambda j, i: (0, j))]
        args += [conv_w, conv_b.reshape(1, n)]
    return pl.pallas_call(
        functools.partial(_proj_kernel, mode=mode, pad_left=pad_left, period=period, row_stride=row_stride),
        grid=(n // tn, m // tm),
        in_specs=in_specs,
        out_specs=pl.BlockSpec((tm, tn), lambda j, i: (i, j)),
        out_shape=jax.ShapeDtypeStruct((m, n), out_dtype),
        compiler_params=_params(),
        name="proj_" + mode,
    )(*args)


def _proj_conv_bt_kernel(x_ref, w_ref, b_ref, cw_ref, cb_ref, o_ref, tb_scr, *, pad_left):
    nb, tt, tn = o_ref.shape
    for j in range(tn // LANE_TILE):
        cols = slice(j * LANE_TILE, (j + 1) * LANE_TILE)
        y = _dot(x_ref[...], w_ref[:, cols]) + b_ref[:, cols]
        taps = _conv_tap_weights(cw_ref[:, cols], pad_left, tt * nb, nb)
        y = _short_conv_period(y, taps, cb_ref[:, cols], pad_left, nb)
        for s in range(LANE_TILE // LANES):
            slab = j * (LANE_TILE // LANES) + s
            tb_scr[slab] = y[:, s * LANES:(s + 1) * LANES]
            for bi in range(nb):
                piece = tb_scr.at[slab][pl.ds(bi, tt, stride=nb), :]
                o_ref[bi, :, slab * LANES:(slab + 1) * LANES] = piece.astype(o_ref.dtype)


def _proj_conv_bt(x_tb, nb, w, bias, conv_w, conv_b, pad_left, period, out_dtype=_f32):
    rows, k = x_tb.shape
    seq = rows // nb
    n = w.shape[1]
    taps = conv_w.shape[0]
    full = lambda shape: pl.BlockSpec(shape, lambda i: (0, 0))
    return pl.pallas_call(
        functools.partial(_proj_conv_bt_kernel, pad_left=pad_left),
        grid=(seq // period,),
        in_specs=[pl.BlockSpec((period * nb, k), lambda i: (i, 0)),
                  full((k, n)), full((1, n)), full((taps, n)), full((1, n))],
        out_specs=pl.BlockSpec((nb, period, n), lambda i: (0, i, 0)),
        out_shape=jax.ShapeDtypeStruct((nb, seq, n), out_dtype),
        scratch_shapes=[pltpu.VMEM((n // LANES, period * nb, LANES), _f32)],
        compiler_params=_params(),
        name="proj_conv_bt",
    )(x_tb, w, bias.reshape(1, n), conv_w, conv_b.reshape(1, n))


RNN_TT = 512
RNN_SUB_TT = 64


def _rnn_coeffs(u, w_ref, bias_ref, lam_ref):
    tl = u.shape[-1]
    half_decay = (-0.5 * RG_C) * jax.nn.softplus(-lam_ref[...])
    g = _dot(u.astype(_bf16), w_ref[...]) + bias_ref[...]
    log_a = half_decay * jnp.tanh(g[:, :tl]) + half_decay
    i = 0.5 * jnp.tanh(g[:, tl:]) + 0.5
    a = jnp.exp(log_a)
    x = -jnp.tanh(log_a) * (a * a + 1.0)
    root = jnp.where(x > 0.0, x * lax.rsqrt(x), 0.0)
    return a, root * (i * u.astype(_f32))


def _rnn_scan(u_ref, w_ref, bias_ref, lam_ref, a_scr, b_scr, h_ref, h0, reverse):
    nb = h0.shape[0]
    rows = u_ref.shape[0]
    sub = RNN_SUB_TT * nb
    n_sub = rows // sub
    order = list(range(n_sub))[::-1] if reverse else list(range(n_sub))

    def coeffs(k):
        sl = slice(k * sub, (k + 1) * sub)
        a, b = _rnn_coeffs(u_ref[sl, :], w_ref, bias_ref, lam_ref)
        a_scr[sl, :] = a
        b_scr[sl, :] = b

    coeffs(order[0])
    h = h0
    for pos, k in enumerate(order):
        if pos + 1 < n_sub:
            coeffs(order[pos + 1])
        steps = list(range(k * sub, (k + 1) * sub, nb))
        for r0 in (steps[::-1] if reverse else steps):
            h = a_scr[r0:r0 + nb, :] * h + b_scr[r0:r0 + nb, :]
            h_ref[r0:r0 + nb, :] = h
    return h


def _rnn_fwd_kernel(u_ref, w_ref, bias_ref, lam_ref, h0_ref, hf_ref, hlast_ref, a_scr, b_scr, h_scr):
    @pl.when(pl.program_id(1) == 0)
    def _():
        h_scr[...] = h0_ref[...]

    h = _rnn_scan(u_ref, w_ref, bias_ref, lam_ref, a_scr, b_scr, hf_ref, h_scr[...], False)
    h_scr[...] = h
    hlast_ref[...] = h


def _rnn_bwd_kernel(u_ref, grg_ref, hf_ref, w_ref, bias_ref, lam_ref, h0_ref, y_ref, hfirst_ref,
                    a_scr, b_scr, h_scr, y_scr):
    @pl.when(pl.program_id(1) == 0)
    def _():
        h_scr[...] = h0_ref[...]

    nb, tt, tl = y_ref.shape
    h = _rnn_scan(u_ref, w_ref, bias_ref, lam_ref, a_scr, b_scr, b_scr, h_scr[...], True)
    h_scr[...] = h
    hfirst_ref[...] = h
    y = (hf_ref[...] + b_scr[...]) * grg_ref[...].astype(_f32)
    for s in range(tl // LANES):
        y_scr[s] = y[:, s * LANES:(s + 1) * LANES]
    for bi in range(nb):
        for s in range(tl // LANES):
            piece = y_scr.at[s][pl.ds(bi, tt, stride=nb), :]
            y_ref[bi, :, s * LANES:(s + 1) * LANES] = piece.astype(y_ref.dtype)


def _rglru(u_tb, grg_tb, nb, gate_w, gate_bias, lam, h0f, h0b):
    rows, d = u_tb.shape
    seq = rows // nb
    tl = LANE_TILE
    tt = min(RNN_TT, seq)
    n_chunks = seq // tt
    chunk = tt * nb
    state = pl.BlockSpec((nb, tl), lambda j, c: (0, j))
    scratch = [pltpu.VMEM((chunk, tl), _f32), pltpu.VMEM((chunk, tl), _f32), pltpu.VMEM((nb, tl), _f32)]

    def param_specs(direction):
        return [pl.BlockSpec((None, None, tl, 2 * tl), lambda j, c: (direction, j, 0, 0)),
                pl.BlockSpec((None, None, 1, 2 * tl), lambda j, c: (direction, j, 0, 0)),
                pl.BlockSpec((None, 1, tl), lambda j, c: (direction, 0, j))]

    fwd_rows = pl.BlockSpec((chunk, tl), lambda j, c: (c, j))
    hf_tb, hf_last = pl.pallas_call(
        _rnn_fwd_kernel,
        grid=(d // tl, n_chunks),
        in_specs=[fwd_rows] + param_specs(0) + [state],
        out_specs=[fwd_rows, state],
        out_shape=[jax.ShapeDtypeStruct((rows, d), _f32), jax.ShapeDtypeStruct((nb, d), _f32)],
        scratch_shapes=scratch,
        compiler_params=_params(),
        name="rglru_fwd",
    )(u_tb, gate_w, gate_bias, lam, h0f)

    bwd_rows = pl.BlockSpec((chunk, tl), lambda j, c: (n_chunks - 1 - c, j))
    y, hb_first = pl.pallas_call(
        _rnn_bwd_kernel,
        grid=(d // tl, n_chunks),
        in_specs=[bwd_rows, bwd_rows, bwd_rows] + param_specs(1) + [state],
        out_specs=[pl.BlockSpec((nb, tt, tl), lambda j, c: (0, n_chunks - 1 - c, j)), state],
        out_shape=[jax.ShapeDtypeStruct((nb, seq, d), _bf16), jax.ShapeDtypeStruct((nb, d), _f32)],
        scratch_shapes=scratch + [pltpu.VMEM((tl // LANES, chunk, LANES), _f32)],
        compiler_params=_params(),
        name="rglru_bwd",
    )(u_tb, grg_tb, hf_tb, gate_w, gate_bias, lam, h0b)
    return y, hf_last, hb_first


def _gate_weights(wa, wx, ba, bx):
    heads_per_tile = LANE_TILE // RNN_HEAD_DIM
    n_tiles = RNN_HEADS // heads_per_tile

    def tile_blockdiag(w):
        w = w.reshape(n_tiles, heads_per_tile, RNN_HEAD_DIM, RNN_HEAD_DIM)
        eye = jnp.eye(heads_per_tile, dtype=w.dtype)
        full = jnp.einsum('thij,hg->thigj', w, eye)
        return full.reshape(n_tiles, LANE_TILE, LANE_TILE)

    w = (0.5 * jnp.concatenate([tile_blockdiag(wa), tile_blockdiag(wx)], axis=-1)).astype(_bf16)
    bias = jnp.concatenate([ba.reshape(n_tiles, 1, LANE_TILE), bx.reshape(n_tiles, 1, LANE_TILE)], axis=-1)
    return w, (0.5 * bias).astype(_f32)


HY_UNROLL_A = 16
HY_UNROLL_B = 31
HY_EPILOGUE_SLABS = 4


def _dft_constants():
    n1 = np.arange(FFT_H)
    k1 = np.arange(FFT_H + 1)
    ang = 2.0 * np.pi * np.outer(k1, n1) / FFT_N1
    fa = np.concatenate([np.cos(ang), -np.sin(ang)[1:FFT_H]], axis=0)
    fa_k = np.kron(fa, np.eye(SUBLANES))
    weight = np.where((k1 == 0) | (k1 == FFT_H), 1.0, 2.0)[:, None] / FFT_N
    fai = np.concatenate([weight * np.cos(ang), (-2.0 / FFT_N) * np.sin(ang)[1:FFT_H]], axis=0).T
    fai_k = np.kron(fai, np.eye(SUBLANES))

    n2 = np.arange(FFT_N2)
    k2 = np.arange(FFT_N2)

    def cs(k1v):
        idx = (np.outer(k2, n2) * FFT_N1 + k1v * n2[None, :]) % FFT_N
        phi = 2.0 * np.pi * idx / FFT_N
        return np.cos(phi), np.sin(phi)

    fb, fbi = [], []
    for k1v in range(1, FFT_H):
        c, s = cs(k1v)
        fb.append(np.block([[c, s], [-s, c]]))
        fbi.append(np.block([[c.T, -s.T], [s.T, c.T]]))
    c0, s0 = cs(0)
    ch, sh = cs(FFT_H)
    z = np.zeros_like(c0)
    fbs = np.block([[c0, z], [-s0, z], [z, ch], [z, -sh]])
    fbsi = np.block([[c0.T, -s0.T, z, z], [z, z, ch.T, -sh.T]])
    as32 = lambda a: np.asarray(a, np.float32)
    return as32(fa_k), as32(fai_k), as32(np.stack(fb)), as32(np.stack(fbi)), as32(fbs), as32(fbsi)


def _hyena_kernel(z_ref, gate_ref, kf_ref, skip_ref, fa_ref, fai_ref, fb_ref, fbi_ref, fbs_ref, fbsi_ref,
                  o_ref, w_scr, conv_scr):
    tl = z_ref.shape[-1]
    half = FFT_N2
    pack = 2 * SUBLANES

    for p in range(FFT_N2 // pack):
        zz = z_ref[:, p * pack:(p + 1) * pack, :].astype(_f32)
        for hf in range(2):
            r0 = p * pack + hf * SUBLANES
            zin = zz[:, hf * SUBLANES:(hf + 1) * SUBLANES, :].reshape(FFT_H * SUBLANES, tl)
            y = _dot(fa_ref[...], zin.astype(_bf16))
            w_scr[:, r0:r0 + SUBLANES, :] = y.reshape(FFT_N1, SUBLANES, tl)

    yin = jnp.concatenate([w_scr[0], w_scr[FFT_H]], axis=0).astype(_bf16)
    x = _dot(fbs_ref[...], yin)
    pieces = []
    for q in range(2):
        xr = x[(2 * q) * half:(2 * q + 1) * half]
        xi = x[(2 * q + 1) * half:(2 * q + 2) * half]
        kr = kf_ref[pl.ds((2 * q) * half, half), :]
        ki = kf_ref[pl.ds((2 * q + 1) * half, half), :]
        pieces += [xr * kr - xi * ki, xr * ki + xi * kr]
    v = _dot(fbsi_ref[...], jnp.concatenate(pieces, axis=0).astype(_bf16))
    w_scr[0] = v[:half]
    w_scr[FFT_H] = v[half:]

    def stage_b(k1, carry):
        yin = jnp.concatenate([w_scr[k1], w_scr[FFT_H + k1]], axis=0).astype(_bf16)
        x = _dot(fb_ref[k1 - 1], yin)
        xr, xi = x[:half], x[half:]
        base = pl.multiple_of(2 * half * (k1 + 1), 2 * half)
        kr = kf_ref[pl.ds(base, half), :]
        ki = kf_ref[pl.ds(base + half, half), :]
        p = jnp.concatenate([xr * kr - xi * ki, xr * ki + xi * kr], axis=0).astype(_bf16)
        v = _dot(fbi_ref[k1 - 1], p)
        w_scr[k1] = v[:half]
        w_scr[FFT_H + k1] = v[half:]
        return carry

    lax.fori_loop(1, FFT_H, stage_b, 0, unroll=HY_UNROLL_B)

    for g in range(FFT_N2 // SUBLANES):
        r0 = g * SUBLANES
        vin = w_scr[:, r0:r0 + SUBLANES, :].reshape(FFT_N1 * SUBLANES, tl)
        conv_scr[:, r0:r0 + SUBLANES, :] = _dot(fai_ref[...], vin.astype(_bf16)).reshape(FFT_H, SUBLANES, tl)

    skip = skip_ref[...]
    for c in range(0, FFT_H, HY_EPILOGUE_SLABS):
        sl = slice(c, c + HY_EPILOGUE_SLABS)
        out = gate_ref[sl].astype(_f32) * (conv_scr[sl] + skip * z_ref[sl].astype(_f32))
        o_ref[sl] = out.astype(o_ref.dtype)


def _single(shape, index_map):
    return pl.BlockSpec(shape, index_map, pipeline_mode=pl.Buffered(1))


def _hyena_order(z_arr, z_col, gate_arr, gate_col, kf, order, skip, consts, out_dtype):
    fa_k, fai_k, fb, fbi, fbs, fbsi = consts
    b = z_arr.shape[0]
    tl = LANE_TILE
    d_out = kf.shape[-1]
    seq_block = (None, FFT_H, FFT_N2, tl)
    const2 = lambda shape: _single(shape, lambda j, i: (0, 0))
    const3 = lambda shape: _single(shape, lambda j, i: (0, 0, 0))
    return pl.pallas_call(
        _hyena_kernel,
        grid=(d_out // tl, b),
        in_specs=[pl.BlockSpec(seq_block, lambda j, i: (i, 0, 0, z_col + j)),
                  pl.BlockSpec(seq_block, lambda j, i: (i, 0, 0, gate_col + j)),
                  _single((None, kf.shape[1], tl), lambda j, i: (order, 0, j)),
                  pl.BlockSpec((1, tl), lambda j, i: (0, j)),
                  const2(fa_k.shape), const2(fai_k.shape), const3(fb.shape), const3(fbi.shape),
                  const2(fbs.shape), const2(fbsi.shape)],
        out_specs=pl.BlockSpec(seq_block, lambda j, i: (i, 0, 0, j)),
        out_shape=jax.ShapeDtypeStruct((b, FFT_H, FFT_N2, d_out), out_dtype),
        scratch_shapes=[pltpu.VMEM((FFT_N1, FFT_N2, tl), _f32), pltpu.VMEM((FFT_H, FFT_N2, tl), _f32)],
        compiler_params=_params(),
        name="hyena_conv",
    )(z_arr, gate_arr, kf, skip, fa_k, fai_k, fb, fbi, fbs, fbsi)


def _filter_kernel(hdn_ref, w4f_ref, w4b_ref, delta_ref, fa_ref, fb_ref, fbs_ref, o_ref, sig_scr, w_scr):
    seq = hdn_ref.shape[0]
    tl = o_ref.shape[-1]
    half = FFT_N2
    hdn = hdn_ref[...]
    hdn_hi = hdn.astype(_bf16)
    hdn_lo = (hdn - hdn_hi.astype(_f32)).astype(_bf16)

    def dot3(w):
        w_hi = w.astype(_bf16)
        w_lo = (w - w_hi.astype(_f32)).astype(_bf16)
        return _dot(hdn_hi, w_hi) + (_dot(hdn_hi, w_lo) + _dot(hdn_lo, w_hi))

    row = lax.broadcasted_iota(jnp.int32, (seq, tl), 0)
    decay = jnp.exp(-(row.astype(_f32) * (1.0 / (seq - 1))) * delta_ref[...])
    f = dot3(w4f_ref[...]) * decay
    g = dot3(w4b_ref[...]) * decay
    g = jnp.where(row == 0, 0.0, g)
    norm = jnp.sum(jnp.abs(f), axis=0, keepdims=True) + jnp.sum(jnp.abs(g), axis=0, keepdims=True)
    inv_norm = 1.0 / norm
    f = f * inv_norm
    g = g * inv_norm

    for part, sig in enumerate((f + g, f - g)):
        sig_scr[...] = sig.reshape(FFT_H, FFT_N2, tl)

        def stage_a(gi, carry):
            r0 = pl.multiple_of(gi * SUBLANES, SUBLANES)
            zin = sig_scr[:, pl.ds(r0, SUBLANES), :].reshape(FFT_H * SUBLANES, tl)
            y = _dot(fa_ref[...], zin.astype(_bf16))
            w_scr[:, pl.ds(r0, SUBLANES), :] = y.reshape(FFT_N1, SUBLANES, tl)
            return carry

        lax.fori_loop(0, FFT_N2 // SUBLANES, stage_a, 0, unroll=HY_UNROLL_A)

        yin = jnp.concatenate([w_scr[0], w_scr[FFT_H]], axis=0).astype(_bf16)
        o_ref[pl.ds(part * half, half), :] = _dot(fbs_ref[pl.ds(part * half, half), :], yin)
        o_ref[pl.ds((2 + part) * half, half), :] = _dot(fbs_ref[pl.ds((2 + part) * half, half), :], yin)

        def stage_b(k1, carry):
            yin = jnp.concatenate([w_scr[k1], w_scr[FFT_H + k1]], axis=0).astype(_bf16)
            base = pl.multiple_of(2 * half * (k1 + 1) + part * half, half)
            o_ref[pl.ds(base, half), :] = _dot(fb_ref[k1 - 1, pl.ds(part * half, half), :], yin)
            return carry

        lax.fori_loop(1, FFT_H, stage_b, 0, unroll=HY_UNROLL_B)


def _filter_features(seq, hy_w1, hy_b1, hy_w2, hy_b2, hy_w3, hy_b3, hy_freq):
    t = jnp.linspace(0.0, 1.0, seq, dtype=_f32)[:, None]
    w = 2.0 * math.pi * jnp.arange(seq, dtype=_f32)[:, None] / seq
    f = jnp.linspace(1e-4, HY_BANDS - 1, HY_BANDS, dtype=_f32)[None, :]
    z = jnp.concatenate([t, jnp.cos(f * w), -jnp.sin(f * w)], axis=-1)
    hi = lax.Precision.HIGHEST
    hdn = jnp.sin(hy_freq * (jnp.dot(z, hy_w1, precision=hi) + hy_b1))
    hdn = jnp.sin(hy_freq * (jnp.dot(hdn, hy_w2, precision=hi) + hy_b2))
    return jnp.sin(hy_freq * (jnp.dot(hdn, hy_w3, precision=hi) + hy_b3))


def _filter_spectra(hdn, hy_w4, consts):
    fa_k, _, fb, _, fbs, _ = consts
    seq, width = hdn.shape
    d = hy_w4.shape[1] // (2 * HY_ORDER)
    tl = LANE_TILE
    w4 = hy_w4.reshape(width, 2 * HY_ORDER, d).transpose(1, 0, 2)
    max_decay = math.log(HY_TARGET) / HY_FAST_DECAY
    min_decay = math.log(HY_TARGET) / HY_SLOW_DECAY
    deltas = jnp.abs(jnp.linspace(min_decay, max_decay, d, dtype=_f32)).reshape(1, d)
    n_rows = (FFT_H + 1) * 2 * FFT_N2
    const2 = lambda shape: _single(shape, lambda o, j: (0, 0))
    return pl.pallas_call(
        _filter_kernel,
        grid=(HY_ORDER, d // tl),
        in_specs=[const2(hdn.shape),
                  pl.BlockSpec((None, width, tl), lambda o, j: (2 * o, 0, j)),
                  pl.BlockSpec((None, width, tl), lambda o, j: (2 * o + 1, 0, j)),
                  pl.BlockSpec((1, tl), lambda o, j: (0, j)),
                  const2(fa_k.shape), _single(fb.shape, lambda o, j: (0, 0, 0)), const2(fbs.shape)],
        out_specs=pl.BlockSpec((None, n_rows, tl), lambda o, j: (o, 0, j)),
        out_shape=jax.ShapeDtypeStruct((HY_ORDER, n_rows, d), _f32),
        scratch_shapes=[pltpu.VMEM((FFT_H, FFT_N2, tl), _f32), pltpu.VMEM((FFT_N1, FFT_N2, tl), _f32)],
        compiler_params=_params(),
        name="filter_spectra",
    )(hdn, w4, w4, deltas, fa_k, fb, fbs)


def _merge_kernel(hx_ref, yr_ref, yh_ref, x_ref, wbg_ref, bbg_ref, wa_ref, wb_ref, wo_ref, g1_ref,
                  g_ref, sh_ref, sc_ref, x1_ref, h2_ref):
    d = x_ref.shape[-1]
    gates = _sigmoid(_dot(hx_ref[...], wbg_ref[...]) + bbg_ref[...])
    ya = _dot(yr_ref[...], wa_ref[...])
    yb = _dot(yh_ref[...].astype(_bf16), wb_ref[...])
    merged = gates[:, :d] * ya + gates[:, d:] * yb
    x1 = x_ref[...] + g1_ref[...] * _dot(merged.astype(_bf16), wo_ref[...])
    x1_ref[...] = x1
    y = x1 * lax.rsqrt(jnp.mean(x1 * x1, axis=-1, keepdims=True) + RMS_EPS) * g_ref[...]
    h2_ref[...] = (y * (1.0 + sc_ref[...]) + sh_ref[...]).astype(h2_ref.dtype)


def _merge(hx, y_rnn, y_hy, x, w_bg, b_bg, w_a, w_b, w_o, g1, norm2_g, sh2, sc2, tm=512):
    b, l, d = x.shape
    row = lambda width: pl.BlockSpec((None, tm, width), lambda i, j: (i, j, 0))
    vec = pl.BlockSpec((None, 1, d), lambda i, j: (i, 0, 0))
    full = lambda a: pl.BlockSpec(a.shape, lambda i, j: (0,) * a.ndim)
    b_bg = b_bg.reshape(1, -1)
    norm2_g = norm2_g.reshape(1, d)
    return pl.pallas_call(
        _merge_kernel,
        grid=(b, l // tm),
        in_specs=[row(d), row(d), row(d), row(d), full(w_bg), full(b_bg), full(w_a), full(w_b), full(w_o),
                  vec, full(norm2_g), vec, vec],
        out_specs=[row(d), row(d)],
        out_shape=[jax.ShapeDtypeStruct((b, l, d), _f32), jax.ShapeDtypeStruct((b, l, d), _bf16)],
        compiler_params=_params(),
        name="merge",
    )(hx, y_rnn, y_hy, x, w_bg, b_bg, w_a, w_b, w_o, g1, norm2_g, sh2, sc2)


def _ffn_kernel(h_ref, x1_ref, wg_ref, wu_ref, wo_ref, g2_ref, fg_ref, o_ref):
    h = h_ref[...]
    g = _dot(h, wg_ref[...])
    u = _dot(h, wu_ref[...])
    a = (g * _sigmoid(g) * u).astype(_bf16)
    x2 = x1_ref[...] + g2_ref[...] * _dot(a, wo_ref[...])
    y = x2 * lax.rsqrt(jnp.mean(x2 * x2, axis=-1, keepdims=True) + RMS_EPS)
    o_ref[...] = y * fg_ref[...]


def _ffn(h2, x1, w_g, w_u, w_o, g2, final_g, tm=512):
    b, l, d = x1.shape
    row = pl.BlockSpec((None, tm, d), lambda i, j: (i, j, 0))
    vec = pl.BlockSpec((None, 1, d), lambda i, j: (i, 0, 0))
    full = lambda a: _single(a.shape, lambda i, j: (0,) * a.ndim)
    final_g = final_g.reshape(1, d)
    return pl.pallas_call(
        _ffn_kernel,
        grid=(b, l // tm),
        in_specs=[row, row, full(w_g), full(w_u), full(w_o), vec, full(final_g)],
        out_specs=row,
        out_shape=jax.ShapeDtypeStruct((b, l, d), _f32),
        compiler_params=_params(),
        name="ffn",
    )(h2, x1, w_g, w_u, w_o, g2, final_g)


def kernel(x, c, ctx, c_ctx, w_mod, b_mod, norm1_g, norm2_g, w_in, b_in, rnn_conv_w, rnn_conv_b, rg_wa, rg_ba,
           rg_wx, rg_bx, rg_lambda, hy_conv_w, hy_conv_b, hy_w1, hy_b1, hy_w2, hy_b2, hy_w3, hy_b3, hy_freq,
           hy_w4, hy_skip, w_a_out, w_b_out, w_out, w_ffn_in, w_ffn_out, final_g):
    assert w_mod.shape[0] == 1, "single-layer block"
    b, seq, d = x.shape
    ctx_len = ctx.shape[1]
    assert seq == FFT_H * FFT_N2 and d == D_MODEL
    assert b == SUBLANES, "time-major rows put the batch on the sublanes of one register"
    (w_mod, b_mod, norm1_g, norm2_g, w_in, b_in, rnn_conv_w, rnn_conv_b, rg_wa, rg_ba, rg_wx, rg_bx,
     rg_lambda, hy_conv_w, hy_conv_b, hy_w1, hy_b1, hy_w2, hy_b2, hy_w3, hy_b3, hy_freq, hy_w4, hy_skip,
     w_a_out, w_b_out, w_out, w_ffn_in, w_ffn_out) = [
        a[0] for a in (w_mod, b_mod, norm1_g, norm2_g, w_in, b_in, rnn_conv_w, rnn_conv_b, rg_wa, rg_ba,
                       rg_wx, rg_bx, rg_lambda, hy_conv_w, hy_conv_b, hy_w1, hy_b1, hy_w2, hy_b2, hy_w3,
                       hy_b3, hy_freq, hy_w4, hy_skip, w_a_out, w_b_out, w_out, w_ffn_in, w_ffn_out)]

    pad_rows = 2 * SUBLANES - b - 1
    c_all = jnp.concatenate([c, c_ctx[None, :], jnp.zeros((pad_rows, d), _f32)], axis=0)
    mod = _mod_vectors(c_all, w_mod, b_mod)
    sh1, sc1, g1, sh2, sc2, g2 = [m.reshape(b, 1, d) for m in jnp.split(mod[:b], N_MOD, axis=-1)]
    csh1, csc1 = [jnp.broadcast_to(m.reshape(1, 1, d), (b, 1, d))
                  for m in jnp.split(mod[b], N_MOD, axis=-1)[:2]]

    w_in_bf = w_in.astype(_bf16)
    w_rx, b_rx = w_in_bf[:, :d], b_in[:d]
    w_rg, b_rg = w_in_bf[:, d:2 * d], b_in[d:2 * d]
    w_hy, b_hy = w_in_bf[:, 2 * d:5 * d], b_in[2 * d:5 * d]
    w_bg, b_bg = w_in_bf[:, 5 * d:], b_in[5 * d:]
    gate_w, gate_bias = zip(*[_gate_weights(rg_wa[i], rg_wx[i], rg_ba[i], rg_bx[i]) for i in range(2)])
    gate_w, gate_bias = jnp.stack(gate_w), jnp.stack(gate_bias)
    lam = rg_lambda.reshape(2, 1, d)
    zeros_state = jnp.zeros((b, d), _f32)

    _, hc_tb = _norm_mod(ctx, norm1_g, csh1, csc1)
    u_c = _proj(hc_tb, w_rx, b_rx, mode="conv", conv_w=rnn_conv_w, conv_b=rnn_conv_b,
                pad_left=RNN_CONV_PAD_LEFT, period=ctx_len, row_stride=b, tm=ctx_len * b, out_dtype=_bf16)
    _, cf, cb = _rglru(u_c, jnp.zeros_like(u_c), b, gate_w, gate_bias, lam, zeros_state, zeros_state)

    hx, hx_tb = _norm_mod(x, norm1_g, sh1, sc1)
    u = _proj(hx_tb, w_rx, b_rx, mode="conv", conv_w=rnn_conv_w, conv_b=rnn_conv_b,
              pad_left=RNN_CONV_PAD_LEFT, row_stride=b, out_dtype=_bf16)
    grg = _proj(hx_tb, w_rg, b_rg, mode="gelu", out_dtype=_bf16)
    q = _proj_conv_bt(hx_tb, b, w_hy, b_hy, hy_conv_w, hy_conv_b, HY_CONV_PAD_LEFT, GRID_W, out_dtype=_bf16)
    y_rnn, _, _ = _rglru(u, grg, b, gate_w, gate_bias, lam, cf, cb)

    consts = [jnp.asarray(a).astype(_bf16) for a in _dft_constants()]
    hdn = _filter_features(seq, hy_w1, hy_b1, hy_w2, hy_b2, hy_w3, hy_b3, hy_freq)
    kf = _filter_spectra(hdn, hy_w4, consts)
    q4 = q.reshape(b, FFT_H, FFT_N2, 3 * d)
    n_tiles = d // LANE_TILE
    z1 = _hyena_order(q4, 0, q4, n_tiles, kf, 0, hy_skip[0:1], consts, _bf16)
    y_hy = _hyena_order(z1, 0, q4, 2 * n_tiles, kf, 1, hy_skip[1:2], consts, _bf16)
    y_hy = y_hy.reshape(b, seq, d)

    x1, h2 = _merge(hx, y_rnn, y_hy, x, w_bg, b_bg, w_a_out.astype(_bf16), w_b_out.astype(_bf16),
                    w_out.astype(_bf16), g1, norm2_g, sh2, sc2)
    w_ffn_bf = w_ffn_in.astype(_bf16)
    return _ffn(h2, x1, w_ffn_bf[:, :D_FF], w_ffn_bf[:, D_FF:], w_ffn_out.astype(_bf16), g2, final_g)
```

```python
import functools
import math

import numpy as np
import jax
import jax.numpy as jnp
from jax import lax
from jax.experimental import pallas as pl
from jax.experimental.pallas import tpu as pltpu

D_MODEL = 1024
GRID_W = 64
N_MOD = 6
RMS_EPS = 1e-6
RNN_HEADS = 16
RNN_HEAD_DIM = D_MODEL // RNN_HEADS
RNN_CONV_PAD_LEFT = 2
HY_CONV_PAD_LEFT = 1
RG_C = 8.0
HY_ORDER = 2
HY_EMB = 33
HY_BANDS = (HY_EMB - 1) // 2
HY_FAST_DECAY = 0.3
HY_SLOW_DECAY = 1.5
HY_TARGET = 1e-2
D_FF = ((8 * D_MODEL // 3 + 255) // 256) * 256

SUBLANES = 8
LANE_TILE = 256
VMEM_LIMIT = 56 * 1024 * 1024

FFT_N1 = 64
FFT_N2 = 128
FFT_N = FFT_N1 * FFT_N2
FFT_H = FFT_N1 // 2

_f32 = jnp.float32
_bf16 = jnp.bfloat16


def _params(**kw):
    return pltpu.CompilerParams(vmem_limit_bytes=VMEM_LIMIT, **kw)


def _dot(a, b):
    return jnp.dot(a, b, preferred_element_type=_f32)


def _sigmoid(x):
    return 0.5 * jnp.tanh(0.5 * x) + 0.5


def _mod_kernel(c_ref, w_ref, b_ref, o_ref):
    c = c_ref[...]
    s = c * _sigmoid(c)
    o_ref[...] = jnp.dot(s, w_ref[...], preferred_element_type=_f32,
                         precision=lax.Precision.HIGHEST) + b_ref[...]


def _mod_vectors(c_all, w_mod, b_mod):
    rows, d = c_all.shape
    n = w_mod.shape[1]
    tn = 1024
    return pl.pallas_call(
        _mod_kernel,
        grid=(n // tn,),
        in_specs=[pl.BlockSpec((rows, d), lambda j: (0, 0)),
                  pl.BlockSpec((d, tn), lambda j: (0, j)),
                  pl.BlockSpec((1, tn), lambda j: (0, j))],
        out_specs=pl.BlockSpec((rows, tn), lambda j: (0, j)),
        out_shape=jax.ShapeDtypeStruct((rows, n), _f32),
        compiler_params=_params(),
        name="mod_vectors",
    )(c_all, w_mod, b_mod.reshape(1, n))


LANES = 128
NORM_TT = 128


def _norm_mod_kernel(x_ref, g_ref, sh_ref, sc_ref, o_ref, otb_ref, tb_scr):
    nb, tt, d = x_ref.shape
    for b in range(nb):
        x = x_ref[b]
        gain = g_ref[...] * (1.0 + sc_ref[b])
        y = x * lax.rsqrt(jnp.mean(x * x, axis=-1, keepdims=True) + RMS_EPS) * gain + sh_ref[b]
        o_ref[b] = y.astype(o_ref.dtype)
        for s in range(d // LANES):
            tb_scr.at[s][pl.ds(b, tt, stride=nb), :] = y[:, s * LANES:(s + 1) * LANES]
    for s in range(d // LANES):
        otb_ref[:, s * LANES:(s + 1) * LANES] = tb_scr[s].astype(otb_ref.dtype)


def _norm_mod(x, g, sh, sc):
    b, l, d = x.shape
    tt = NORM_TT
    vec = pl.BlockSpec((b, 1, d), lambda i: (0, 0, 0))
    return pl.pallas_call(
        _norm_mod_kernel,
        grid=(l // tt,),
        in_specs=[pl.BlockSpec((b, tt, d), lambda i: (0, i, 0)),
                  pl.BlockSpec((1, d), lambda i: (0, 0)), vec, vec],
        out_specs=[pl.BlockSpec((b, tt, d), lambda i: (0, i, 0)),
                   pl.BlockSpec((tt * b, d), lambda i: (i, 0))],
        out_shape=[jax.ShapeDtypeStruct((b, l, d), _bf16), jax.ShapeDtypeStruct((l * b, d), _bf16)],
        scratch_shapes=[pltpu.VMEM((d // LANES, tt * b, LANES), _f32)],
        compiler_params=_params(),
        name="norm_mod",
    )(x, g.reshape(1, d), sh, sc)


def _conv_tap_weights(w, pad_left, rows, row_stride):
    if row_stride != 1:
        return [w[k:k + 1, :] for k in range(w.shape[0])]
    ridx = lax.broadcasted_iota(jnp.int32, (rows, w.shape[1]), 0)
    taps = []
    for k in range(w.shape[0]):
        off = k - pad_left
        valid = (ridx + off >= 0) & (ridx + off < rows)
        taps.append(jnp.where(valid, w[k:k + 1, :], 0.0))
    return taps


def _short_conv_period(y, taps, cb, pad_left, row_stride):
    rows = y.shape[0]
    n_taps = len(taps)
    if row_stride != 1:
        lanes = y.shape[1]
        ypad = jnp.concatenate([jnp.zeros((pad_left * row_stride, lanes), y.dtype), y,
                                jnp.zeros(((n_taps - 1 - pad_left) * row_stride, lanes), y.dtype)], axis=0)
        out = cb
        for k in range(n_taps):
            out = out + ypad[k * row_stride:k * row_stride + rows] * taps[k]
        return out
    out = y * taps[pad_left] + cb
    for k in range(n_taps):
        off = k - pad_left
        if off != 0:
            out = out + pltpu.roll(y, (-off) % rows, axis=0) * taps[k]
    return out


def _proj_conv_kernel(x_ref, w_ref, b_ref, cw_ref, cb_ref, o_ref, *, pad_left, period, row_stride):
    tm, tn = o_ref.shape
    prow = period * row_stride
    for j in range(tn // LANE_TILE):
        cols = slice(j * LANE_TILE, (j + 1) * LANE_TILE)
        y = _dot(x_ref[...], w_ref[:, cols]) + b_ref[:, cols]
        taps = _conv_tap_weights(cw_ref[:, cols], pad_left, prow, row_stride)
        for p in range(tm // prow):
            rows = slice(p * prow, (p + 1) * prow)
            out = _short_conv_period(y[rows], taps, cb_ref[:, cols], pad_left, row_stride)
            o_ref[rows, cols] = out.astype(o_ref.dtype)


def _proj_conv(x2d, w, bias, conv_w, conv_b, *, pad_left, period, row_stride, tm, tn=1024, out_dtype=_f32):
    assert tm % (period * row_stride) == 0
    m, k = x2d.shape
    n = w.shape[1]
    col = lambda rows: pl.BlockSpec((rows, tn), lambda j, i: (0, j))
    return pl.pallas_call(
        functools.partial(_proj_conv_kernel, pad_left=pad_left, period=period, row_stride=row_stride),
        grid=(n // tn, m // tm),
        in_specs=[pl.BlockSpec((tm, k), lambda j, i: (i, 0)), col(k), col(1), col(conv_w.shape[0]), col(1)],
        out_specs=pl.BlockSpec((tm, tn), lambda j, i: (i, j)),
        out_shape=jax.ShapeDtypeStruct((m, n), out_dtype),
        compiler_params=_params(),
        name="proj_conv",
    )(x2d, w, bias.reshape(1, n), conv_w, conv_b.reshape(1, n))


def _inproj_kernel(x_ref, g_ref, sh_ref, sc_ref, w_ref, b_ref, rcw_ref, rcb_ref, hcw_ref, hcb_ref,
                   hx_ref, u_ref, grg_ref, q_ref, xtb_scr, lhs_scr, qtb_scr, *, rnn_pad, hy_pad):
    nb, tt, d = x_ref.shape
    rows = tt * nb
    for b in range(nb):
        x = x_ref[b]
        gain = g_ref[...] * (1.0 + sc_ref[b])
        y = x * lax.rsqrt(jnp.mean(x * x, axis=-1, keepdims=True) + RMS_EPS) * gain + sh_ref[b]
        hx_ref[b] = y.astype(hx_ref.dtype)
        for s in range(d // LANES):
            xtb_scr.at[s][pl.ds(b, tt, stride=nb), :] = y[:, s * LANES:(s + 1) * LANES]
    for s in range(d // LANES):
        lhs_scr[:, s * LANES:(s + 1) * LANES] = xtb_scr[s].astype(lhs_scr.dtype)

    n_rx = u_ref.shape[1] // LANE_TILE
    n_rg = grg_ref.shape[1] // LANE_TILE
    n_hy = q_ref.shape[2] // LANE_TILE
    slabs = LANE_TILE // LANES
    for j in range(n_rx + n_rg + n_hy):
        cols = slice(j * LANE_TILE, (j + 1) * LANE_TILE)
        y = _dot(lhs_scr[...], w_ref[:, cols]) + b_ref[:, cols]
        if j < n_rx:
            taps = _conv_tap_weights(rcw_ref[:, cols], rnn_pad, rows, nb)
            u_ref[:, cols] = _short_conv_period(y, taps, rcb_ref[:, cols], rnn_pad, nb).astype(u_ref.dtype)
        elif j < n_rx + n_rg:
            oc = slice((j - n_rx) * LANE_TILE, (j - n_rx + 1) * LANE_TILE)
            grg_ref[:, oc] = jax.nn.gelu(y).astype(grg_ref.dtype)
        else:
            jh = j - n_rx - n_rg
            oc = slice(jh * LANE_TILE, (jh + 1) * LANE_TILE)
            taps = _conv_tap_weights(hcw_ref[:, oc], hy_pad, rows, nb)
            y = _short_conv_period(y, taps, hcb_ref[:, oc], hy_pad, nb)
            for s in range(slabs):
                slab = jh * slabs + s
                qtb_scr[slab] = y[:, s * LANES:(s + 1) * LANES]
                for bi in range(nb):
                    piece = qtb_scr.at[slab][pl.ds(bi, tt, stride=nb), :]
                    q_ref[bi, :, slab * LANES:(slab + 1) * LANES] = piece.astype(q_ref.dtype)


def _inproj(x, g, sh, sc, w_in_bf, b_in, rnn_conv_w, rnn_conv_b, hy_conv_w, hy_conv_b, period):
    b, l, d = x.shape
    n_all = w_in_bf.shape[1]
    n_hy = hy_conv_w.shape[1]
    tt = period
    vec = pl.BlockSpec((b, 1, d), lambda i: (0, 0, 0))
    full = lambda a: _single(a.shape, lambda i: (0,) * a.ndim)
    g2, b2 = g.reshape(1, d), b_in.reshape(1, n_all)
    rcb, hcb = rnn_conv_b.reshape(1, d), hy_conv_b.reshape(1, n_hy)
    tb_rows = pl.BlockSpec((tt * b, d), lambda i: (i, 0))
    return pl.pallas_call(
        functools.partial(_inproj_kernel, rnn_pad=RNN_CONV_PAD_LEFT, hy_pad=HY_CONV_PAD_LEFT),
        grid=(l // tt,),
        in_specs=[pl.BlockSpec((b, tt, d), lambda i: (0, i, 0)), full(g2), vec, vec,
                  full(w_in_bf), full(b2), full(rnn_conv_w), full(rcb), full(hy_conv_w), full(hcb)],
        out_specs=[pl.BlockSpec((b, tt, d), lambda i: (0, i, 0)), tb_rows, tb_rows,
                   pl.BlockSpec((b, tt, n_hy), lambda i: (0, i, 0))],
        out_shape=[jax.ShapeDtypeStruct((b, l, d), _bf16), jax.ShapeDtypeStruct((l * b, d), _bf16),
                   jax.ShapeDtypeStruct((l * b, d), _bf16), jax.ShapeDtypeStruct((b, l, n_hy), _bf16)],
        scratch_shapes=[pltpu.VMEM((d // LANES, tt * b, LANES), _f32),
                        pltpu.VMEM((tt * b, d), _bf16),
                        pltpu.VMEM((n_hy // LANES, tt * b, LANES), _f32)],
        compiler_params=_params(),
        name="inproj",
    )(x, g2, sh, sc, w_in_bf, b2, rnn_conv_w, rcb, hy_conv_w, hcb)


RNN_TT = 512
RNN_SUB_TT = 64


def _rnn_coeffs(u, w_ref, bias_ref, lam_ref):
    tl = u.shape[-1]
    half_decay = (-0.5 * RG_C) * jax.nn.softplus(-lam_ref[...])
    g = _dot(u.astype(_bf16), w_ref[...]) + bias_ref[...]
    log_a = half_decay * jnp.tanh(g[:, :tl]) + half_decay
    i = 0.5 * jnp.tanh(g[:, tl:]) + 0.5
    a = jnp.exp(log_a)
    x = -jnp.tanh(log_a) * (a * a + 1.0)
    root = jnp.where(x > 0.0, x * lax.rsqrt(x), 0.0)
    return a, root * (i * u.astype(_f32))


def _rnn_scan(u_ref, w_ref, bias_ref, lam_ref, a_scr, b_scr, h_ref, h0, reverse):
    nb = h0.shape[0]
    rows = u_ref.shape[0]
    sub = RNN_SUB_TT * nb
    n_sub = rows // sub
    order = list(range(n_sub))[::-1] if reverse else list(range(n_sub))

    def coeffs(k):
        sl = slice(k * sub, (k + 1) * sub)
        a, b = _rnn_coeffs(u_ref[sl, :], w_ref, bias_ref, lam_ref)
        a_scr[sl, :] = a
        b_scr[sl, :] = b

    coeffs(order[0])
    h = h0
    for pos, k in enumerate(order):
        if pos + 1 < n_sub:
            coeffs(order[pos + 1])
        steps = list(range(k * sub, (k + 1) * sub, nb))
        for r0 in (steps[::-1] if reverse else steps):
            h = a_scr[r0:r0 + nb, :] * h + b_scr[r0:r0 + nb, :]
            h_ref[r0:r0 + nb, :] = h
    return h


def _rnn_fwd_kernel(u_ref, w_ref, bias_ref, lam_ref, h0_ref, hf_ref, hlast_ref, a_scr, b_scr, h_scr):
    @pl.when(pl.program_id(1) == 0)
    def _():
        h_scr[...] = h0_ref[...]

    h = _rnn_scan(u_ref, w_ref, bias_ref, lam_ref, a_scr, b_scr, hf_ref, h_scr[...], False)
    h_scr[...] = h
    hlast_ref[...] = h


def _rnn_bwd_kernel(u_ref, grg_ref, hf_ref, w_ref, bias_ref, lam_ref, h0_ref, y_ref, hfirst_ref,
                    a_scr, b_scr, h_scr, y_scr):
    @pl.when(pl.program_id(1) == 0)
    def _():
        h_scr[...] = h0_ref[...]

    nb, tt, tl = y_ref.shape
    h = _rnn_scan(u_ref, w_ref, bias_ref, lam_ref, a_scr, b_scr, b_scr, h_scr[...], True)
    h_scr[...] = h
    hfirst_ref[...] = h
    y = (hf_ref[...] + b_scr[...]) * grg_ref[...].astype(_f32)
    for s in range(tl // LANES):
        y_scr[s] = y[:, s * LANES:(s + 1) * LANES]
    for bi in range(nb):
        for s in range(tl // LANES):
            piece = y_scr.at[s][pl.ds(bi, tt, stride=nb), :]
            y_ref[bi, :, s * LANES:(s + 1) * LANES] = piece.astype(y_ref.dtype)


def _rglru(u_tb, grg_tb, nb, gate_w, gate_bias, lam, h0f, h0b):
    rows, d = u_tb.shape
    seq = rows // nb
    tl = LANE_TILE
    tt = min(RNN_TT, seq)
    n_chunks = seq // tt
    chunk = tt * nb
    state = pl.BlockSpec((nb, tl), lambda j, c: (0, j))
    scratch = [pltpu.VMEM((chunk, tl), _f32), pltpu.VMEM((chunk, tl), _f32), pltpu.VMEM((nb, tl), _f32)]

    def param_specs(direction):
        return [pl.BlockSpec((None, None, tl, 2 * tl), lambda j, c: (direction, j, 0, 0)),
                pl.BlockSpec((None, None, 1, 2 * tl), lambda j, c: (direction, j, 0, 0)),
                pl.BlockSpec((None, 1, tl), lambda j, c: (direction, 0, j))]

    fwd_rows = pl.BlockSpec((chunk, tl), lambda j, c: (c, j))
    hf_tb, hf_last = pl.pallas_call(
        _rnn_fwd_kernel,
        grid=(d // tl, n_chunks),
        in_specs=[fwd_rows] + param_specs(0) + [state],
        out_specs=[fwd_rows, state],
        out_shape=[jax.ShapeDtypeStruct((rows, d), _f32), jax.ShapeDtypeStruct((nb, d), _f32)],
        scratch_shapes=scratch,
        compiler_params=_params(),
        name="rglru_fwd",
    )(u_tb, gate_w, gate_bias, lam, h0f)

    bwd_rows = pl.BlockSpec((chunk, tl), lambda j, c: (n_chunks - 1 - c, j))
    y, hb_first = pl.pallas_call(
        _rnn_bwd_kernel,
        grid=(d // tl, n_chunks),
        in_specs=[bwd_rows, bwd_rows, bwd_rows] + param_specs(1) + [state],
        out_specs=[pl.BlockSpec((nb, tt, tl), lambda j, c: (0, n_chunks - 1 - c, j)), state],
        out_shape=[jax.ShapeDtypeStruct((nb, seq, d), _bf16), jax.ShapeDtypeStruct((nb, d), _f32)],
        scratch_shapes=scratch + [pltpu.VMEM((tl // LANES, chunk, LANES), _f32)],
        compiler_params=_params(),
        name="rglru_bwd",
    )(u_tb, grg_tb, hf_tb, gate_w, gate_bias, lam, h0b)
    return y, hf_last, hb_first


def _gate_weights(wa, wx, ba, bx):
    heads_per_tile = LANE_TILE // RNN_HEAD_DIM
    n_tiles = RNN_HEADS // heads_per_tile

    def tile_blockdiag(w):
        w = w.reshape(n_tiles, heads_per_tile, RNN_HEAD_DIM, RNN_HEAD_DIM)
        eye = jnp.eye(heads_per_tile, dtype=w.dtype)
        full = jnp.einsum('thij,hg->thigj', w, eye)
        return full.reshape(n_tiles, LANE_TILE, LANE_TILE)

    w = (0.5 * jnp.concatenate([tile_blockdiag(wa), tile_blockdiag(wx)], axis=-1)).astype(_bf16)
    bias = jnp.concatenate([ba.reshape(n_tiles, 1, LANE_TILE), bx.reshape(n_tiles, 1, LANE_TILE)], axis=-1)
    return w, (0.5 * bias).astype(_f32)


HY_UNROLL_A = 16
HY_UNROLL_B = 31
HY_EPILOGUE_SLABS = 4


def _dft_constants():
    n1 = np.arange(FFT_H)
    k1 = np.arange(FFT_H + 1)
    ang = 2.0 * np.pi * np.outer(k1, n1) / FFT_N1
    fa = np.concatenate([np.cos(ang), -np.sin(ang)[1:FFT_H]], axis=0)
    fa_k = np.kron(fa, np.eye(SUBLANES))
    weight = np.where((k1 == 0) | (k1 == FFT_H), 1.0, 2.0)[:, None] / FFT_N
    fai = np.concatenate([weight * np.cos(ang), (-2.0 / FFT_N) * np.sin(ang)[1:FFT_H]], axis=0).T
    fai_k = np.kron(fai, np.eye(SUBLANES))

    n2 = np.arange(FFT_N2)
    k2 = np.arange(FFT_N2)

    def cs(k1v):
        idx = (np.outer(k2, n2) * FFT_N1 + k1v * n2[None, :]) % FFT_N
        phi = 2.0 * np.pi * idx / FFT_N
        return np.cos(phi), np.sin(phi)

    fb, fbi = [], []
    for k1v in range(1, FFT_H):
        c, s = cs(k1v)
        fb.append(np.block([[c, s], [-s, c]]))
        fbi.append(np.block([[c.T, -s.T], [s.T, c.T]]))
    c0, s0 = cs(0)
    ch, sh = cs(FFT_H)
    z = np.zeros_like(c0)
    fbs = np.block([[c0, z], [-s0, z], [z, ch], [z, -sh]])
    fbsi = np.block([[c0.T, -s0.T, z, z], [z, z, ch.T, -sh.T]])
    as32 = lambda a: np.asarray(a, np.float32)
    return as32(fa_k), as32(fai_k), as32(np.stack(fb)), as32(np.stack(fbi)), as32(fbs), as32(fbsi)


def _hyena_kernel(z_ref, gate_ref, kf_ref, skip_ref, fa_ref, fai_ref, fb_ref, fbi_ref, fbs_ref, fbsi_ref,
                  o_ref, w_scr, conv_scr):
    tl = z_ref.shape[-1]
    half = FFT_N2
    pack = 2 * SUBLANES

    for p in range(FFT_N2 // pack):
        zz = z_ref[:, p * pack:(p + 1) * pack, :].astype(_f32)
        for hf in range(2):
            r0 = p * pack + hf * SUBLANES
            zin = zz[:, hf * SUBLANES:(hf + 1) * SUBLANES, :].reshape(FFT_H * SUBLANES, tl)
            y = _dot(fa_ref[...], zin.astype(_bf16))
            w_scr[:, r0:r0 + SUBLANES, :] = y.reshape(FFT_N1, SUBLANES, tl)

    yin = jnp.concatenate([w_scr[0], w_scr[FFT_H]], axis=0).astype(_bf16)
    x = _dot(fbs_ref[...], yin)
    pieces = []
    for q in range(2):
        xr = x[(2 * q) * half:(2 * q + 1) * half]
        xi = x[(2 * q + 1) * half:(2 * q + 2) * half]
        kr = kf_ref[pl.ds((2 * q) * half, half), :]
        ki = kf_ref[pl.ds((2 * q + 1) * half, half), :]
        pieces += [xr * kr - xi * ki, xr * ki + xi * kr]
    v = _dot(fbsi_ref[...], jnp.concatenate(pieces, axis=0).astype(_bf16))
    w_scr[0] = v[:half]
    w_scr[FFT_H] = v[half:]

    def stage_b(k1, carry):
        yin = jnp.concatenate([w_scr[k1], w_scr[FFT_H + k1]], axis=0).astype(_bf16)
        x = _dot(fb_ref[k1 - 1], yin)
        xr, xi = x[:half], x[half:]
        base = pl.multiple_of(2 * half * (k1 + 1), 2 * half)
        kr = kf_ref[pl.ds(base, half), :]
        ki = kf_ref[pl.ds(base + half, half), :]
        p = jnp.concatenate([xr * kr - xi * ki, xr * ki + xi * kr], axis=0).astype(_bf16)
        v = _dot(fbi_ref[k1 - 1], p)
        w_scr[k1] = v[:half]
        w_scr[FFT_H + k1] = v[half:]
        return carry

    lax.fori_loop(1, FFT_H, stage_b, 0, unroll=HY_UNROLL_B)

    for g in range(FFT_N2 // SUBLANES):
        r0 = g * SUBLANES
        vin = w_scr[:, r0:r0 + SUBLANES, :].reshape(FFT_N1 * SUBLANES, tl)
        conv_scr[:, r0:r0 + SUBLANES, :] = _dot(fai_ref[...], vin.astype(_bf16)).reshape(FFT_H, SUBLANES, tl)

    skip = skip_ref[...]
    for c in range(0, FFT_H, HY_EPILOGUE_SLABS):
        sl = slice(c, c + HY_EPILOGUE_SLABS)
        out = gate_ref[sl].astype(_f32) * (conv_scr[sl] + skip * z_ref[sl].astype(_f32))
        o_ref[sl] = out.astype(o_ref.dtype)


def _single(shape, index_map):
    return pl.BlockSpec(shape, index_map, pipeline_mode=pl.Buffered(1))


def _hyena_order(z_arr, z_col, gate_arr, gate_col, kf, order, skip, consts, out_dtype):
    fa_k, fai_k, fb, fbi, fbs, fbsi = consts
    b = z_arr.shape[0]
    tl = LANE_TILE
    d_out = kf.shape[-1]
    seq_block = (None, FFT_H, FFT_N2, tl)
    const2 = lambda shape: _single(shape, lambda j, i: (0, 0))
    const3 = lambda shape: _single(shape, lambda j, i: (0, 0, 0))
    return pl.pallas_call(
        _hyena_kernel,
        grid=(d_out // tl, b),
        in_specs=[pl.BlockSpec(seq_block, lambda j, i: (i, 0, 0, z_col + j)),
                  pl.BlockSpec(seq_block, lambda j, i: (i, 0, 0, gate_col + j)),
                  _single((None, kf.shape[1], tl), lambda j, i: (order, 0, j)),
                  pl.BlockSpec((1, tl), lambda j, i: (0, j)),
                  const2(fa_k.shape), const2(fai_k.shape), const3(fb.shape), const3(fbi.shape),
                  const2(fbs.shape), const2(fbsi.shape)],
        out_specs=pl.BlockSpec(seq_block, lambda j, i: (i, 0, 0, j)),
        out_shape=jax.ShapeDtypeStruct((b, FFT_H, FFT_N2, d_out), out_dtype),
        scratch_shapes=[pltpu.VMEM((FFT_N1, FFT_N2, tl), _f32), pltpu.VMEM((FFT_H, FFT_N2, tl), _f32)],
        compiler_params=_params(),
        name="hyena_conv",
    )(z_arr, gate_arr, kf, skip, fa_k, fai_k, fb, fbi, fbs, fbsi)


def _filter_kernel(hdn_ref, w4f_ref, w4b_ref, delta_ref, fa_ref, fb_ref, fbs_ref, o_ref, sig_scr, w_scr):
    seq = hdn_ref.shape[0]
    tl = o_ref.shape[-1]
    half = FFT_N2
    hdn = hdn_ref[...]
    hdn_hi = hdn.astype(_bf16)
    hdn_lo = (hdn - hdn_hi.astype(_f32)).astype(_bf16)

    def dot3(w):
        w_hi = w.astype(_bf16)
        w_lo = (w - w_hi.astype(_f32)).astype(_bf16)
        return _dot(hdn_hi, w_hi) + (_dot(hdn_hi, w_lo) + _dot(hdn_lo, w_hi))

    row = lax.broadcasted_iota(jnp.int32, (seq, tl), 0)
    decay = jnp.exp(-(row.astype(_f32) * (1.0 / (seq - 1))) * delta_ref[...])
    f = dot3(w4f_ref[...]) * decay
    g = dot3(w4b_ref[...]) * decay
    g = jnp.where(row == 0, 0.0, g)
    norm = jnp.sum(jnp.abs(f), axis=0, keepdims=True) + jnp.sum(jnp.abs(g), axis=0, keepdims=True)
    inv_norm = 1.0 / norm
    f = f * inv_norm
    g = g * inv_norm

    for part, sig in enumerate((f + g, f - g)):
        sig_scr[...] = sig.reshape(FFT_H, FFT_N2, tl)

        def stage_a(gi, carry):
            r0 = pl.multiple_of(gi * SUBLANES, SUBLANES)
            zin = sig_scr[:, pl.ds(r0, SUBLANES), :].reshape(FFT_H * SUBLANES, tl)
            y = _dot(fa_ref[...], zin.astype(_bf16))
            w_scr[:, pl.ds(r0, SUBLANES), :] = y.reshape(FFT_N1, SUBLANES, tl)
            return carry

        lax.fori_loop(0, FFT_N2 // SUBLANES, stage_a, 0, unroll=HY_UNROLL_A)

        yin = jnp.concatenate([w_scr[0], w_scr[FFT_H]], axis=0).astype(_bf16)
        o_ref[pl.ds(part * half, half), :] = _dot(fbs_ref[pl.ds(part * half, half), :], yin)
        o_ref[pl.ds((2 + part) * half, half), :] = _dot(fbs_ref[pl.ds((2 + part) * half, half), :], yin)

        def stage_b(k1, carry):
            yin = jnp.concatenate([w_scr[k1], w_scr[FFT_H + k1]], axis=0).astype(_bf16)
            base = pl.multiple_of(2 * half * (k1 + 1) + part * half, half)
            o_ref[pl.ds(base, half), :] = _dot(fb_ref[k1 - 1, pl.ds(part * half, half), :], yin)
            return carry

        lax.fori_loop(1, FFT_H, stage_b, 0, unroll=HY_UNROLL_B)


def _filter_features(seq, hy_w1, hy_b1, hy_w2, hy_b2, hy_w3, hy_b3, hy_freq):
    t = jnp.linspace(0.0, 1.0, seq, dtype=_f32)[:, None]
    w = 2.0 * math.pi * jnp.arange(seq, dtype=_f32)[:, None] / seq
    f = jnp.linspace(1e-4, HY_BANDS - 1, HY_BANDS, dtype=_f32)[None, :]
    z = jnp.concatenate([t, jnp.cos(f * w), -jnp.sin(f * w)], axis=-1)
    hi = lax.Precision.HIGHEST
    hdn = jnp.sin(hy_freq * (jnp.dot(z, hy_w1, precision=hi) + hy_b1))
    hdn = jnp.sin(hy_freq * (jnp.dot(hdn, hy_w2, precision=hi) + hy_b2))
    return jnp.sin(hy_freq * (jnp.dot(hdn, hy_w3, precision=hi) + hy_b3))


def _filter_spectra(hdn, hy_w4, consts):
    fa_k, _, fb, _, fbs, _ = consts
    seq, width = hdn.shape
    d = hy_w4.shape[1] // (2 * HY_ORDER)
    tl = LANE_TILE
    w4 = hy_w4.reshape(width, 2 * HY_ORDER, d).transpose(1, 0, 2)
    max_decay = math.log(HY_TARGET) / HY_FAST_DECAY
    min_decay = math.log(HY_TARGET) / HY_SLOW_DECAY
    deltas = jnp.abs(jnp.linspace(min_decay, max_decay, d, dtype=_f32)).reshape(1, d)
    n_rows = (FFT_H + 1) * 2 * FFT_N2
    const2 = lambda shape: _single(shape, lambda o, j: (0, 0))
    return pl.pallas_call(
        _filter_kernel,
        grid=(HY_ORDER, d // tl),
        in_specs=[const2(hdn.shape),
                  pl.BlockSpec((None, width, tl), lambda o, j: (2 * o, 0, j)),
                  pl.BlockSpec((None, width, tl), lambda o, j: (2 * o + 1, 0, j)),
                  pl.BlockSpec((1, tl), lambda o, j: (0, j)),
                  const2(fa_k.shape), _single(fb.shape, lambda o, j: (0, 0, 0)), const2(fbs.shape)],
        out_specs=pl.BlockSpec((None, n_rows, tl), lambda o, j: (o, 0, j)),
        out_shape=jax.ShapeDtypeStruct((HY_ORDER, n_rows, d), _f32),
        scratch_shapes=[pltpu.VMEM((FFT_H, FFT_N2, tl), _f32), pltpu.VMEM((FFT_N1, FFT_N2, tl), _f32)],
        compiler_params=_params(),
        name="filter_spectra",
    )(hdn, w4, w4, deltas, fa_k, fb, fbs)


def _merge_kernel(hx_ref, yr_ref, yh_ref, x_ref, wbg_ref, bbg_ref, wa_ref, wb_ref, wo_ref, g1_ref,
                  g_ref, sh_ref, sc_ref, x1_ref, h2_ref):
    d = x_ref.shape[-1]
    gates = _sigmoid(_dot(hx_ref[...], wbg_ref[...]) + bbg_ref[...])
    ya = _dot(yr_ref[...], wa_ref[...])
    yb = _dot(yh_ref[...].astype(_bf16), wb_ref[...])
    merged = gates[:, :d] * ya + gates[:, d:] * yb
    x1 = x_ref[...] + g1_ref[...] * _dot(merged.astype(_bf16), wo_ref[...])
    x1_ref[...] = x1
    y = x1 * lax.rsqrt(jnp.mean(x1 * x1, axis=-1, keepdims=True) + RMS_EPS) * g_ref[...]
    h2_ref[...] = (y * (1.0 + sc_ref[...]) + sh_ref[...]).astype(h2_ref.dtype)


def _merge(hx, y_rnn, y_hy, x, w_bg, b_bg, w_a, w_b, w_o, g1, norm2_g, sh2, sc2, tm=512):
    b, l, d = x.shape
    row = lambda width: pl.BlockSpec((None, tm, width), lambda i, j: (i, j, 0))
    vec = pl.BlockSpec((None, 1, d), lambda i, j: (i, 0, 0))
    full = lambda a: pl.BlockSpec(a.shape, lambda i, j: (0,) * a.ndim)
    b_bg = b_bg.reshape(1, -1)
    norm2_g = norm2_g.reshape(1, d)
    return pl.pallas_call(
        _merge_kernel,
        grid=(b, l // tm),
        in_specs=[row(d), row(d), row(d), row(d), full(w_bg), full(b_bg), full(w_a), full(w_b), full(w_o),
                  vec, full(norm2_g), vec, vec],
        out_specs=[row(d), row(d)],
        out_shape=[jax.ShapeDtypeStruct((b, l, d), _f32), jax.ShapeDtypeStruct((b, l, d), _bf16)],
        compiler_params=_params(),
        name="merge",
    )(hx, y_rnn, y_hy, x, w_bg, b_bg, w_a, w_b, w_o, g1, norm2_g, sh2, sc2)


def _ffn_kernel(h_ref, x1_ref, wg_ref, wu_ref, wo_ref, g2_ref, fg_ref, o_ref):
    h = h_ref[...]
    g = _dot(h, wg_ref[...])
    u = _dot(h, wu_ref[...])
    a = (g * _sigmoid(g) * u).astype(_bf16)
    x2 = x1_ref[...] + g2_ref[...] * _dot(a, wo_ref[...])
    y = x2 * lax.rsqrt(jnp.mean(x2 * x2, axis=-1, keepdims=True) + RMS_EPS)
    o_ref[...] = y * fg_ref[...]


def _ffn(h2, x1, w_gu, w_o, g2, final_g, tm=512):
    b, l, d = x1.shape
    d_ff = w_o.shape[0]
    row = pl.BlockSpec((None, tm, d), lambda i, j: (i, j, 0))
    vec = pl.BlockSpec((None, 1, d), lambda i, j: (i, 0, 0))
    full = lambda a: _single(a.shape, lambda i, j: (0,) * a.ndim)
    final_g = final_g.reshape(1, d)
    return pl.pallas_call(
        _ffn_kernel,
        grid=(b, l // tm),
        in_specs=[row, row, _single((d, d_ff), lambda i, j: (0, 0)), _single((d, d_ff), lambda i, j: (0, 1)),
                  full(w_o), vec, full(final_g)],
        out_specs=row,
        out_shape=jax.ShapeDtypeStruct((b, l, d), _f32),
        compiler_params=_params(),
        name="ffn",
    )(h2, x1, w_gu, w_gu, w_o, g2, final_g)


def kernel(x, c, ctx, c_ctx, w_mod, b_mod, norm1_g, norm2_g, w_in, b_in, rnn_conv_w, rnn_conv_b, rg_wa, rg_ba,
           rg_wx, rg_bx, rg_lambda, hy_conv_w, hy_conv_b, hy_w1, hy_b1, hy_w2, hy_b2, hy_w3, hy_b3, hy_freq,
           hy_w4, hy_skip, w_a_out, w_b_out, w_out, w_ffn_in, w_ffn_out, final_g):
    assert w_mod.shape[0] == 1, "single-layer block"
    b, seq, d = x.shape
    ctx_len = ctx.shape[1]
    assert seq == FFT_H * FFT_N2 and d == D_MODEL
    assert b == SUBLANES, "time-major rows put the batch on the sublanes of one register"
    (w_mod, b_mod, norm1_g, norm2_g, w_in, b_in, rnn_conv_w, rnn_conv_b, rg_wa, rg_ba, rg_wx, rg_bx,
     rg_lambda, hy_conv_w, hy_conv_b, hy_w1, hy_b1, hy_w2, hy_b2, hy_w3, hy_b3, hy_freq, hy_w4, hy_skip,
     w_a_out, w_b_out, w_out, w_ffn_in, w_ffn_out) = [
        a[0] for a in (w_mod, b_mod, norm1_g, norm2_g, w_in, b_in, rnn_conv_w, rnn_conv_b, rg_wa, rg_ba,
                       rg_wx, rg_bx, rg_lambda, hy_conv_w, hy_conv_b, hy_w1, hy_b1, hy_w2, hy_b2, hy_w3,
                       hy_b3, hy_freq, hy_w4, hy_skip, w_a_out, w_b_out, w_out, w_ffn_in, w_ffn_out)]

    pad_rows = 2 * SUBLANES - b - 1
    c_all = jnp.concatenate([c, c_ctx[None, :], jnp.zeros((pad_rows, d), _f32)], axis=0)
    mod = _mod_vectors(c_all, w_mod, b_mod)
    sh1, sc1, g1, sh2, sc2, g2 = [m.reshape(b, 1, d) for m in jnp.split(mod[:b], N_MOD, axis=-1)]
    csh1, csc1 = [jnp.broadcast_to(m.reshape(1, 1, d), (b, 1, d))
                  for m in jnp.split(mod[b], N_MOD, axis=-1)[:2]]

    w_in_bf = w_in.astype(_bf16)
    w_rx, b_rx = w_in_bf[:, :d], b_in[:d]
    w_bg, b_bg = w_in_bf[:, 5 * d:], b_in[5 * d:]
    gate_w, gate_bias = zip(*[_gate_weights(rg_wa[i], rg_wx[i], rg_ba[i], rg_bx[i]) for i in range(2)])
    gate_w, gate_bias = jnp.stack(gate_w), jnp.stack(gate_bias)
    lam = rg_lambda.reshape(2, 1, d)
    zeros_state = jnp.zeros((b, d), _f32)

    _, hc_tb = _norm_mod(ctx, norm1_g, csh1, csc1)
    u_c = _proj_conv(hc_tb, w_rx, b_rx, rnn_conv_w, rnn_conv_b, pad_left=RNN_CONV_PAD_LEFT, period=ctx_len,
                     row_stride=b, tm=ctx_len * b, out_dtype=_bf16)
    _, cf, cb = _rglru(u_c, jnp.zeros_like(u_c), b, gate_w, gate_bias, lam, zeros_state, zeros_state)

    hx, u, grg, q = _inproj(x, norm1_g, sh1, sc1, w_in_bf, b_in, rnn_conv_w, rnn_conv_b,
                            hy_conv_w, hy_conv_b, GRID_W)
    y_rnn, _, _ = _rglru(u, grg, b, gate_w, gate_bias, lam, cf, cb)

    consts = [jnp.asarray(a).astype(_bf16) for a in _dft_constants()]
    hdn = _filter_features(seq, hy_w1, hy_b1, hy_w2, hy_b2, hy_w3, hy_b3, hy_freq)
    kf = _filter_spectra(hdn, hy_w4, consts)
    q4 = q.reshape(b, FFT_H, FFT_N2, 3 * d)
    n_tiles = d // LANE_TILE
    z1 = _hyena_order(q4, 0, q4, n_tiles, kf, 0, hy_skip[0:1], consts, _bf16)
    y_hy = _hyena_order(z1, 0, q4, 2 * n_tiles, kf, 1, hy_skip[1:2], consts, _bf16)
    y_hy = y_hy.reshape(b, seq, d)

    x1, h2 = _merge(hx, y_rnn, y_hy, x, w_bg, b_bg, w_a_out.astype(_bf16), w_b_out.astype(_bf16),
                    w_out.astype(_bf16), g1, norm2_g, sh2, sc2)
    w_ffn_bf = w_ffn_in.astype(_bf16)
    return _ffn(h2, x1, w_ffn_bf, w_ffn_out.astype(_bf16), g2, final_g)
```

```python
import functools
import math

import numpy as np
import jax
import jax.numpy as jnp
from jax import lax
from jax.experimental import pallas as pl
from jax.experimental.pallas import tpu as pltpu

D_MODEL = 1024
GRID_W = 64
N_MOD = 6
RMS_EPS = 1e-6
RNN_HEADS = 16
RNN_HEAD_DIM = D_MODEL // RNN_HEADS
RNN_CONV_PAD_LEFT = 2
HY_CONV_PAD_LEFT = 1
RG_C = 8.0
HY_ORDER = 2
HY_EMB = 33
HY_BANDS = (HY_EMB - 1) // 2
HY_FAST_DECAY = 0.3
HY_SLOW_DECAY = 1.5
HY_TARGET = 1e-2
D_FF = ((8 * D_MODEL // 3 + 255) // 256) * 256

SUBLANES = 8
LANE_TILE = 256
VMEM_LIMIT = 56 * 1024 * 1024

FFT_N1 = 64
FFT_N2 = 128
FFT_N = FFT_N1 * FFT_N2
FFT_H = FFT_N1 // 2

_f32 = jnp.float32
_bf16 = jnp.bfloat16


def _params(**kw):
    return pltpu.CompilerParams(vmem_limit_bytes=VMEM_LIMIT, **kw)


def _dot(a, b):
    return jnp.dot(a, b, preferred_element_type=_f32)


def _sigmoid(x):
    return 0.5 * jnp.tanh(0.5 * x) + 0.5


def _mod_kernel(c_ref, w_ref, b_ref, o_ref):
    c = c_ref[...]
    s = c * _sigmoid(c)
    s_hi = s.astype(_bf16)
    s_lo = (s - s_hi.astype(_f32)).astype(_bf16)
    w = w_ref[...]
    w_hi = w.astype(_bf16)
    w_lo = (w - w_hi.astype(_f32)).astype(_bf16)
    o_ref[...] = _dot(jnp.concatenate([s_hi, s_lo, s_hi], axis=1),
                      jnp.concatenate([w_hi, w_hi, w_lo], axis=0)) + b_ref[...]


def _mod_vectors(c_all, w_mod, b_mod):
    rows, d = c_all.shape
    n = w_mod.shape[1]
    tn = 1024
    return pl.pallas_call(
        _mod_kernel,
        grid=(n // tn,),
        in_specs=[pl.BlockSpec((rows, d), lambda j: (0, 0)),
                  pl.BlockSpec((d, tn), lambda j: (0, j)),
                  pl.BlockSpec((1, tn), lambda j: (0, j))],
        out_specs=pl.BlockSpec((rows, tn), lambda j: (0, j)),
        out_shape=jax.ShapeDtypeStruct((rows, n), _f32),
        compiler_params=_params(),
        name="mod_vectors",
    )(c_all, w_mod, b_mod.reshape(1, n))


LANES = 128
NORM_TT = 128


def _norm_mod_kernel(x_ref, g_ref, sh_ref, sc_ref, o_ref, otb_ref, tb_scr):
    nb, tt, d = x_ref.shape
    for b in range(nb):
        x = x_ref[b]
        gain = g_ref[...] * (1.0 + sc_ref[b])
        y = x * lax.rsqrt(jnp.mean(x * x, axis=-1, keepdims=True) + RMS_EPS) * gain + sh_ref[b]
        o_ref[b] = y.astype(o_ref.dtype)
        for s in range(d // LANES):
            tb_scr.at[s][pl.ds(b, tt, stride=nb), :] = y[:, s * LANES:(s + 1) * LANES]
    for s in range(d // LANES):
        otb_ref[:, s * LANES:(s + 1) * LANES] = tb_scr[s].astype(otb_ref.dtype)


def _norm_mod(x, g, sh, sc):
    b, l, d = x.shape
    tt = NORM_TT
    vec = pl.BlockSpec((b, 1, d), lambda i: (0, 0, 0))
    return pl.pallas_call(
        _norm_mod_kernel,
        grid=(l // tt,),
        in_specs=[pl.BlockSpec((b, tt, d), lambda i: (0, i, 0)),
                  pl.BlockSpec((1, d), lambda i: (0, 0)), vec, vec],
        out_specs=[pl.BlockSpec((b, tt, d), lambda i: (0, i, 0)),
                   pl.BlockSpec((tt * b, d), lambda i: (i, 0))],
        out_shape=[jax.ShapeDtypeStruct((b, l, d), _bf16), jax.ShapeDtypeStruct((l * b, d), _bf16)],
        scratch_shapes=[pltpu.VMEM((d // LANES, tt * b, LANES), _f32)],
        compiler_params=_params(),
        name="norm_mod",
    )(x, g.reshape(1, d), sh, sc)


CONV_CHUNK_STEPS = 8


def _short_conv_period(y, w, cb, pad_left, nb, store):
    rows, lanes = y.shape
    n_taps = w.shape[0]
    ypad = jnp.concatenate([jnp.zeros((pad_left * nb, lanes), y.dtype), y,
                            jnp.zeros(((n_taps - 1 - pad_left) * nb, lanes), y.dtype)], axis=0)
    chunk = CONV_CHUNK_STEPS * nb
    for r0 in range(0, rows, chunk):
        out = cb
        for k in range(n_taps):
            out = out + ypad[r0 + k * nb:r0 + k * nb + chunk] * w[k:k + 1, :]
        store(r0, r0 + chunk, out)


def _proj_conv_kernel(x_ref, w_ref, b_ref, cw_ref, cb_ref, o_ref, *, pad_left, period, row_stride):
    tm, tn = o_ref.shape
    prow = period * row_stride
    for j in range(tn // LANE_TILE):
        cols = slice(j * LANE_TILE, (j + 1) * LANE_TILE)
        y = _dot(x_ref[...], w_ref[:, cols]) + b_ref[:, cols]
        for p in range(tm // prow):
            def store(r0, r1, val, base=p * prow, cols=cols):
                o_ref[base + r0:base + r1, cols] = val.astype(o_ref.dtype)

            _short_conv_period(y[p * prow:(p + 1) * prow], cw_ref[:, cols], cb_ref[:, cols], pad_left,
                               row_stride, store)


def _proj_conv(x2d, w, bias, conv_w, conv_b, *, pad_left, period, row_stride, tm, tn=1024, out_dtype=_f32):
    assert tm % (period * row_stride) == 0
    m, k = x2d.shape
    n = w.shape[1]
    col = lambda rows: pl.BlockSpec((rows, tn), lambda j, i: (0, j))
    return pl.pallas_call(
        functools.partial(_proj_conv_kernel, pad_left=pad_left, period=period, row_stride=row_stride),
        grid=(n // tn, m // tm),
        in_specs=[pl.BlockSpec((tm, k), lambda j, i: (i, 0)), col(k), col(1), col(conv_w.shape[0]), col(1)],
        out_specs=pl.BlockSpec((tm, tn), lambda j, i: (i, j)),
        out_shape=jax.ShapeDtypeStruct((m, n), out_dtype),
        compiler_params=_params(),
        name="proj_conv",
    )(x2d, w, bias.reshape(1, n), conv_w, conv_b.reshape(1, n))


GELU_CHUNK_ROWS = 64


def _inproj_kernel(x_ref, g_ref, sh_ref, sc_ref, w_ref, b_ref, rcw_ref, rcb_ref, hcw_ref, hcb_ref,
                   hx_ref, u_ref, grg_ref, q_ref, xtb_scr, lhs_scr, qtb_scr, *, rnn_pad, hy_pad):
    nb, tt, d = x_ref.shape
    rows = tt * nb
    for b in range(nb):
        x = x_ref[b]
        gain = g_ref[...] * (1.0 + sc_ref[b])
        y = x * lax.rsqrt(jnp.mean(x * x, axis=-1, keepdims=True) + RMS_EPS) * gain + sh_ref[b]
        hx_ref[b] = y.astype(hx_ref.dtype)
        for s in range(d // LANES):
            xtb_scr.at[s][pl.ds(b, tt, stride=nb), :] = y[:, s * LANES:(s + 1) * LANES]
    for s in range(d // LANES):
        lhs_scr[:, s * LANES:(s + 1) * LANES] = xtb_scr[s].astype(lhs_scr.dtype)

    n_rx = u_ref.shape[1] // LANE_TILE
    n_rg = grg_ref.shape[1] // LANE_TILE
    n_hy = q_ref.shape[2] // LANE_TILE
    slabs = LANE_TILE // LANES
    for j in range(n_rx + n_rg + n_hy):
        cols = slice(j * LANE_TILE, (j + 1) * LANE_TILE)
        y = _dot(lhs_scr[...], w_ref[:, cols]) + b_ref[:, cols]
        if j < n_rx:
            def store_u(r0, r1, val, cols=cols):
                u_ref[r0:r1, cols] = val.astype(u_ref.dtype)

            _short_conv_period(y, rcw_ref[:, cols], rcb_ref[:, cols], rnn_pad, nb, store_u)
        elif j < n_rx + n_rg:
            oc0 = (j - n_rx) * LANE_TILE
            for r0 in range(0, rows, GELU_CHUNK_ROWS):
                grg_ref[r0:r0 + GELU_CHUNK_ROWS, oc0:oc0 + LANE_TILE] = (
                    jax.nn.gelu(y[r0:r0 + GELU_CHUNK_ROWS]).astype(grg_ref.dtype))
        else:
            jh = j - n_rx - n_rg
            oc = slice(jh * LANE_TILE, (jh + 1) * LANE_TILE)

            def store_q(r0, r1, val, jh=jh):
                for s in range(slabs):
                    qtb_scr[jh * slabs + s, r0:r1, :] = val[:, s * LANES:(s + 1) * LANES]

            _short_conv_period(y, hcw_ref[:, oc], hcb_ref[:, oc], hy_pad, nb, store_q)
            for s in range(slabs):
                slab = jh * slabs + s
                for bi in range(nb):
                    piece = qtb_scr.at[slab][pl.ds(bi, tt, stride=nb), :]
                    q_ref[bi, :, slab * LANES:(slab + 1) * LANES] = piece.astype(q_ref.dtype)


def _inproj(x, g, sh, sc, w_in_bf, b_in, rnn_conv_w, rnn_conv_b, hy_conv_w, hy_conv_b, period):
    b, l, d = x.shape
    n_all = w_in_bf.shape[1]
    n_hy = hy_conv_w.shape[1]
    tt = period
    vec = pl.BlockSpec((b, 1, d), lambda i: (0, 0, 0))
    full = lambda a: _single(a.shape, lambda i: (0,) * a.ndim)
    g2, b2 = g.reshape(1, d), b_in.reshape(1, n_all)
    rcb, hcb = rnn_conv_b.reshape(1, d), hy_conv_b.reshape(1, n_hy)
    tb_rows = pl.BlockSpec((tt * b, d), lambda i: (i, 0))
    return pl.pallas_call(
        functools.partial(_inproj_kernel, rnn_pad=RNN_CONV_PAD_LEFT, hy_pad=HY_CONV_PAD_LEFT),
        grid=(l // tt,),
        in_specs=[pl.BlockSpec((b, tt, d), lambda i: (0, i, 0)), full(g2), vec, vec,
                  full(w_in_bf), full(b2), full(rnn_conv_w), full(rcb), full(hy_conv_w), full(hcb)],
        out_specs=[pl.BlockSpec((b, tt, d), lambda i: (0, i, 0)), tb_rows, tb_rows,
                   pl.BlockSpec((b, tt, n_hy), lambda i: (0, i, 0))],
        out_shape=[jax.ShapeDtypeStruct((b, l, d), _bf16), jax.ShapeDtypeStruct((l * b, d), _bf16),
                   jax.ShapeDtypeStruct((l * b, d), _bf16), jax.ShapeDtypeStruct((b, l, n_hy), _bf16)],
        scratch_shapes=[pltpu.VMEM((d // LANES, tt * b, LANES), _f32),
                        pltpu.VMEM((tt * b, d), _bf16),
                        pltpu.VMEM((n_hy // LANES, tt * b, LANES), _f32)],
        compiler_params=_params(),
        name="inproj",
    )(x, g2, sh, sc, w_in_bf, b2, rnn_conv_w, rcb, hy_conv_w, hcb)


RNN_TT = 512
RNN_SUB_TT = 64


def _rnn_coeffs(u, w_ref, bias_ref, lam_ref):
    tl = u.shape[-1]
    half_decay = (-0.5 * RG_C) * jax.nn.softplus(-lam_ref[...])
    g = _dot(u.astype(_bf16), w_ref[...]) + bias_ref[...]
    log_a = half_decay * jnp.tanh(g[:, :tl]) + half_decay
    i = 0.5 * jnp.tanh(g[:, tl:]) + 0.5
    a = jnp.exp(log_a)
    x = -jnp.tanh(log_a) * (a * a + 1.0)
    root = jnp.where(x > 0.0, x * lax.rsqrt(x), 0.0)
    return a, root * (i * u.astype(_f32))


def _rnn_scan(u_ref, w_ref, bias_ref, lam_ref, a_scr, b_scr, h_ref, h0, reverse):
    nb = h0.shape[0]
    rows = u_ref.shape[0]
    sub = RNN_SUB_TT * nb
    n_sub = rows // sub
    order = list(range(n_sub))[::-1] if reverse else list(range(n_sub))

    def coeffs(k):
        sl = slice(k * sub, (k + 1) * sub)
        a, b = _rnn_coeffs(u_ref[sl, :], w_ref, bias_ref, lam_ref)
        a_scr[sl, :] = a
        b_scr[sl, :] = b

    coeffs(order[0])
    h = h0
    for pos, k in enumerate(order):
        if pos + 1 < n_sub:
            coeffs(order[pos + 1])
        steps = list(range(k * sub, (k + 1) * sub, nb))
        for r0 in (steps[::-1] if reverse else steps):
            h = a_scr[r0:r0 + nb, :] * h + b_scr[r0:r0 + nb, :]
            h_ref[r0:r0 + nb, :] = h
    return h


def _rnn_fwd_kernel(u_ref, w_ref, bias_ref, lam_ref, h0_ref, hf_ref, hlast_ref, a_scr, b_scr, h_scr):
    @pl.when(pl.program_id(1) == 0)
    def _():
        h_scr[...] = h0_ref[...]

    h = _rnn_scan(u_ref, w_ref, bias_ref, lam_ref, a_scr, b_scr, hf_ref, h_scr[...], False)
    h_scr[...] = h
    hlast_ref[...] = h


def _rnn_bwd_kernel(u_ref, grg_ref, hf_ref, w_ref, bias_ref, lam_ref, h0_ref, y_ref, hfirst_ref,
                    a_scr, b_scr, h_scr, y_scr):
    @pl.when(pl.program_id(1) == 0)
    def _():
        h_scr[...] = h0_ref[...]

    nb, tt, tl = y_ref.shape
    h = _rnn_scan(u_ref, w_ref, bias_ref, lam_ref, a_scr, b_scr, b_scr, h_scr[...], True)
    h_scr[...] = h
    hfirst_ref[...] = h
    y = (hf_ref[...] + b_scr[...]) * grg_ref[...].astype(_f32)
    for s in range(tl // LANES):
        y_scr[s] = y[:, s * LANES:(s + 1) * LANES]
    for bi in range(nb):
        for s in range(tl // LANES):
            piece = y_scr.at[s][pl.ds(bi, tt, stride=nb), :]
            y_ref[bi, :, s * LANES:(s + 1) * LANES] = piece.astype(y_ref.dtype)


def _rglru(u_tb, grg_tb, nb, gate_w, gate_bias, lam, h0f, h0b):
    rows, d = u_tb.shape
    seq = rows // nb
    tl = LANE_TILE
    tt = min(RNN_TT, seq)
    n_chunks = seq // tt
    chunk = tt * nb
    state = pl.BlockSpec((nb, tl), lambda j, c: (0, j))
    scratch = [pltpu.VMEM((chunk, tl), _f32), pltpu.VMEM((chunk, tl), _f32), pltpu.VMEM((nb, tl), _f32)]

    def param_specs(direction):
        return [pl.BlockSpec((None, None, tl, 2 * tl), lambda j, c: (direction, j, 0, 0)),
                pl.BlockSpec((None, None, 1, 2 * tl), lambda j, c: (direction, j, 0, 0)),
                pl.BlockSpec((None, 1, tl), lambda j, c: (direction, 0, j))]

    fwd_rows = pl.BlockSpec((chunk, tl), lambda j, c: (c, j))
    hf_tb, hf_last = pl.pallas_call(
        _rnn_fwd_kernel,
        grid=(d // tl, n_chunks),
        in_specs=[fwd_rows] + param_specs(0) + [state],
        out_specs=[fwd_rows, state],
        out_shape=[jax.ShapeDtypeStruct((rows, d), _f32), jax.ShapeDtypeStruct((nb, d), _f32)],
        scratch_shapes=scratch,
        compiler_params=_params(),
        name="rglru_fwd",
    )(u_tb, gate_w, gate_bias, lam, h0f)

    bwd_rows = pl.BlockSpec((chunk, tl), lambda j, c: (n_chunks - 1 - c, j))
    y, hb_first = pl.pallas_call(
        _rnn_bwd_kernel,
        grid=(d // tl, n_chunks),
        in_specs=[bwd_rows, bwd_rows, bwd_rows] + param_specs(1) + [state],
        out_specs=[pl.BlockSpec((nb, tt, tl), lambda j, c: (0, n_chunks - 1 - c, j)), state],
        out_shape=[jax.ShapeDtypeStruct((nb, seq, d), _bf16), jax.ShapeDtypeStruct((nb, d), _f32)],
        scratch_shapes=scratch + [pltpu.VMEM((tl // LANES, chunk, LANES), _f32)],
        compiler_params=_params(),
        name="rglru_bwd",
    )(u_tb, grg_tb, hf_tb, gate_w, gate_bias, lam, h0b)
    return y, hf_last, hb_first


def _gate_weights(wa, wx, ba, bx):
    heads_per_tile = LANE_TILE // RNN_HEAD_DIM
    n_tiles = RNN_HEADS // heads_per_tile

    def tile_blockdiag(w):
        w = w.reshape(n_tiles, heads_per_tile, RNN_HEAD_DIM, RNN_HEAD_DIM)
        eye = jnp.eye(heads_per_tile, dtype=w.dtype)
        full = jnp.einsum('thij,hg->thigj', w, eye)
        return full.reshape(n_tiles, LANE_TILE, LANE_TILE)

    w = (0.5 * jnp.concatenate([tile_blockdiag(wa), tile_blockdiag(wx)], axis=-1)).astype(_bf16)
    bias = jnp.concatenate([ba.reshape(n_tiles, 1, LANE_TILE), bx.reshape(n_tiles, 1, LANE_TILE)], axis=-1)
    return w, (0.5 * bias).astype(_f32)


HY_UNROLL_A = 16
HY_UNROLL_B = 31
HY_EPILOGUE_SLABS = 4


def _dft_constants():
    n1 = np.arange(FFT_H)
    k1 = np.arange(FFT_H + 1)
    ang = 2.0 * np.pi * np.outer(k1, n1) / FFT_N1
    fa = np.concatenate([np.cos(ang), -np.sin(ang)[1:FFT_H]], axis=0)
    fa_k = np.kron(fa, np.eye(SUBLANES))
    weight = np.where((k1 == 0) | (k1 == FFT_H), 1.0, 2.0)[:, None] / FFT_N
    fai = np.concatenate([weight * np.cos(ang), (-2.0 / FFT_N) * np.sin(ang)[1:FFT_H]], axis=0).T
    fai_k = np.kron(fai, np.eye(SUBLANES))

    n2 = np.arange(FFT_N2)
    k2 = np.arange(FFT_N2)

    def cs(k1v):
        idx = (np.outer(k2, n2) * FFT_N1 + k1v * n2[None, :]) % FFT_N
        phi = 2.0 * np.pi * idx / FFT_N
        return np.cos(phi), np.sin(phi)

    fb, fbi = [], []
    for k1v in range(1, FFT_H):
        c, s = cs(k1v)
        fb.append(np.block([[c, s], [-s, c]]))
        fbi.append(np.block([[c.T, -s.T], [s.T, c.T]]))
    c0, s0 = cs(0)
    ch, sh = cs(FFT_H)
    z = np.zeros_like(c0)
    fbs = np.block([[c0, z], [-s0, z], [z, ch], [z, -sh]])
    fbsi = np.block([[c0.T, -s0.T, z, z], [z, z, ch.T, -sh.T]])
    as32 = lambda a: np.asarray(a, np.float32)
    return as32(fa_k), as32(fai_k), as32(np.stack(fb)), as32(np.stack(fbi)), as32(fbs), as32(fbsi)


def _hyena_kernel(z_ref, gate_ref, kf_ref, skip_ref, fa_ref, fai_ref, fb_ref, fbi_ref, fbs_ref, fbsi_ref,
                  o_ref, w_scr, conv_scr):
    tl = z_ref.shape[-1]
    half = FFT_N2
    pack = 2 * SUBLANES

    for p in range(FFT_N2 // pack):
        zz = z_ref[:, p * pack:(p + 1) * pack, :].astype(_f32)
        for hf in range(2):
            r0 = p * pack + hf * SUBLANES
            zin = zz[:, hf * SUBLANES:(hf + 1) * SUBLANES, :].reshape(FFT_H * SUBLANES, tl)
            y = _dot(fa_ref[...], zin.astype(_bf16))
            w_scr[:, r0:r0 + SUBLANES, :] = y.reshape(FFT_N1, SUBLANES, tl)

    yin = jnp.concatenate([w_scr[0], w_scr[FFT_H]], axis=0).astype(_bf16)
    x = _dot(fbs_ref[...], yin)
    pieces = []
    for q in range(2):
        xr = x[(2 * q) * half:(2 * q + 1) * half]
        xi = x[(2 * q + 1) * half:(2 * q + 2) * half]
        kr = kf_ref[pl.ds((2 * q) * half, half), :]
        ki = kf_ref[pl.ds((2 * q + 1) * half, half), :]
        pieces += [xr * kr - xi * ki, xr * ki + xi * kr]
    v = _dot(fbsi_ref[...], jnp.concatenate(pieces, axis=0).astype(_bf16))
    w_scr[0] = v[:half]
    w_scr[FFT_H] = v[half:]

    def stage_b(k1, carry):
        yin = jnp.concatenate([w_scr[k1], w_scr[FFT_H + k1]], axis=0).astype(_bf16)
        x = _dot(fb_ref[k1 - 1], yin)
        xr, xi = x[:half], x[half:]
        base = pl.multiple_of(2 * half * (k1 + 1), 2 * half)
        kr = kf_ref[pl.ds(base, half), :]
        ki = kf_ref[pl.ds(base + half, half), :]
        p = jnp.concatenate([xr * kr - xi * ki, xr * ki + xi * kr], axis=0).astype(_bf16)
        v = _dot(fbi_ref[k1 - 1], p)
        w_scr[k1] = v[:half]
        w_scr[FFT_H + k1] = v[half:]
        return carry

    lax.fori_loop(1, FFT_H, stage_b, 0, unroll=HY_UNROLL_B)

    for g in range(FFT_N2 // SUBLANES):
        r0 = g * SUBLANES
        vin = w_scr[:, r0:r0 + SUBLANES, :].reshape(FFT_N1 * SUBLANES, tl)
        conv_scr[:, r0:r0 + SUBLANES, :] = _dot(fai_ref[...], vin.astype(_bf16)).reshape(FFT_H, SUBLANES, tl)

    skip = skip_ref[...]
    for c in range(0, FFT_H, HY_EPILOGUE_SLABS):
        sl = slice(c, c + HY_EPILOGUE_SLABS)
        out = gate_ref[sl].astype(_f32) * (conv_scr[sl] + skip * z_ref[sl].astype(_f32))
        o_ref[sl] = out.astype(o_ref.dtype)


def _single(shape, index_map):
    return pl.BlockSpec(shape, index_map, pipeline_mode=pl.Buffered(1))


def _hyena_order(z_arr, z_col, gate_arr, gate_col, kf, order, skip, consts, out_dtype):
    fa_k, fai_k, fb, fbi, fbs, fbsi = consts
    b = z_arr.shape[0]
    tl = LANE_TILE
    d_out = kf.shape[-1]
    seq_block = (None, FFT_H, FFT_N2, tl)
    const2 = lambda shape: _single(shape, lambda j, i: (0, 0))
    const3 = lambda shape: _single(shape, lambda j, i: (0, 0, 0))
    return pl.pallas_call(
        _hyena_kernel,
        grid=(d_out // tl, b),
        in_specs=[pl.BlockSpec(seq_block, lambda j, i: (i, 0, 0, z_col + j)),
                  pl.BlockSpec(seq_block, lambda j, i: (i, 0, 0, gate_col + j)),
                  pl.BlockSpec((None, kf.shape[1], tl), lambda j, i: (order, 0, j)),
                  pl.BlockSpec((1, tl), lambda j, i: (0, j)),
                  const2(fa_k.shape), const2(fai_k.shape), const3(fb.shape), const3(fbi.shape),
                  const2(fbs.shape), const2(fbsi.shape)],
        out_specs=pl.BlockSpec(seq_block, lambda j, i: (i, 0, 0, j)),
        out_shape=jax.ShapeDtypeStruct((b, FFT_H, FFT_N2, d_out), out_dtype),
        scratch_shapes=[pltpu.VMEM((FFT_N1, FFT_N2, tl), _f32), pltpu.VMEM((FFT_H, FFT_N2, tl), _f32)],
        compiler_params=_params(),
        name="hyena_conv",
    )(z_arr, gate_arr, kf, skip, fa_k, fai_k, fb, fbi, fbs, fbsi)


def _filter_kernel(hdn_ref, w4f_ref, w4b_ref, delta_ref, fa_ref, fb_ref, fbs_ref, o_ref, sig_scr, w_scr):
    seq = hdn_ref.shape[0]
    tl = o_ref.shape[-1]
    half = FFT_N2
    def dot3(w):
        w_hi = w.astype(_bf16)
        w_lo = (w - w_hi.astype(_f32)).astype(_bf16)
        return _dot(hdn_ref[...], jnp.concatenate([w_hi, w_hi, w_lo], axis=0))

    row = lax.broadcasted_iota(jnp.int32, (seq, tl), 0)
    decay = jnp.exp(-(row.astype(_f32) * (1.0 / (seq - 1))) * delta_ref[...])
    f = dot3(w4f_ref[...]) * decay
    g = dot3(w4b_ref[...]) * decay
    g = jnp.where(row == 0, 0.0, g)
    norm = jnp.sum(jnp.abs(f), axis=0, keepdims=True) + jnp.sum(jnp.abs(g), axis=0, keepdims=True)
    inv_norm = 1.0 / norm
    f = f * inv_norm
    g = g * inv_norm

    for part, sig in enumerate((f + g, f - g)):
        sig_scr[...] = sig.reshape(FFT_H, FFT_N2, tl)

        def stage_a(gi, carry):
            r0 = pl.multiple_of(gi * SUBLANES, SUBLANES)
            zin = sig_scr[:, pl.ds(r0, SUBLANES), :].reshape(FFT_H * SUBLANES, tl)
            y = _dot(fa_ref[...], zin.astype(_bf16))
            w_scr[:, pl.ds(r0, SUBLANES), :] = y.reshape(FFT_N1, SUBLANES, tl)
            return carry

        lax.fori_loop(0, FFT_N2 // SUBLANES, stage_a, 0, unroll=HY_UNROLL_A)

        yin = jnp.concatenate([w_scr[0], w_scr[FFT_H]], axis=0).astype(_bf16)
        o_ref[pl.ds(part * half, half), :] = _dot(fbs_ref[pl.ds(part * half, half), :], yin)
        o_ref[pl.ds((2 + part) * half, half), :] = _dot(fbs_ref[pl.ds((2 + part) * half, half), :], yin)

        def stage_b(k1, carry):
            yin = jnp.concatenate([w_scr[k1], w_scr[FFT_H + k1]], axis=0).astype(_bf16)
            base = pl.multiple_of(2 * half * (k1 + 1) + part * half, half)
            o_ref[pl.ds(base, half), :] = _dot(fb_ref[k1 - 1, pl.ds(part * half, half), :], yin)
            return carry

        lax.fori_loop(1, FFT_H, stage_b, 0, unroll=HY_UNROLL_B)


def _filter_features(seq, hy_w1, hy_b1, hy_w2, hy_b2, hy_w3, hy_b3, hy_freq):
    t = jnp.linspace(0.0, 1.0, seq, dtype=_f32)[:, None]
    w = 2.0 * math.pi * jnp.arange(seq, dtype=_f32)[:, None] / seq
    f = jnp.linspace(1e-4, HY_BANDS - 1, HY_BANDS, dtype=_f32)[None, :]
    z = jnp.concatenate([t, jnp.cos(f * w), -jnp.sin(f * w)], axis=-1)
    hi = lax.Precision.HIGHEST
    hdn = jnp.sin(hy_freq * (jnp.dot(z, hy_w1, precision=hi) + hy_b1))
    hdn = jnp.sin(hy_freq * (jnp.dot(hdn, hy_w2, precision=hi) + hy_b2))
    hdn = jnp.sin(hy_freq * (jnp.dot(hdn, hy_w3, precision=hi) + hy_b3))
    hdn_hi = hdn.astype(_bf16)
    hdn_lo = (hdn - hdn_hi.astype(_f32)).astype(_bf16)
    return jnp.concatenate([hdn_hi, hdn_lo, hdn_hi], axis=1)


def _filter_spectra(hdn, hy_w4, consts):
    fa_k, _, fb, _, fbs, _ = consts
    width = hy_w4.shape[0]
    d = hy_w4.shape[1] // (2 * HY_ORDER)
    tl = LANE_TILE
    w4 = hy_w4.reshape(width, 2 * HY_ORDER, d).transpose(1, 0, 2)
    max_decay = math.log(HY_TARGET) / HY_FAST_DECAY
    min_decay = math.log(HY_TARGET) / HY_SLOW_DECAY
    deltas = jnp.abs(jnp.linspace(min_decay, max_decay, d, dtype=_f32)).reshape(1, d)
    n_rows = (FFT_H + 1) * 2 * FFT_N2
    const2 = lambda shape: _single(shape, lambda o, j: (0, 0))
    return pl.pallas_call(
        _filter_kernel,
        grid=(HY_ORDER, d // tl),
        in_specs=[const2(hdn.shape),
                  pl.BlockSpec((None, width, tl), lambda o, j: (2 * o, 0, j)),
                  pl.BlockSpec((None, width, tl), lambda o, j: (2 * o + 1, 0, j)),
                  pl.BlockSpec((1, tl), lambda o, j: (0, j)),
                  const2(fa_k.shape), _single(fb.shape, lambda o, j: (0, 0, 0)), const2(fbs.shape)],
        out_specs=pl.BlockSpec((None, n_rows, tl), lambda o, j: (o, 0, j)),
        out_shape=jax.ShapeDtypeStruct((HY_ORDER, n_rows, d), _f32),
        scratch_shapes=[pltpu.VMEM((FFT_H, FFT_N2, tl), _f32), pltpu.VMEM((FFT_N1, FFT_N2, tl), _f32)],
        compiler_params=_params(),
        name="filter_spectra",
    )(hdn, w4, w4, deltas, fa_k, fb, fbs)


def _merge_kernel(hx_ref, yr_ref, yh_ref, x_ref, wbg_ref, bbg_ref, wa_ref, wb_ref, wo_ref, g1_ref,
                  g_ref, sh_ref, sc_ref, x1_ref, h2_ref):
    d = x_ref.shape[-1]
    gates = _sigmoid(_dot(hx_ref[...], wbg_ref[...]) + bbg_ref[...])
    ya = _dot(yr_ref[...], wa_ref[...])
    yb = _dot(yh_ref[...].astype(_bf16), wb_ref[...])
    merged = gates[:, :d] * ya + gates[:, d:] * yb
    x1 = x_ref[...] + g1_ref[...] * _dot(merged.astype(_bf16), wo_ref[...])
    x1_ref[...] = x1
    y = x1 * lax.rsqrt(jnp.mean(x1 * x1, axis=-1, keepdims=True) + RMS_EPS) * g_ref[...]
    h2_ref[...] = (y * (1.0 + sc_ref[...]) + sh_ref[...]).astype(h2_ref.dtype)


def _merge(hx, y_rnn, y_hy, x, w_bg, b_bg, w_a, w_b, w_o, g1, norm2_g, sh2, sc2, tm=512):
    b, l, d = x.shape
    row = lambda width: pl.BlockSpec((None, tm, width), lambda i, j: (i, j, 0))
    vec = pl.BlockSpec((None, 1, d), lambda i, j: (i, 0, 0))
    full = lambda a: pl.BlockSpec(a.shape, lambda i, j: (0,) * a.ndim)
    b_bg = b_bg.reshape(1, -1)
    norm2_g = norm2_g.reshape(1, d)
    return pl.pallas_call(
        _merge_kernel,
        grid=(b, l // tm),
        in_specs=[row(d), row(d), row(d), row(d), full(w_bg), full(b_bg), full(w_a), full(w_b), full(w_o),
                  vec, full(norm2_g), vec, vec],
        out_specs=[row(d), row(d)],
        out_shape=[jax.ShapeDtypeStruct((b, l, d), _f32), jax.ShapeDtypeStruct((b, l, d), _bf16)],
        compiler_params=_params(),
        name="merge",
    )(hx, y_rnn, y_hy, x, w_bg, b_bg, w_a, w_b, w_o, g1, norm2_g, sh2, sc2)


def _ffn_kernel(h_ref, x1_ref, wg_ref, wu_ref, wo_ref, g2_ref, fg_ref, o_ref):
    h = h_ref[...]
    g = _dot(h, wg_ref[...])
    u = _dot(h, wu_ref[...])
    a = (g * _sigmoid(g) * u).astype(_bf16)
    x2 = x1_ref[...] + g2_ref[...] * _dot(a, wo_ref[...])
    y = x2 * lax.rsqrt(jnp.mean(x2 * x2, axis=-1, keepdims=True) + RMS_EPS)
    o_ref[...] = y * fg_ref[...]


def _ffn(h2, x1, w_gu, w_o, g2, final_g, tm=512):
    b, l, d = x1.shape
    d_ff = w_o.shape[0]
    row = pl.BlockSpec((None, tm, d), lambda i, j: (i, j, 0))
    vec = pl.BlockSpec((None, 1, d), lambda i, j: (i, 0, 0))
    full = lambda a: _single(a.shape, lambda i, j: (0,) * a.ndim)
    final_g = final_g.reshape(1, d)
    return pl.pallas_call(
        _ffn_kernel,
        grid=(b, l // tm),
        in_specs=[row, row, _single((d, d_ff), lambda i, j: (0, 0)), _single((d, d_ff), lambda i, j: (0, 1)),
                  full(w_o), vec, full(final_g)],
        out_specs=row,
        out_shape=jax.ShapeDtypeStruct((b, l, d), _f32),
        compiler_params=_params(),
        name="ffn",
    )(h2, x1, w_gu, w_gu, w_o, g2, final_g)


def kernel(x, c, ctx, c_ctx, w_mod, b_mod, norm1_g, norm2_g, w_in, b_in, rnn_conv_w, rnn_conv_b, rg_wa, rg_ba,
           rg_wx, rg_bx, rg_lambda, hy_conv_w, hy_conv_b, hy_w1, hy_b1, hy_w2, hy_b2, hy_w3, hy_b3, hy_freq,
           hy_w4, hy_skip, w_a_out, w_b_out, w_out, w_ffn_in, w_ffn_out, final_g):
    assert w_mod.shape[0] == 1, "single-layer block"
    b, seq, d = x.shape
    ctx_len = ctx.shape[1]
    assert seq == FFT_H * FFT_N2 and d == D_MODEL
    assert b == SUBLANES, "time-major rows put the batch on the sublanes of one register"
    (w_mod, b_mod, norm1_g, norm2_g, w_in, b_in, rnn_conv_w, rnn_conv_b, rg_wa, rg_ba, rg_wx, rg_bx,
     rg_lambda, hy_conv_w, hy_conv_b, hy_w1, hy_b1, hy_w2, hy_b2, hy_w3, hy_b3, hy_freq, hy_w4, hy_skip,
     w_a_out, w_b_out, w_out, w_ffn_in, w_ffn_out) = [
        a[0] for a in (w_mod, b_mod, norm1_g, norm2_g, w_in, b_in, rnn_conv_w, rnn_conv_b, rg_wa, rg_ba,
                       rg_wx, rg_bx, rg_lambda, hy_conv_w, hy_conv_b, hy_w1, hy_b1, hy_w2, hy_b2, hy_w3,
                       hy_b3, hy_freq, hy_w4, hy_skip, w_a_out, w_b_out, w_out, w_ffn_in, w_ffn_out)]

    pad_rows = 2 * SUBLANES - b - 1
    c_all = jnp.concatenate([c, c_ctx[None, :], jnp.zeros((pad_rows, d), _f32)], axis=0)
    mod = _mod_vectors(c_all, w_mod, b_mod)
    sh1, sc1, g1, sh2, sc2, g2 = [m.reshape(b, 1, d) for m in jnp.split(mod[:b], N_MOD, axis=-1)]
    csh1, csc1 = [jnp.broadcast_to(m.reshape(1, 1, d), (b, 1, d))
                  for m in jnp.split(mod[b], N_MOD, axis=-1)[:2]]

    w_in_bf = w_in.astype(_bf16)
    w_rx, b_rx = w_in_bf[:, :d], b_in[:d]
    w_bg, b_bg = w_in_bf[:, 5 * d:], b_in[5 * d:]
    gate_w, gate_bias = zip(*[_gate_weights(rg_wa[i], rg_wx[i], rg_ba[i], rg_bx[i]) for i in range(2)])
    gate_w, gate_bias = jnp.stack(gate_w), jnp.stack(gate_bias)
    lam = rg_lambda.reshape(2, 1, d)
    zeros_state = jnp.zeros((b, d), _f32)

    _, hc_tb = _norm_mod(ctx, norm1_g, csh1, csc1)
    u_c = _proj_conv(hc_tb, w_rx, b_rx, rnn_conv_w, rnn_conv_b, pad_left=RNN_CONV_PAD_LEFT, period=ctx_len,
                     row_stride=b, tm=ctx_len * b, out_dtype=_bf16)
    _, cf, cb = _rglru(u_c, jnp.zeros_like(u_c), b, gate_w, gate_bias, lam, zeros_state, zeros_state)

    hx, u, grg, q = _inproj(x, norm1_g, sh1, sc1, w_in_bf, b_in, rnn_conv_w, rnn_conv_b,
                            hy_conv_w, hy_conv_b, GRID_W)
    y_rnn, _, _ = _rglru(u, grg, b, gate_w, gate_bias, lam, cf, cb)

    consts = [jnp.asarray(a).astype(_bf16) for a in _dft_constants()]
    hdn = _filter_features(seq, hy_w1, hy_b1, hy_w2, hy_b2, hy_w3, hy_b3, hy_freq)
    kf = _filter_spectra(hdn, hy_w4, consts)
    q4 = q.reshape(b, FFT_H, FFT_N2, 3 * d)
    n_tiles = d // LANE_TILE
    z1 = _hyena_order(q4, 0, q4, n_tiles, kf, 0, hy_skip[0:1], consts, _bf16)
    y_hy = _hyena_order(z1, 0, q4, 2 * n_tiles, kf, 1, hy_skip[1:2], consts, _bf16)
    y_hy = y_hy.reshape(b, seq, d)

    x1, h2 = _merge(hx, y_rnn, y_hy, x, w_bg, b_bg, w_a_out.astype(_bf16), w_b_out.astype(_bf16),
                    w_out.astype(_bf16), g1, norm2_g, sh2, sc2)
    w_ffn_bf = w_ffn_in.astype(_bf16)
    return _ffn(h2, x1, w_ffn_bf, w_ffn_out.astype(_bf16), g2, final_g)
```

```python
import functools
import math

import numpy as np
import jax
import jax.numpy as jnp
from jax import lax
from jax.experimental import pallas as pl
from jax.experimental.pallas import tpu as pltpu

D_MODEL = 1024
GRID_W = 64
N_MOD = 6
RMS_EPS = 1e-6
RNN_HEADS = 16
RNN_HEAD_DIM = D_MODEL // RNN_HEADS
RNN_CONV_PAD_LEFT = 2
HY_CONV_PAD_LEFT = 1
RG_C = 8.0
HY_ORDER = 2
HY_EMB = 33
HY_BANDS = (HY_EMB - 1) // 2
HY_FAST_DECAY = 0.3
HY_SLOW_DECAY = 1.5
HY_TARGET = 1e-2
D_FF = ((8 * D_MODEL // 3 + 255) // 256) * 256

SUBLANES = 8
LANE_TILE = 256
VMEM_LIMIT = 56 * 1024 * 1024

FFT_N1 = 64
FFT_N2 = 128
FFT_N = FFT_N1 * FFT_N2
FFT_H = FFT_N1 // 2

_f32 = jnp.float32
_bf16 = jnp.bfloat16


def _params(**kw):
    return pltpu.CompilerParams(vmem_limit_bytes=VMEM_LIMIT, **kw)


def _dot(a, b):
    return jnp.dot(a, b, preferred_element_type=_f32)


def _sigmoid(x):
    return 0.5 * jnp.tanh(0.5 * x) + 0.5


def _mod_kernel(c_ref, w_ref, b_ref, o_ref):
    c = c_ref[...]
    s = c * _sigmoid(c)
    s_hi = s.astype(_bf16)
    s_lo = (s - s_hi.astype(_f32)).astype(_bf16)
    w = w_ref[...]
    w_hi = w.astype(_bf16)
    w_lo = (w - w_hi.astype(_f32)).astype(_bf16)
    o_ref[...] = _dot(jnp.concatenate([s_hi, s_lo, s_hi], axis=1),
                      jnp.concatenate([w_hi, w_hi, w_lo], axis=0)) + b_ref[...]


def _mod_vectors(c_all, w_mod, b_mod):
    rows, d = c_all.shape
    n = w_mod.shape[1]
    tn = 1024
    return pl.pallas_call(
        _mod_kernel,
        grid=(n // tn,),
        in_specs=[pl.BlockSpec((rows, d), lambda j: (0, 0)),
                  pl.BlockSpec((d, tn), lambda j: (0, j)),
                  pl.BlockSpec((1, tn), lambda j: (0, j))],
        out_specs=pl.BlockSpec((rows, tn), lambda j: (0, j)),
        out_shape=jax.ShapeDtypeStruct((rows, n), _f32),
        compiler_params=_params(),
        name="mod_vectors",
    )(c_all, w_mod, b_mod.reshape(1, n))


LANES = 128
NORM_TT = 128


def _norm_mod_kernel(x_ref, g_ref, sh_ref, sc_ref, otb_ref, tb_scr):
    nb, tt, d = x_ref.shape
    for b in range(nb):
        x = x_ref[b]
        gain = g_ref[...] * (1.0 + sc_ref[b])
        y = x * lax.rsqrt(jnp.mean(x * x, axis=-1, keepdims=True) + RMS_EPS) * gain + sh_ref[b]
        for s in range(d // LANES):
            tb_scr.at[s][pl.ds(b, tt, stride=nb), :] = y[:, s * LANES:(s + 1) * LANES]
    for s in range(d // LANES):
        otb_ref[:, s * LANES:(s + 1) * LANES] = tb_scr[s].astype(otb_ref.dtype)


def _norm_mod(x, g, sh, sc):
    b, l, d = x.shape
    tt = NORM_TT
    vec = pl.BlockSpec((b, 1, d), lambda i: (0, 0, 0))
    return pl.pallas_call(
        _norm_mod_kernel,
        grid=(l // tt,),
        in_specs=[pl.BlockSpec((b, tt, d), lambda i: (0, i, 0)),
                  pl.BlockSpec((1, d), lambda i: (0, 0)), vec, vec],
        out_specs=pl.BlockSpec((tt * b, d), lambda i: (i, 0)),
        out_shape=jax.ShapeDtypeStruct((l * b, d), _bf16),
        scratch_shapes=[pltpu.VMEM((d // LANES, tt * b, LANES), _f32)],
        compiler_params=_params(),
        name="norm_mod",
    )(x, g.reshape(1, d), sh, sc)


CONV_CHUNK_STEPS = 8


def _short_conv_period(y, w, cb, pad_left, nb, store):
    rows, lanes = y.shape
    n_taps = w.shape[0]
    ypad = jnp.concatenate([jnp.zeros((pad_left * nb, lanes), y.dtype), y,
                            jnp.zeros(((n_taps - 1 - pad_left) * nb, lanes), y.dtype)], axis=0)
    chunk = CONV_CHUNK_STEPS * nb
    for r0 in range(0, rows, chunk):
        out = cb
        for k in range(n_taps):
            out = out + ypad[r0 + k * nb:r0 + k * nb + chunk] * w[k:k + 1, :]
        store(r0, r0 + chunk, out)


def _proj_conv_kernel(x_ref, w_ref, b_ref, cw_ref, cb_ref, o_ref, *, pad_left, period, row_stride):
    tm, tn = o_ref.shape
    prow = period * row_stride
    for j in range(tn // LANE_TILE):
        cols = slice(j * LANE_TILE, (j + 1) * LANE_TILE)
        y = _dot(x_ref[...], w_ref[:, cols]) + b_ref[:, cols]
        for p in range(tm // prow):
            def store(r0, r1, val, base=p * prow, cols=cols):
                o_ref[base + r0:base + r1, cols] = val.astype(o_ref.dtype)

            _short_conv_period(y[p * prow:(p + 1) * prow], cw_ref[:, cols], cb_ref[:, cols], pad_left,
                               row_stride, store)


def _proj_conv(x2d, w, bias, conv_w, conv_b, *, pad_left, period, row_stride, tm, tn=1024, out_dtype=_f32):
    assert tm % (period * row_stride) == 0
    m, k = x2d.shape
    n = w.shape[1]
    col = lambda rows: pl.BlockSpec((rows, tn), lambda j, i: (0, j))
    return pl.pallas_call(
        functools.partial(_proj_conv_kernel, pad_left=pad_left, period=period, row_stride=row_stride),
        grid=(n // tn, m // tm),
        in_specs=[pl.BlockSpec((tm, k), lambda j, i: (i, 0)), col(k), col(1), col(conv_w.shape[0]), col(1)],
        out_specs=pl.BlockSpec((tm, tn), lambda j, i: (i, j)),
        out_shape=jax.ShapeDtypeStruct((m, n), out_dtype),
        compiler_params=_params(),
        name="proj_conv",
    )(x2d, w, bias.reshape(1, n), conv_w, conv_b.reshape(1, n))


def _inproj_kernel(x_ref, g_ref, sh_ref, sc_ref, w_ref, b_ref, rcw_ref, rcb_ref, hcw_ref, hcb_ref,
                   hx_ref, u_ref, q_ref, xtb_scr, lhs_scr, qtb_scr, *, rnn_pad, hy_pad, hy_col0):
    nb, tt, d = x_ref.shape
    rows = tt * nb
    for b in range(nb):
        x = x_ref[b]
        gain = g_ref[...] * (1.0 + sc_ref[b])
        y = x * lax.rsqrt(jnp.mean(x * x, axis=-1, keepdims=True) + RMS_EPS) * gain + sh_ref[b]
        hx_ref[b] = y.astype(hx_ref.dtype)
        for s in range(d // LANES):
            xtb_scr.at[s][pl.ds(b, tt, stride=nb), :] = y[:, s * LANES:(s + 1) * LANES]
    for s in range(d // LANES):
        lhs_scr[:, s * LANES:(s + 1) * LANES] = xtb_scr[s].astype(lhs_scr.dtype)

    n_rx = u_ref.shape[1] // LANE_TILE
    n_hy = q_ref.shape[2] // LANE_TILE
    slabs = LANE_TILE // LANES
    for j in range(n_rx + n_hy):
        c0 = j * LANE_TILE if j < n_rx else hy_col0 + (j - n_rx) * LANE_TILE
        cols = slice(c0, c0 + LANE_TILE)
        y = _dot(lhs_scr[...], w_ref[:, cols]) + b_ref[:, cols]
        if j < n_rx:
            def store_u(r0, r1, val, cols=cols):
                u_ref[r0:r1, cols] = val.astype(u_ref.dtype)

            _short_conv_period(y, rcw_ref[:, cols], rcb_ref[:, cols], rnn_pad, nb, store_u)
        else:
            jh = j - n_rx
            oc = slice(jh * LANE_TILE, (jh + 1) * LANE_TILE)

            def store_q(r0, r1, val, jh=jh):
                for s in range(slabs):
                    qtb_scr[jh * slabs + s, r0:r1, :] = val[:, s * LANES:(s + 1) * LANES]

            _short_conv_period(y, hcw_ref[:, oc], hcb_ref[:, oc], hy_pad, nb, store_q)
            for s in range(slabs):
                slab = jh * slabs + s
                for bi in range(nb):
                    piece = qtb_scr.at[slab][pl.ds(bi, tt, stride=nb), :]
                    q_ref[bi, :, slab * LANES:(slab + 1) * LANES] = piece.astype(q_ref.dtype)


def _inproj(x, g, sh, sc, w_in_bf, b_in, rnn_conv_w, rnn_conv_b, hy_conv_w, hy_conv_b, period):
    b, l, d = x.shape
    n_all = w_in_bf.shape[1]
    n_hy = hy_conv_w.shape[1]
    tt = period
    vec = pl.BlockSpec((b, 1, d), lambda i: (0, 0, 0))
    full = lambda a: _single(a.shape, lambda i: (0,) * a.ndim)
    g2, b2 = g.reshape(1, d), b_in.reshape(1, n_all)
    rcb, hcb = rnn_conv_b.reshape(1, d), hy_conv_b.reshape(1, n_hy)
    return pl.pallas_call(
        functools.partial(_inproj_kernel, rnn_pad=RNN_CONV_PAD_LEFT, hy_pad=HY_CONV_PAD_LEFT, hy_col0=2 * d),
        grid=(l // tt,),
        in_specs=[pl.BlockSpec((b, tt, d), lambda i: (0, i, 0)), full(g2), vec, vec,
                  full(w_in_bf), full(b2), full(rnn_conv_w), full(rcb), full(hy_conv_w), full(hcb)],
        out_specs=[pl.BlockSpec((b, tt, d), lambda i: (0, i, 0)), pl.BlockSpec((tt * b, d), lambda i: (i, 0)),
                   pl.BlockSpec((b, tt, n_hy), lambda i: (0, i, 0))],
        out_shape=[jax.ShapeDtypeStruct((b, l, d), _bf16), jax.ShapeDtypeStruct((l * b, d), _bf16),
                   jax.ShapeDtypeStruct((b, l, n_hy), _bf16)],
        scratch_shapes=[pltpu.VMEM((d // LANES, tt * b, LANES), _f32),
                        pltpu.VMEM((tt * b, d), _bf16),
                        pltpu.VMEM((n_hy // LANES, tt * b, LANES), _f32)],
        compiler_params=_params(),
        name="inproj",
    )(x, g2, sh, sc, w_in_bf, b2, rnn_conv_w, rcb, hy_conv_w, hcb)


RNN_TT = 512
RNN_SUB_TT = 64


def _rnn_coeffs(u, w_ref, bias_ref, lam_ref):
    tl = u.shape[-1]
    half_decay = (-0.5 * RG_C) * jax.nn.softplus(-lam_ref[...])
    g = _dot(u.astype(_bf16), w_ref[...]) + bias_ref[...]
    log_a = half_decay * jnp.tanh(g[:, :tl]) + half_decay
    i = 0.5 * jnp.tanh(g[:, tl:]) + 0.5
    a = jnp.exp(log_a)
    x = -jnp.tanh(log_a) * (a * a + 1.0)
    root = jnp.where(x > 0.0, x * lax.rsqrt(x), 0.0)
    return a, root * (i * u.astype(_f32))


def _rnn_scan(u_ref, w_ref, bias_ref, lam_ref, a_scr, b_scr, h_ref, h0, reverse):
    nb = h0.shape[0]
    rows = u_ref.shape[0]
    sub = RNN_SUB_TT * nb
    n_sub = rows // sub
    order = list(range(n_sub))[::-1] if reverse else list(range(n_sub))

    def coeffs(k):
        sl = slice(k * sub, (k + 1) * sub)
        a, b = _rnn_coeffs(u_ref[sl, :], w_ref, bias_ref, lam_ref)
        a_scr[sl, :] = a
        b_scr[sl, :] = b

    coeffs(order[0])
    h = h0
    for pos, k in enumerate(order):
        if pos + 1 < n_sub:
            coeffs(order[pos + 1])
        steps = list(range(k * sub, (k + 1) * sub, nb))
        for r0 in (steps[::-1] if reverse else steps):
            h = a_scr[r0:r0 + nb, :] * h + b_scr[r0:r0 + nb, :]
            h_ref[r0:r0 + nb, :] = h
    return h


def _rnn_fwd_kernel(u_ref, w_ref, bias_ref, lam_ref, h0_ref, hf_ref, hlast_ref, a_scr, b_scr, h_scr):
    @pl.when(pl.program_id(1) == 0)
    def _():
        h_scr[...] = h0_ref[...]

    h = _rnn_scan(u_ref, w_ref, bias_ref, lam_ref, a_scr, b_scr, hf_ref, h_scr[...], False)
    h_scr[...] = h
    hlast_ref[...] = h


def _rnn_bwd_kernel(u_ref, hf_ref, w_ref, bias_ref, lam_ref, h0_ref, y_ref, hfirst_ref,
                    a_scr, b_scr, h_scr, y_scr):
    @pl.when(pl.program_id(1) == 0)
    def _():
        h_scr[...] = h0_ref[...]

    nb, tt, tl = y_ref.shape
    h = _rnn_scan(u_ref, w_ref, bias_ref, lam_ref, a_scr, b_scr, b_scr, h_scr[...], True)
    h_scr[...] = h
    hfirst_ref[...] = h
    y = hf_ref[...] + b_scr[...]
    for s in range(tl // LANES):
        y_scr[s] = y[:, s * LANES:(s + 1) * LANES]
    for bi in range(nb):
        for s in range(tl // LANES):
            piece = y_scr.at[s][pl.ds(bi, tt, stride=nb), :]
            y_ref[bi, :, s * LANES:(s + 1) * LANES] = piece.astype(y_ref.dtype)


def _rglru(u_tb, nb, gate_w, gate_bias, lam, h0f, h0b):
    rows, d = u_tb.shape
    seq = rows // nb
    tl = LANE_TILE
    tt = min(RNN_TT, seq)
    n_chunks = seq // tt
    chunk = tt * nb
    state = pl.BlockSpec((nb, tl), lambda j, c: (0, j))
    scratch = [pltpu.VMEM((chunk, tl), _f32), pltpu.VMEM((chunk, tl), _f32), pltpu.VMEM((nb, tl), _f32)]

    def param_specs(direction):
        return [pl.BlockSpec((None, None, tl, 2 * tl), lambda j, c: (direction, j, 0, 0)),
                pl.BlockSpec((None, None, 1, 2 * tl), lambda j, c: (direction, j, 0, 0)),
                pl.BlockSpec((None, 1, tl), lambda j, c: (direction, 0, j))]

    fwd_rows = pl.BlockSpec((chunk, tl), lambda j, c: (c, j))
    hf_tb, hf_last = pl.pallas_call(
        _rnn_fwd_kernel,
        grid=(d // tl, n_chunks),
        in_specs=[fwd_rows] + param_specs(0) + [state],
        out_specs=[fwd_rows, state],
        out_shape=[jax.ShapeDtypeStruct((rows, d), _f32), jax.ShapeDtypeStruct((nb, d), _f32)],
        scratch_shapes=scratch,
        compiler_params=_params(),
        name="rglru_fwd",
    )(u_tb, gate_w, gate_bias, lam, h0f)

    bwd_rows = pl.BlockSpec((chunk, tl), lambda j, c: (n_chunks - 1 - c, j))
    y, hb_first = pl.pallas_call(
        _rnn_bwd_kernel,
        grid=(d // tl, n_chunks),
        in_specs=[bwd_rows, bwd_rows] + param_specs(1) + [state],
        out_specs=[pl.BlockSpec((nb, tt, tl), lambda j, c: (0, n_chunks - 1 - c, j)), state],
        out_shape=[jax.ShapeDtypeStruct((nb, seq, d), _bf16), jax.ShapeDtypeStruct((nb, d), _f32)],
        scratch_shapes=scratch + [pltpu.VMEM((tl // LANES, chunk, LANES), _f32)],
        compiler_params=_params(),
        name="rglru_bwd",
    )(u_tb, hf_tb, gate_w, gate_bias, lam, h0b)
    return y, hf_last, hb_first


def _gate_weights(wa, wx, ba, bx):
    heads_per_tile = LANE_TILE // RNN_HEAD_DIM
    n_tiles = RNN_HEADS // heads_per_tile

    def tile_blockdiag(w):
        w = w.reshape(n_tiles, heads_per_tile, RNN_HEAD_DIM, RNN_HEAD_DIM)
        eye = jnp.eye(heads_per_tile, dtype=w.dtype)
        full = jnp.einsum('thij,hg->thigj', w, eye)
        return full.reshape(n_tiles, LANE_TILE, LANE_TILE)

    w = (0.5 * jnp.concatenate([tile_blockdiag(wa), tile_blockdiag(wx)], axis=-1)).astype(_bf16)
    bias = jnp.concatenate([ba.reshape(n_tiles, 1, LANE_TILE), bx.reshape(n_tiles, 1, LANE_TILE)], axis=-1)
    return w, (0.5 * bias).astype(_f32)


HY_UNROLL_A = 16
HY_UNROLL_B = 31
HY_EPILOGUE_SLABS = 4


def _dft_constants():
    n1 = np.arange(FFT_H)
    k1 = np.arange(FFT_H + 1)
    ang = 2.0 * np.pi * np.outer(k1, n1) / FFT_N1
    fa = np.concatenate([np.cos(ang), -np.sin(ang)[1:FFT_H]], axis=0)
    fa_k = np.kron(fa, np.eye(SUBLANES))
    weight = np.where((k1 == 0) | (k1 == FFT_H), 1.0, 2.0)[:, None] / FFT_N
    fai = np.concatenate([weight * np.cos(ang), (-2.0 / FFT_N) * np.sin(ang)[1:FFT_H]], axis=0).T
    fai_k = np.kron(fai, np.eye(SUBLANES))

    n2 = np.arange(FFT_N2)
    k2 = np.arange(FFT_N2)

    def cs(k1v):
        idx = (np.outer(k2, n2) * FFT_N1 + k1v * n2[None, :]) % FFT_N
        phi = 2.0 * np.pi * idx / FFT_N
        return np.cos(phi), np.sin(phi)

    fb, fbi = [], []
    for k1v in range(1, FFT_H):
        c, s = cs(k1v)
        fb.append(np.block([[c, s], [-s, c]]))
        fbi.append(np.block([[c.T, -s.T], [s.T, c.T]]))
    c0, s0 = cs(0)
    ch, sh = cs(FFT_H)
    z = np.zeros_like(c0)
    fbs = np.block([[c0, z], [-s0, z], [z, ch], [z, -sh]])
    fbsi = np.block([[c0.T, -s0.T, z, z], [z, z, ch.T, -sh.T]])
    as32 = lambda a: np.asarray(a, np.float32)
    return as32(fa_k), as32(fai_k), as32(np.stack(fb)), as32(np.stack(fbi)), as32(fbs), as32(fbsi)


def _hyena_kernel(z_ref, gate_ref, kf_ref, skip_ref, fa_ref, fai_ref, fb_ref, fbi_ref, fbs_ref, fbsi_ref,
                  o_ref, w_scr, conv_scr):
    tl = z_ref.shape[-1]
    half = FFT_N2
    pack = 2 * SUBLANES

    for p in range(FFT_N2 // pack):
        zz = z_ref[:, p * pack:(p + 1) * pack, :].astype(_f32)
        for hf in range(2):
            r0 = p * pack + hf * SUBLANES
            zin = zz[:, hf * SUBLANES:(hf + 1) * SUBLANES, :].reshape(FFT_H * SUBLANES, tl)
            y = _dot(fa_ref[...], zin.astype(_bf16))
            w_scr[:, r0:r0 + SUBLANES, :] = y.reshape(FFT_N1, SUBLANES, tl)

    yin = jnp.concatenate([w_scr[0], w_scr[FFT_H]], axis=0).astype(_bf16)
    x = _dot(fbs_ref[...], yin)
    pieces = []
    for q in range(2):
        xr = x[(2 * q) * half:(2 * q + 1) * half]
        xi = x[(2 * q + 1) * half:(2 * q + 2) * half]
        kr = kf_ref[pl.ds((2 * q) * half, half), :]
        ki = kf_ref[pl.ds((2 * q + 1) * half, half), :]
        pieces += [xr * kr - xi * ki, xr * ki + xi * kr]
    v = _dot(fbsi_ref[...], jnp.concatenate(pieces, axis=0).astype(_bf16))
    w_scr[0] = v[:half]
    w_scr[FFT_H] = v[half:]

    def stage_b(k1, carry):
        yin = jnp.concatenate([w_scr[k1], w_scr[FFT_H + k1]], axis=0).astype(_bf16)
        x = _dot(fb_ref[k1 - 1], yin)
        xr, xi = x[:half], x[half:]
        base = pl.multiple_of(2 * half * (k1 + 1), 2 * half)
        kr = kf_ref[pl.ds(base, half), :]
        ki = kf_ref[pl.ds(base + half, half), :]
        p = jnp.concatenate([xr * kr - xi * ki, xr * ki + xi * kr], axis=0).astype(_bf16)
        v = _dot(fbi_ref[k1 - 1], p)
        w_scr[k1] = v[:half]
        w_scr[FFT_H + k1] = v[half:]
        return carry

    lax.fori_loop(1, FFT_H, stage_b, 0, unroll=HY_UNROLL_B)

    for g in range(FFT_N2 // SUBLANES):
        r0 = g * SUBLANES
        vin = w_scr[:, r0:r0 + SUBLANES, :].reshape(FFT_N1 * SUBLANES, tl)
        conv_scr[:, r0:r0 + SUBLANES, :] = _dot(fai_ref[...], vin.astype(_bf16)).reshape(FFT_H, SUBLANES, tl)

    skip = skip_ref[...]
    for c in range(0, FFT_H, HY_EPILOGUE_SLABS):
        sl = slice(c, c + HY_EPILOGUE_SLABS)
        out = gate_ref[sl].astype(_f32) * (conv_scr[sl] + skip * z_ref[sl].astype(_f32))
        o_ref[sl] = out.astype(o_ref.dtype)


def _single(shape, index_map):
    return pl.BlockSpec(shape, index_map, pipeline_mode=pl.Buffered(1))


def _hyena_order(z_arr, z_col, gate_arr, gate_col, kf, order, skip, consts, out_dtype):
    fa_k, fai_k, fb, fbi, fbs, fbsi = consts
    b = z_arr.shape[0]
    tl = LANE_TILE
    d_out = kf.shape[-1]
    seq_block = (None, FFT_H, FFT_N2, tl)
    const2 = lambda shape: _single(shape, lambda j, i: (0, 0))
    const3 = lambda shape: _single(shape, lambda j, i: (0, 0, 0))
    return pl.pallas_call(
        _hyena_kernel,
        grid=(d_out // tl, b),
        in_specs=[pl.BlockSpec(seq_block, lambda j, i: (i, 0, 0, z_col + j)),
                  pl.BlockSpec(seq_block, lambda j, i: (i, 0, 0, gate_col + j)),
                  pl.BlockSpec((None, kf.shape[1], tl), lambda j, i: (order, 0, j)),
                  pl.BlockSpec((1, tl), lambda j, i: (0, j)),
                  const2(fa_k.shape), const2(fai_k.shape), const3(fb.shape), const3(fbi.shape),
                  const2(fbs.shape), const2(fbsi.shape)],
        out_specs=pl.BlockSpec(seq_block, lambda j, i: (i, 0, 0, j)),
        out_shape=jax.ShapeDtypeStruct((b, FFT_H, FFT_N2, d_out), out_dtype),
        scratch_shapes=[pltpu.VMEM((FFT_N1, FFT_N2, tl), _f32), pltpu.VMEM((FFT_H, FFT_N2, tl), _f32)],
        compiler_params=_params(),
        name="hyena_conv",
    )(z_arr, gate_arr, kf, skip, fa_k, fai_k, fb, fbi, fbs, fbsi)


def _filter_kernel(hdn_ref, w4f_ref, w4b_ref, delta_ref, fa_ref, fb_ref, fbs_ref, o_ref, sig_scr, w_scr):
    seq = hdn_ref.shape[0]
    tl = o_ref.shape[-1]
    half = FFT_N2
    def dot3(w):
        w_hi = w.astype(_bf16)
        w_lo = (w - w_hi.astype(_f32)).astype(_bf16)
        return _dot(hdn_ref[...], jnp.concatenate([w_hi, w_hi, w_lo], axis=0))

    row = lax.broadcasted_iota(jnp.int32, (seq, tl), 0)
    decay = jnp.exp(-(row.astype(_f32) * (1.0 / (seq - 1))) * delta_ref[...])
    f = dot3(w4f_ref[...]) * decay
    g = dot3(w4b_ref[...]) * decay
    g = jnp.where(row == 0, 0.0, g)
    norm = jnp.sum(jnp.abs(f), axis=0, keepdims=True) + jnp.sum(jnp.abs(g), axis=0, keepdims=True)
    inv_norm = 1.0 / norm
    f = f * inv_norm
    g = g * inv_norm

    for part, sig in enumerate((f + g, f - g)):
        sig_scr[...] = sig.reshape(FFT_H, FFT_N2, tl)

        def stage_a(gi, carry):
            r0 = pl.multiple_of(gi * SUBLANES, SUBLANES)
            zin = sig_scr[:, pl.ds(r0, SUBLANES), :].reshape(FFT_H * SUBLANES, tl)
            y = _dot(fa_ref[...], zin.astype(_bf16))
            w_scr[:, pl.ds(r0, SUBLANES), :] = y.reshape(FFT_N1, SUBLANES, tl)
            return carry

        lax.fori_loop(0, FFT_N2 // SUBLANES, stage_a, 0, unroll=HY_UNROLL_A)

        yin = jnp.concatenate([w_scr[0], w_scr[FFT_H]], axis=0).astype(_bf16)
        o_ref[pl.ds(part * half, half), :] = _dot(fbs_ref[pl.ds(part * half, half), :], yin)
        o_ref[pl.ds((2 + part) * half, half), :] = _dot(fbs_ref[pl.ds((2 + part) * half, half), :], yin)

        def stage_b(k1, carry):
            yin = jnp.concatenate([w_scr[k1], w_scr[FFT_H + k1]], axis=0).astype(_bf16)
            base = pl.multiple_of(2 * half * (k1 + 1) + part * half, half)
            o_ref[pl.ds(base, half), :] = _dot(fb_ref[k1 - 1, pl.ds(part * half, half), :], yin)
            return carry

        lax.fori_loop(1, FFT_H, stage_b, 0, unroll=HY_UNROLL_B)


def _filter_features(seq, hy_w1, hy_b1, hy_w2, hy_b2, hy_w3, hy_b3, hy_freq):
    t = jnp.linspace(0.0, 1.0, seq, dtype=_f32)[:, None]
    w = 2.0 * math.pi * jnp.arange(seq, dtype=_f32)[:, None] / seq
    f = jnp.linspace(1e-4, HY_BANDS - 1, HY_BANDS, dtype=_f32)[None, :]
    z = jnp.concatenate([t, jnp.cos(f * w), -jnp.sin(f * w)], axis=-1)
    hi = lax.Precision.HIGHEST
    hdn = jnp.sin(hy_freq * (jnp.dot(z, hy_w1, precision=hi) + hy_b1))
    hdn = jnp.sin(hy_freq * (jnp.dot(hdn, hy_w2, precision=hi) + hy_b2))
    hdn = jnp.sin(hy_freq * (jnp.dot(hdn, hy_w3, precision=hi) + hy_b3))
    hdn_hi = hdn.astype(_bf16)
    hdn_lo = (hdn - hdn_hi.astype(_f32)).astype(_bf16)
    return jnp.concatenate([hdn_hi, hdn_lo, hdn_hi], axis=1)


def _filter_spectra(hdn, hy_w4, consts):
    fa_k, _, fb, _, fbs, _ = consts
    width = hy_w4.shape[0]
    d = hy_w4.shape[1] // (2 * HY_ORDER)
    tl = LANE_TILE
    w4 = hy_w4.reshape(width, 2 * HY_ORDER, d).transpose(1, 0, 2)
    max_decay = math.log(HY_TARGET) / HY_FAST_DECAY
    min_decay = math.log(HY_TARGET) / HY_SLOW_DECAY
    deltas = jnp.abs(jnp.linspace(min_decay, max_decay, d, dtype=_f32)).reshape(1, d)
    n_rows = (FFT_H + 1) * 2 * FFT_N2
    const2 = lambda shape: _single(shape, lambda o, j: (0, 0))
    return pl.pallas_call(
        _filter_kernel,
        grid=(HY_ORDER, d // tl),
        in_specs=[const2(hdn.shape),
                  pl.BlockSpec((None, width, tl), lambda o, j: (2 * o, 0, j)),
                  pl.BlockSpec((None, width, tl), lambda o, j: (2 * o + 1, 0, j)),
                  pl.BlockSpec((1, tl), lambda o, j: (0, j)),
                  const2(fa_k.shape), _single(fb.shape, lambda o, j: (0, 0, 0)), const2(fbs.shape)],
        out_specs=pl.BlockSpec((None, n_rows, tl), lambda o, j: (o, 0, j)),
        out_shape=jax.ShapeDtypeStruct((HY_ORDER, n_rows, d), _f32),
        scratch_shapes=[pltpu.VMEM((FFT_H, FFT_N2, tl), _f32), pltpu.VMEM((FFT_N1, FFT_N2, tl), _f32)],
        compiler_params=_params(),
        name="filter_spectra",
    )(hdn, w4, w4, deltas, fa_k, fb, fbs)


def _merge_kernel(hx_ref, hsum_ref, yh_ref, x_ref, wrg_ref, brg_ref, wbg_ref, bbg_ref, wa_ref, wb_ref, wo_ref,
                  g1_ref, g_ref, sh_ref, sc_ref, x1_ref, h2_ref):
    d = x_ref.shape[-1]
    gates = _sigmoid(_dot(hx_ref[...], wbg_ref[...]) + bbg_ref[...])
    y_rnn = hsum_ref[...].astype(_f32) * jax.nn.gelu(_dot(hx_ref[...], wrg_ref[...]) + brg_ref[...])
    ya = _dot(y_rnn.astype(_bf16), wa_ref[...])
    yb = _dot(yh_ref[...].astype(_bf16), wb_ref[...])
    merged = gates[:, :d] * ya + gates[:, d:] * yb
    x1 = x_ref[...] + g1_ref[...] * _dot(merged.astype(_bf16), wo_ref[...])
    x1_ref[...] = x1
    y = x1 * lax.rsqrt(jnp.mean(x1 * x1, axis=-1, keepdims=True) + RMS_EPS) * g_ref[...]
    h2_ref[...] = (y * (1.0 + sc_ref[...]) + sh_ref[...]).astype(h2_ref.dtype)


def _merge(hx, h_sum, y_hy, x, w_in_bf, b_in, w_bg, b_bg, w_a, w_b, w_o, g1, norm2_g, sh2, sc2, tm=512):
    b, l, d = x.shape
    row = lambda width: pl.BlockSpec((None, tm, width), lambda i, j: (i, j, 0))
    vec = pl.BlockSpec((None, 1, d), lambda i, j: (i, 0, 0))
    full = lambda a: pl.BlockSpec(a.shape, lambda i, j: (0,) * a.ndim)
    b_in = b_in.reshape(1, -1)
    b_bg = b_bg.reshape(1, -1)
    norm2_g = norm2_g.reshape(1, d)
    return pl.pallas_call(
        _merge_kernel,
        grid=(b, l // tm),
        in_specs=[row(d), row(d), row(d), row(d),
                  pl.BlockSpec((d, d), lambda i, j: (0, 1)), pl.BlockSpec((1, d), lambda i, j: (0, 1)),
                  full(w_bg), full(b_bg), full(w_a), full(w_b), full(w_o), vec, full(norm2_g), vec, vec],
        out_specs=[row(d), row(d)],
        out_shape=[jax.ShapeDtypeStruct((b, l, d), _f32), jax.ShapeDtypeStruct((b, l, d), _bf16)],
        compiler_params=_params(),
        name="merge",
    )(hx, h_sum, y_hy, x, w_in_bf, b_in, w_bg, b_bg, w_a, w_b, w_o, g1, norm2_g, sh2, sc2)


def _ffn_kernel(h_ref, x1_ref, wg_ref, wu_ref, wo_ref, g2_ref, fg_ref, o_ref):
    h = h_ref[...]
    g = _dot(h, wg_ref[...])
    u = _dot(h, wu_ref[...])
    a = (g * _sigmoid(g) * u).astype(_bf16)
    x2 = x1_ref[...] + g2_ref[...] * _dot(a, wo_ref[...])
    y = x2 * lax.rsqrt(jnp.mean(x2 * x2, axis=-1, keepdims=True) + RMS_EPS)
    o_ref[...] = y * fg_ref[...]


def _ffn(h2, x1, w_gu, w_o, g2, final_g, tm=512):
    b, l, d = x1.shape
    d_ff = w_o.shape[0]
    row = pl.BlockSpec((None, tm, d), lambda i, j: (i, j, 0))
    vec = pl.BlockSpec((None, 1, d), lambda i, j: (i, 0, 0))
    full = lambda a: _single(a.shape, lambda i, j: (0,) * a.ndim)
    final_g = final_g.reshape(1, d)
    return pl.pallas_call(
        _ffn_kernel,
        grid=(b, l // tm),
        in_specs=[row, row, _single((d, d_ff), lambda i, j: (0, 0)), _single((d, d_ff), lambda i, j: (0, 1)),
                  full(w_o), vec, full(final_g)],
        out_specs=row,
        out_shape=jax.ShapeDtypeStruct((b, l, d), _f32),
        compiler_params=_params(),
        name="ffn",
    )(h2, x1, w_gu, w_gu, w_o, g2, final_g)


def kernel(x, c, ctx, c_ctx, w_mod, b_mod, norm1_g, norm2_g, w_in, b_in, rnn_conv_w, rnn_conv_b, rg_wa, rg_ba,
           rg_wx, rg_bx, rg_lambda, hy_conv_w, hy_conv_b, hy_w1, hy_b1, hy_w2, hy_b2, hy_w3, hy_b3, hy_freq,
           hy_w4, hy_skip, w_a_out, w_b_out, w_out, w_ffn_in, w_ffn_out, final_g):
    assert w_mod.shape[0] == 1, "single-layer block"
    b, seq, d = x.shape
    ctx_len = ctx.shape[1]
    assert seq == FFT_H * FFT_N2 and d == D_MODEL
    assert b == SUBLANES, "time-major rows put the batch on the sublanes of one register"
    (w_mod, b_mod, norm1_g, norm2_g, w_in, b_in, rnn_conv_w, rnn_conv_b, rg_wa, rg_ba, rg_wx, rg_bx,
     rg_lambda, hy_conv_w, hy_conv_b, hy_w1, hy_b1, hy_w2, hy_b2, hy_w3, hy_b3, hy_freq, hy_w4, hy_skip,
     w_a_out, w_b_out, w_out, w_ffn_in, w_ffn_out) = [
        a[0] for a in (w_mod, b_mod, norm1_g, norm2_g, w_in, b_in, rnn_conv_w, rnn_conv_b, rg_wa, rg_ba,
                       rg_wx, rg_bx, rg_lambda, hy_conv_w, hy_conv_b, hy_w1, hy_b1, hy_w2, hy_b2, hy_w3,
                       hy_b3, hy_freq, hy_w4, hy_skip, w_a_out, w_b_out, w_out, w_ffn_in, w_ffn_out)]

    pad_rows = 2 * SUBLANES - b - 1
    c_all = jnp.concatenate([c, c_ctx[None, :], jnp.zeros((pad_rows, d), _f32)], axis=0)
    mod = _mod_vectors(c_all, w_mod, b_mod)
    sh1, sc1, g1, sh2, sc2, g2 = [m.reshape(b, 1, d) for m in jnp.split(mod[:b], N_MOD, axis=-1)]
    csh1, csc1 = [jnp.broadcast_to(m.reshape(1, 1, d), (b, 1, d))
                  for m in jnp.split(mod[b], N_MOD, axis=-1)[:2]]

    w_in_bf = w_in.astype(_bf16)
    w_rx, b_rx = w_in_bf[:, :d], b_in[:d]
    w_bg, b_bg = w_in_bf[:, 5 * d:], b_in[5 * d:]
    gate_w, gate_bias = zip(*[_gate_weights(rg_wa[i], rg_wx[i], rg_ba[i], rg_bx[i]) for i in range(2)])
    gate_w, gate_bias = jnp.stack(gate_w), jnp.stack(gate_bias)
    lam = rg_lambda.reshape(2, 1, d)
    zeros_state = jnp.zeros((b, d), _f32)

    hc_tb = _norm_mod(ctx, norm1_g, csh1, csc1)
    u_c = _proj_conv(hc_tb, w_rx, b_rx, rnn_conv_w, rnn_conv_b, pad_left=RNN_CONV_PAD_LEFT, period=ctx_len,
                     row_stride=b, tm=ctx_len * b, out_dtype=_bf16)
    _, cf, cb = _rglru(u_c, b, gate_w, gate_bias, lam, zeros_state, zeros_state)

    hx, u, q = _inproj(x, norm1_g, sh1, sc1, w_in_bf, b_in, rnn_conv_w, rnn_conv_b, hy_conv_w, hy_conv_b, GRID_W)
    h_sum, _, _ = _rglru(u, b, gate_w, gate_bias, lam, cf, cb)

    consts = [jnp.asarray(a).astype(_bf16) for a in _dft_constants()]
    hdn = _filter_features(seq, hy_w1, hy_b1, hy_w2, hy_b2, hy_w3, hy_b3, hy_freq)
    kf = _filter_spectra(hdn, hy_w4, consts)
    q4 = q.reshape(b, FFT_H, FFT_N2, 3 * d)
    n_tiles = d // LANE_TILE
    z1 = _hyena_order(q4, 0, q4, n_tiles, kf, 0, hy_skip[0:1], consts, _bf16)
    y_hy = _hyena_order(z1, 0, q4, 2 * n_tiles, kf, 1, hy_skip[1:2], consts, _bf16)
    y_hy = y_hy.reshape(b, seq, d)

    x1, h2 = _merge(hx, h_sum, y_hy, x, w_in_bf, b_in, w_bg, b_bg, w_a_out.astype(_bf16), w_b_out.astype(_bf16),
                    w_out.astype(_bf16), g1, norm2_g, sh2, sc2)
    w_ffn_bf = w_ffn_in.astype(_bf16)
    return _ffn(h2, x1, w_ffn_bf, w_ffn_out.astype(_bf16), g2, final_g)
```

```python
import functools
import math

import numpy as np
import jax
import jax.numpy as jnp
from jax import lax
from jax.experimental import pallas as pl
from jax.experimental.pallas import tpu as pltpu

D_MODEL = 1024
GRID_W = 64
N_MOD = 6
RMS_EPS = 1e-6
RNN_HEADS = 16
RNN_HEAD_DIM = D_MODEL // RNN_HEADS
RNN_CONV_PAD_LEFT = 2
HY_CONV_PAD_LEFT = 1
RG_C = 8.0
HY_ORDER = 2
HY_EMB = 33
HY_BANDS = (HY_EMB - 1) // 2
HY_FAST_DECAY = 0.3
HY_SLOW_DECAY = 1.5
HY_TARGET = 1e-2
D_FF = ((8 * D_MODEL // 3 + 255) // 256) * 256

SUBLANES = 8
LANE_TILE = 256
VMEM_LIMIT = 56 * 1024 * 1024

FFT_N1 = 64
FFT_N2 = 128
FFT_N = FFT_N1 * FFT_N2
FFT_H = FFT_N1 // 2

_f32 = jnp.float32
_bf16 = jnp.bfloat16


def _params(**kw):
    return pltpu.CompilerParams(vmem_limit_bytes=VMEM_LIMIT, **kw)


def _dot(a, b):
    return jnp.dot(a, b, preferred_element_type=_f32)


def _sigmoid(x):
    return 0.5 * jnp.tanh(0.5 * x) + 0.5


def _mod_kernel(c_ref, w_ref, b_ref, o_ref):
    c = c_ref[...]
    s = c * _sigmoid(c)
    s_hi = s.astype(_bf16)
    s_lo = (s - s_hi.astype(_f32)).astype(_bf16)
    w = w_ref[...]
    w_hi = w.astype(_bf16)
    w_lo = (w - w_hi.astype(_f32)).astype(_bf16)
    o_ref[...] = _dot(jnp.concatenate([s_hi, s_lo, s_hi], axis=1),
                      jnp.concatenate([w_hi, w_hi, w_lo], axis=0)) + b_ref[...]


def _mod_vectors(c_all, w_mod, b_mod):
    rows, d = c_all.shape
    n = w_mod.shape[1]
    tn = 1024
    return pl.pallas_call(
        _mod_kernel,
        grid=(n // tn,),
        in_specs=[pl.BlockSpec((rows, d), lambda j: (0, 0)),
                  pl.BlockSpec((d, tn), lambda j: (0, j)),
                  pl.BlockSpec((1, tn), lambda j: (0, j))],
        out_specs=pl.BlockSpec((rows, tn), lambda j: (0, j)),
        out_shape=jax.ShapeDtypeStruct((rows, n), _f32),
        compiler_params=_params(),
        name="mod_vectors",
    )(c_all, w_mod, b_mod.reshape(1, n))


LANES = 128
NORM_TT = 128


def _norm_mod_kernel(x_ref, g_ref, sh_ref, sc_ref, otb_ref, tb_scr):
    nb, tt, d = x_ref.shape
    for b in range(nb):
        x = x_ref[b]
        gain = g_ref[...] * (1.0 + sc_ref[b])
        y = x * lax.rsqrt(jnp.mean(x * x, axis=-1, keepdims=True) + RMS_EPS) * gain + sh_ref[b]
        for s in range(d // LANES):
            tb_scr.at[s][pl.ds(b, tt, stride=nb), :] = y[:, s * LANES:(s + 1) * LANES]
    for s in range(d // LANES):
        otb_ref[:, s * LANES:(s + 1) * LANES] = tb_scr[s].astype(otb_ref.dtype)


def _norm_mod(x, g, sh, sc):
    b, l, d = x.shape
    tt = NORM_TT
    vec = pl.BlockSpec((b, 1, d), lambda i: (0, 0, 0))
    return pl.pallas_call(
        _norm_mod_kernel,
        grid=(l // tt,),
        in_specs=[pl.BlockSpec((b, tt, d), lambda i: (0, i, 0)),
                  pl.BlockSpec((1, d), lambda i: (0, 0)), vec, vec],
        out_specs=pl.BlockSpec((tt * b, d), lambda i: (i, 0)),
        out_shape=jax.ShapeDtypeStruct((l * b, d), _bf16),
        scratch_shapes=[pltpu.VMEM((d // LANES, tt * b, LANES), _f32)],
        compiler_params=_params(),
        name="norm_mod",
    )(x, g.reshape(1, d), sh, sc)


CONV_CHUNK_STEPS = 8


def _short_conv_period(y, w, cb, pad_left, nb, store):
    rows, lanes = y.shape
    n_taps = w.shape[0]
    ypad = jnp.concatenate([jnp.zeros((pad_left * nb, lanes), y.dtype), y,
                            jnp.zeros(((n_taps - 1 - pad_left) * nb, lanes), y.dtype)], axis=0)
    chunk = CONV_CHUNK_STEPS * nb
    for r0 in range(0, rows, chunk):
        out = cb
        for k in range(n_taps):
            out = out + ypad[r0 + k * nb:r0 + k * nb + chunk] * w[k:k + 1, :]
        store(r0, r0 + chunk, out)


def _proj_conv_kernel(x_ref, w_ref, b_ref, cw_ref, cb_ref, o_ref, *, pad_left, period, row_stride):
    tm, tn = o_ref.shape
    prow = period * row_stride
    for j in range(tn // LANE_TILE):
        cols = slice(j * LANE_TILE, (j + 1) * LANE_TILE)
        y = _dot(x_ref[...], w_ref[:, cols]) + b_ref[:, cols]
        for p in range(tm // prow):
            def store(r0, r1, val, base=p * prow, cols=cols):
                o_ref[base + r0:base + r1, cols] = val.astype(o_ref.dtype)

            _short_conv_period(y[p * prow:(p + 1) * prow], cw_ref[:, cols], cb_ref[:, cols], pad_left,
                               row_stride, store)


def _proj_conv(x2d, w, bias, conv_w, conv_b, *, pad_left, period, row_stride, tm, tn=1024, out_dtype=_f32):
    assert tm % (period * row_stride) == 0
    m, k = x2d.shape
    n = w.shape[1]
    col = lambda rows: pl.BlockSpec((rows, tn), lambda j, i: (0, j))
    return pl.pallas_call(
        functools.partial(_proj_conv_kernel, pad_left=pad_left, period=period, row_stride=row_stride),
        grid=(n // tn, m // tm),
        in_specs=[pl.BlockSpec((tm, k), lambda j, i: (i, 0)), col(k), col(1), col(conv_w.shape[0]), col(1)],
        out_specs=pl.BlockSpec((tm, tn), lambda j, i: (i, j)),
        out_shape=jax.ShapeDtypeStruct((m, n), out_dtype),
        compiler_params=_params(),
        name="proj_conv",
    )(x2d, w, bias.reshape(1, n), conv_w, conv_b.reshape(1, n))


def _inproj_kernel(x_ref, g_ref, sh_ref, sc_ref, w_ref, b_ref, rcw_ref, rcb_ref, hcw_ref, hcb_ref,
                   hx_ref, u_ref, q_ref, xtb_scr, lhs_scr, qtb_scr, *, rnn_pad, hy_pad, hy_col0):
    nb, tt, d = x_ref.shape
    for b in range(nb):
        x = x_ref[b]
        gain = g_ref[...] * (1.0 + sc_ref[b])
        y = x * lax.rsqrt(jnp.mean(x * x, axis=-1, keepdims=True) + RMS_EPS) * gain + sh_ref[b]
        hx_ref[b] = y.astype(hx_ref.dtype)
        for s in range(d // LANES):
            xtb_scr.at[s][pl.ds(b, tt, stride=nb), :] = y[:, s * LANES:(s + 1) * LANES]
    for s in range(d // LANES):
        lhs_scr[:, s * LANES:(s + 1) * LANES] = xtb_scr[s].astype(lhs_scr.dtype)

    n_rx = u_ref.shape[1] // LANE_TILE
    n_hy = q_ref.shape[2] // LANE_TILE
    slabs = LANE_TILE // LANES
    for j in range(n_rx + n_hy):
        c0 = j * LANE_TILE if j < n_rx else hy_col0 + (j - n_rx) * LANE_TILE
        cols = slice(c0, c0 + LANE_TILE)
        y = _dot(lhs_scr[...], w_ref[:, cols]) + b_ref[:, cols]
        if j < n_rx:
            def store_u(r0, r1, val, cols=cols):
                u_ref[r0:r1, cols] = val.astype(u_ref.dtype)

            _short_conv_period(y, rcw_ref[:, cols], rcb_ref[:, cols], rnn_pad, nb, store_u)
        else:
            jh = j - n_rx
            oc = slice(jh * LANE_TILE, (jh + 1) * LANE_TILE)

            def store_q(r0, r1, val, jh=jh):
                for s in range(slabs):
                    qtb_scr[jh * slabs + s, r0:r1, :] = val[:, s * LANES:(s + 1) * LANES]

            _short_conv_period(y, hcw_ref[:, oc], hcb_ref[:, oc], hy_pad, nb, store_q)
            for s in range(slabs):
                slab = jh * slabs + s
                for bi in range(nb):
                    piece = qtb_scr.at[slab][pl.ds(bi, tt, stride=nb), :]
                    q_ref[bi, :, slab * LANES:(slab + 1) * LANES] = piece.astype(q_ref.dtype)


def _inproj(x, g, sh, sc, w_in_bf, b_in, rnn_conv_w, rnn_conv_b, hy_conv_w, hy_conv_b, period):
    b, l, d = x.shape
    n_all = w_in_bf.shape[1]
    n_hy = hy_conv_w.shape[1]
    tt = period
    vec = pl.BlockSpec((b, 1, d), lambda i: (0, 0, 0))
    full = lambda a: _single(a.shape, lambda i: (0,) * a.ndim)
    g2, b2 = g.reshape(1, d), b_in.reshape(1, n_all)
    rcb, hcb = rnn_conv_b.reshape(1, d), hy_conv_b.reshape(1, n_hy)
    return pl.pallas_call(
        functools.partial(_inproj_kernel, rnn_pad=RNN_CONV_PAD_LEFT, hy_pad=HY_CONV_PAD_LEFT, hy_col0=2 * d),
        grid=(l // tt,),
        in_specs=[pl.BlockSpec((b, tt, d), lambda i: (0, i, 0)), full(g2), vec, vec,
                  full(w_in_bf), full(b2), full(rnn_conv_w), full(rcb), full(hy_conv_w), full(hcb)],
        out_specs=[pl.BlockSpec((b, tt, d), lambda i: (0, i, 0)), pl.BlockSpec((tt * b, d), lambda i: (i, 0)),
                   pl.BlockSpec((b, tt, n_hy), lambda i: (0, i, 0))],
        out_shape=[jax.ShapeDtypeStruct((b, l, d), _bf16), jax.ShapeDtypeStruct((l * b, d), _bf16),
                   jax.ShapeDtypeStruct((b, l, n_hy), _bf16)],
        scratch_shapes=[pltpu.VMEM((d // LANES, tt * b, LANES), _f32),
                        pltpu.VMEM((tt * b, d), _bf16),
                        pltpu.VMEM((n_hy // LANES, tt * b, LANES), _f32)],
        compiler_params=_params(),
        name="inproj",
    )(x, g2, sh, sc, w_in_bf, b2, rnn_conv_w, rcb, hy_conv_w, hcb)


RNN_TT = 512
RNN_SUB_TT = 64


def _rnn_coeffs(u, w_ref, bias_ref, lam_ref):
    tl = u.shape[-1]
    half_decay = (-0.5 * RG_C) * jax.nn.softplus(-lam_ref[...])
    g = _dot(u.astype(_bf16), w_ref[...]) + bias_ref[...]
    log_a = half_decay * jnp.tanh(g[:, :tl]) + half_decay
    i = 0.5 * jnp.tanh(g[:, tl:]) + 0.5
    a = jnp.exp(log_a)
    x = -jnp.tanh(log_a) * (a * a + 1.0)
    root = jnp.where(x > 0.0, x * lax.rsqrt(x), 0.0)
    return a, root * (i * u.astype(_f32))


def _rnn_scan(u_ref, w_ref, bias_ref, lam_ref, a_scr, b_scr, h_ref, h0, reverse):
    nb = h0.shape[0]
    rows = u_ref.shape[0]
    sub = RNN_SUB_TT * nb
    n_sub = rows // sub
    order = list(range(n_sub))[::-1] if reverse else list(range(n_sub))

    def coeffs(k):
        sl = slice(k * sub, (k + 1) * sub)
        a, b = _rnn_coeffs(u_ref[sl, :], w_ref, bias_ref, lam_ref)
        a_scr[sl, :] = a
        b_scr[sl, :] = b

    coeffs(order[0])
    h = h0
    for pos, k in enumerate(order):
        if pos + 1 < n_sub:
            coeffs(order[pos + 1])
        steps = list(range(k * sub, (k + 1) * sub, nb))
        for r0 in (steps[::-1] if reverse else steps):
            h = a_scr[r0:r0 + nb, :] * h + b_scr[r0:r0 + nb, :]
            h_ref[r0:r0 + nb, :] = h
    return h


def _rnn_fwd_kernel(u_ref, w_ref, bias_ref, lam_ref, h0_ref, hf_ref, hlast_ref, a_scr, b_scr, h_scr):
    @pl.when(pl.program_id(1) == 0)
    def _():
        h_scr[...] = h0_ref[...]

    h = _rnn_scan(u_ref, w_ref, bias_ref, lam_ref, a_scr, b_scr, hf_ref, h_scr[...], False)
    h_scr[...] = h
    hlast_ref[...] = h


def _rnn_bwd_kernel(u_ref, hf_ref, w_ref, bias_ref, lam_ref, h0_ref, y_ref, hfirst_ref,
                    a_scr, b_scr, h_scr, y_scr):
    @pl.when(pl.program_id(1) == 0)
    def _():
        h_scr[...] = h0_ref[...]

    nb, tt, tl = y_ref.shape
    h = _rnn_scan(u_ref, w_ref, bias_ref, lam_ref, a_scr, b_scr, b_scr, h_scr[...], True)
    h_scr[...] = h
    hfirst_ref[...] = h
    y = hf_ref[...] + b_scr[...]
    for s in range(tl // LANES):
        y_scr[s] = y[:, s * LANES:(s + 1) * LANES]
    for bi in range(nb):
        for s in range(tl // LANES):
            piece = y_scr.at[s][pl.ds(bi, tt, stride=nb), :]
            y_ref[bi, :, s * LANES:(s + 1) * LANES] = piece.astype(y_ref.dtype)


def _rglru(u_tb, nb, gate_w, gate_bias, lam, h0f, h0b):
    rows, d = u_tb.shape
    seq = rows // nb
    tl = LANE_TILE
    tt = min(RNN_TT, seq)
    n_chunks = seq // tt
    chunk = tt * nb
    state = pl.BlockSpec((nb, tl), lambda j, c: (0, j))
    scratch = [pltpu.VMEM((chunk, tl), _f32), pltpu.VMEM((chunk, tl), _f32), pltpu.VMEM((nb, tl), _f32)]

    def param_specs(direction):
        return [pl.BlockSpec((None, None, tl, 2 * tl), lambda j, c: (direction, j, 0, 0)),
                pl.BlockSpec((None, None, 1, 2 * tl), lambda j, c: (direction, j, 0, 0)),
                pl.BlockSpec((None, 1, tl), lambda j, c: (direction, 0, j))]

    fwd_rows = pl.BlockSpec((chunk, tl), lambda j, c: (c, j))
    hf_tb, hf_last = pl.pallas_call(
        _rnn_fwd_kernel,
        grid=(d // tl, n_chunks),
        in_specs=[fwd_rows] + param_specs(0) + [state],
        out_specs=[fwd_rows, state],
        out_shape=[jax.ShapeDtypeStruct((rows, d), _f32), jax.ShapeDtypeStruct((nb, d), _f32)],
        scratch_shapes=scratch,
        compiler_params=_params(),
        name="rglru_fwd",
    )(u_tb, gate_w, gate_bias, lam, h0f)

    bwd_rows = pl.BlockSpec((chunk, tl), lambda j, c: (n_chunks - 1 - c, j))
    y, hb_first = pl.pallas_call(
        _rnn_bwd_kernel,
        grid=(d // tl, n_chunks),
        in_specs=[bwd_rows, bwd_rows] + param_specs(1) + [state],
        out_specs=[pl.BlockSpec((nb, tt, tl), lambda j, c: (0, n_chunks - 1 - c, j)), state],
        out_shape=[jax.ShapeDtypeStruct((nb, seq, d), _bf16), jax.ShapeDtypeStruct((nb, d), _f32)],
        scratch_shapes=scratch + [pltpu.VMEM((tl // LANES, chunk, LANES), _f32)],
        compiler_params=_params(),
        name="rglru_bwd",
    )(u_tb, hf_tb, gate_w, gate_bias, lam, h0b)
    return y, hf_last, hb_first


def _gate_weights(wa, wx, ba, bx):
    heads_per_tile = LANE_TILE // RNN_HEAD_DIM
    n_tiles = RNN_HEADS // heads_per_tile

    def tile_blockdiag(w):
        w = w.reshape(n_tiles, heads_per_tile, RNN_HEAD_DIM, RNN_HEAD_DIM)
        eye = jnp.eye(heads_per_tile, dtype=w.dtype)
        full = jnp.einsum('thij,hg->thigj', w, eye)
        return full.reshape(n_tiles, LANE_TILE, LANE_TILE)

    w = (0.5 * jnp.concatenate([tile_blockdiag(wa), tile_blockdiag(wx)], axis=-1)).astype(_bf16)
    bias = jnp.concatenate([ba.reshape(n_tiles, 1, LANE_TILE), bx.reshape(n_tiles, 1, LANE_TILE)], axis=-1)
    return w, (0.5 * bias).astype(_f32)


HY_UNROLL_A = 16
HY_UNROLL_B = 31
HY_EPILOGUE_SLABS = 4


def _dft_constants():
    n1 = np.arange(FFT_H)
    k1 = np.arange(FFT_H + 1)
    ang = 2.0 * np.pi * np.outer(k1, n1) / FFT_N1
    fa = np.concatenate([np.cos(ang), -np.sin(ang)[1:FFT_H]], axis=0)
    fa_k = np.kron(fa, np.eye(SUBLANES))
    weight = np.where((k1 == 0) | (k1 == FFT_H), 1.0, 2.0)[:, None] / FFT_N
    fai = np.concatenate([weight * np.cos(ang), (-2.0 / FFT_N) * np.sin(ang)[1:FFT_H]], axis=0).T
    fai_k = np.kron(fai, np.eye(SUBLANES))

    n2 = np.arange(FFT_N2)
    k2 = np.arange(FFT_N2)

    def cs(k1v):
        idx = (np.outer(k2, n2) * FFT_N1 + k1v * n2[None, :]) % FFT_N
        phi = 2.0 * np.pi * idx / FFT_N
        return np.cos(phi), np.sin(phi)

    fb, fbi = [], []
    for k1v in range(1, FFT_H):
        c, s = cs(k1v)
        fb.append(np.block([[c, s], [-s, c]]))
        fbi.append(np.block([[c.T, -s.T], [s.T, c.T]]))
    c0, s0 = cs(0)
    ch, sh = cs(FFT_H)
    z = np.zeros_like(c0)
    fbs = np.block([[c0, z], [-s0, z], [z, ch], [z, -sh]])
    fbsi = np.block([[c0.T, -s0.T, z, z], [z, z, ch.T, -sh.T]])
    as32 = lambda a: np.asarray(a, np.float32)
    return as32(fa_k), as32(fai_k), as32(np.stack(fb)), as32(np.stack(fbi)), as32(fbs), as32(fbsi)


def _hyena_kernel(z_ref, gate_ref, kf_ref, skip_ref, fa_ref, fai_ref, fb_ref, fbi_ref, fbs_ref, fbsi_ref,
                  o_ref, w_scr, conv_scr):
    tl = z_ref.shape[-1]
    half = FFT_N2
    pack = 2 * SUBLANES

    for p in range(FFT_N2 // pack):
        zz = z_ref[:, p * pack:(p + 1) * pack, :].astype(_f32)
        for hf in range(2):
            r0 = p * pack + hf * SUBLANES
            zin = zz[:, hf * SUBLANES:(hf + 1) * SUBLANES, :].reshape(FFT_H * SUBLANES, tl)
            y = _dot(fa_ref[...], zin.astype(_bf16))
            w_scr[:, r0:r0 + SUBLANES, :] = y.reshape(FFT_N1, SUBLANES, tl)

    yin = jnp.concatenate([w_scr[0], w_scr[FFT_H]], axis=0).astype(_bf16)
    x = _dot(fbs_ref[...], yin)
    pieces = []
    for q in range(2):
        xr = x[(2 * q) * half:(2 * q + 1) * half]
        xi = x[(2 * q + 1) * half:(2 * q + 2) * half]
        kr = kf_ref[pl.ds((2 * q) * half, half), :]
        ki = kf_ref[pl.ds((2 * q + 1) * half, half), :]
        pieces += [xr * kr - xi * ki, xr * ki + xi * kr]
    v = _dot(fbsi_ref[...], jnp.concatenate(pieces, axis=0).astype(_bf16))
    w_scr[0] = v[:half]
    w_scr[FFT_H] = v[half:]

    def stage_b(k1, carry):
        yin = jnp.concatenate([w_scr[k1], w_scr[FFT_H + k1]], axis=0).astype(_bf16)
        x = _dot(fb_ref[k1 - 1], yin)
        xr, xi = x[:half], x[half:]
        base = pl.multiple_of(2 * half * (k1 + 1), 2 * half)
        kr = kf_ref[pl.ds(base, half), :]
        ki = kf_ref[pl.ds(base + half, half), :]
        p = jnp.concatenate([xr * kr - xi * ki, xr * ki + xi * kr], axis=0).astype(_bf16)
        v = _dot(fbi_ref[k1 - 1], p)
        w_scr[k1] = v[:half]
        w_scr[FFT_H + k1] = v[half:]
        return carry

    lax.fori_loop(1, FFT_H, stage_b, 0, unroll=HY_UNROLL_B)

    for g in range(FFT_N2 // SUBLANES):
        r0 = g * SUBLANES
        vin = w_scr[:, r0:r0 + SUBLANES, :].reshape(FFT_N1 * SUBLANES, tl)
        conv_scr[:, r0:r0 + SUBLANES, :] = _dot(fai_ref[...], vin.astype(_bf16)).reshape(FFT_H, SUBLANES, tl)

    skip = skip_ref[...]
    for c in range(0, FFT_H, HY_EPILOGUE_SLABS):
        sl = slice(c, c + HY_EPILOGUE_SLABS)
        out = gate_ref[sl].astype(_f32) * (conv_scr[sl] + skip * z_ref[sl].astype(_f32))
        o_ref[sl] = out.astype(o_ref.dtype)


def _single(shape, index_map):
    return pl.BlockSpec(shape, index_map, pipeline_mode=pl.Buffered(1))


def _hyena_order(z_arr, z_col, gate_arr, gate_col, kf, order, skip, consts, out_dtype):
    fa_k, fai_k, fb, fbi, fbs, fbsi = consts
    b = z_arr.shape[0]
    tl = LANE_TILE
    d_out = kf.shape[-1]
    seq_block = (None, FFT_H, FFT_N2, tl)
    const2 = lambda shape: _single(shape, lambda j, i: (0, 0))
    const3 = lambda shape: _single(shape, lambda j, i: (0, 0, 0))
    return pl.pallas_call(
        _hyena_kernel,
        grid=(d_out // tl, b),
        in_specs=[pl.BlockSpec(seq_block, lambda j, i: (i, 0, 0, z_col + j)),
                  pl.BlockSpec(seq_block, lambda j, i: (i, 0, 0, gate_col + j)),
                  pl.BlockSpec((None, kf.shape[1], tl), lambda j, i: (order, 0, j)),
                  pl.BlockSpec((1, tl), lambda j, i: (0, j)),
                  const2(fa_k.shape), const2(fai_k.shape), const3(fb.shape), const3(fbi.shape),
                  const2(fbs.shape), const2(fbsi.shape)],
        out_specs=pl.BlockSpec(seq_block, lambda j, i: (i, 0, 0, j)),
        out_shape=jax.ShapeDtypeStruct((b, FFT_H, FFT_N2, d_out), out_dtype),
        scratch_shapes=[pltpu.VMEM((FFT_N1, FFT_N2, tl), _f32), pltpu.VMEM((FFT_H, FFT_N2, tl), _f32)],
        compiler_params=_params(),
        name="hyena_conv",
    )(z_arr, gate_arr, kf, skip, fa_k, fai_k, fb, fbi, fbs, fbsi)


def _filter_kernel(hdn_ref, w4f_ref, w4b_ref, delta_ref, fa_ref, fb_ref, fbs_ref, o_ref, sig_scr, w_scr):
    seq = hdn_ref.shape[0]
    tl = o_ref.shape[-1]
    half = FFT_N2
    def dot3(w):
        w_hi = w.astype(_bf16)
        w_lo = (w - w_hi.astype(_f32)).astype(_bf16)
        return _dot(hdn_ref[...], jnp.concatenate([w_hi, w_hi, w_lo], axis=0))

    row = lax.broadcasted_iota(jnp.int32, (seq, tl), 0)
    decay = jnp.exp(-(row.astype(_f32) * (1.0 / (seq - 1))) * delta_ref[...])
    f = dot3(w4f_ref[...]) * decay
    g = dot3(w4b_ref[...]) * decay
    g = jnp.where(row == 0, 0.0, g)
    norm = jnp.sum(jnp.abs(f), axis=0, keepdims=True) + jnp.sum(jnp.abs(g), axis=0, keepdims=True)
    inv_norm = 1.0 / norm
    f = f * inv_norm
    g = g * inv_norm

    for part, sig in enumerate((f + g, f - g)):
        sig_scr[...] = sig.reshape(FFT_H, FFT_N2, tl)

        def stage_a(gi, carry):
            r0 = pl.multiple_of(gi * SUBLANES, SUBLANES)
            zin = sig_scr[:, pl.ds(r0, SUBLANES), :].reshape(FFT_H * SUBLANES, tl)
            y = _dot(fa_ref[...], zin.astype(_bf16))
            w_scr[:, pl.ds(r0, SUBLANES), :] = y.reshape(FFT_N1, SUBLANES, tl)
            return carry

        lax.fori_loop(0, FFT_N2 // SUBLANES, stage_a, 0, unroll=HY_UNROLL_A)

        yin = jnp.concatenate([w_scr[0], w_scr[FFT_H]], axis=0).astype(_bf16)
        o_ref[pl.ds(part * half, half), :] = _dot(fbs_ref[pl.ds(part * half, half), :], yin)
        o_ref[pl.ds((2 + part) * half, half), :] = _dot(fbs_ref[pl.ds((2 + part) * half, half), :], yin)

        def stage_b(k1, carry):
            yin = jnp.concatenate([w_scr[k1], w_scr[FFT_H + k1]], axis=0).astype(_bf16)
            base = pl.multiple_of(2 * half * (k1 + 1) + part * half, half)
            o_ref[pl.ds(base, half), :] = _dot(fb_ref[k1 - 1, pl.ds(part * half, half), :], yin)
            return carry

        lax.fori_loop(1, FFT_H, stage_b, 0, unroll=HY_UNROLL_B)


def _filter_features(seq, hy_w1, hy_b1, hy_w2, hy_b2, hy_w3, hy_b3, hy_freq):
    t = jnp.linspace(0.0, 1.0, seq, dtype=_f32)[:, None]
    w = 2.0 * math.pi * jnp.arange(seq, dtype=_f32)[:, None] / seq
    f = jnp.linspace(1e-4, HY_BANDS - 1, HY_BANDS, dtype=_f32)[None, :]
    z = jnp.concatenate([t, jnp.cos(f * w), -jnp.sin(f * w)], axis=-1)
    hi = lax.Precision.HIGHEST
    hdn = jnp.sin(hy_freq * (jnp.dot(z, hy_w1, precision=hi) + hy_b1))
    hdn = jnp.sin(hy_freq * (jnp.dot(hdn, hy_w2, precision=hi) + hy_b2))
    hdn = jnp.sin(hy_freq * (jnp.dot(hdn, hy_w3, precision=hi) + hy_b3))
    hdn_hi = hdn.astype(_bf16)
    hdn_lo = (hdn - hdn_hi.astype(_f32)).astype(_bf16)
    return jnp.concatenate([hdn_hi, hdn_lo, hdn_hi], axis=1)


def _filter_spectra(hdn, hy_w4, consts):
    fa_k, _, fb, _, fbs, _ = consts
    width = hy_w4.shape[0]
    d = hy_w4.shape[1] // (2 * HY_ORDER)
    tl = LANE_TILE
    w4 = hy_w4.reshape(width, 2 * HY_ORDER, d).transpose(1, 0, 2)
    max_decay = math.log(HY_TARGET) / HY_FAST_DECAY
    min_decay = math.log(HY_TARGET) / HY_SLOW_DECAY
    deltas = jnp.abs(jnp.linspace(min_decay, max_decay, d, dtype=_f32)).reshape(1, d)
    n_rows = (FFT_H + 1) * 2 * FFT_N2
    const2 = lambda shape: _single(shape, lambda o, j: (0, 0))
    return pl.pallas_call(
        _filter_kernel,
        grid=(HY_ORDER, d // tl),
        in_specs=[const2(hdn.shape),
                  pl.BlockSpec((None, width, tl), lambda o, j: (2 * o, 0, j)),
                  pl.BlockSpec((None, width, tl), lambda o, j: (2 * o + 1, 0, j)),
                  pl.BlockSpec((1, tl), lambda o, j: (0, j)),
                  const2(fa_k.shape), _single(fb.shape, lambda o, j: (0, 0, 0)), const2(fbs.shape)],
        out_specs=pl.BlockSpec((None, n_rows, tl), lambda o, j: (o, 0, j)),
        out_shape=jax.ShapeDtypeStruct((HY_ORDER, n_rows, d), _f32),
        scratch_shapes=[pltpu.VMEM((FFT_H, FFT_N2, tl), _f32), pltpu.VMEM((FFT_N1, FFT_N2, tl), _f32)],
        compiler_params=_params(),
        name="filter_spectra",
    )(hdn, w4, w4, deltas, fa_k, fb, fbs)


def _merge_kernel(hx_ref, hsum_ref, yh_ref, x_ref, wrg_ref, brg_ref, wbg_ref, bbg_ref, wa_ref, wb_ref, wo_ref,
                  g1_ref, g_ref, sh_ref, sc_ref, x1_ref, h2_ref):
    d = x_ref.shape[-1]
    gates = _sigmoid(_dot(hx_ref[...], wbg_ref[...]) + bbg_ref[...])
    y_rnn = hsum_ref[...].astype(_f32) * jax.nn.gelu(_dot(hx_ref[...], wrg_ref[...]) + brg_ref[...])
    ya = _dot(y_rnn.astype(_bf16), wa_ref[...])
    yb = _dot(yh_ref[...].astype(_bf16), wb_ref[...])
    merged = gates[:, :d] * ya + gates[:, d:] * yb
    x1 = x_ref[...] + g1_ref[...] * _dot(merged.astype(_bf16), wo_ref[...])
    x1_ref[...] = x1
    y = x1 * lax.rsqrt(jnp.mean(x1 * x1, axis=-1, keepdims=True) + RMS_EPS) * g_ref[...]
    h2_ref[...] = (y * (1.0 + sc_ref[...]) + sh_ref[...]).astype(h2_ref.dtype)


def _merge(hx, h_sum, y_hy, x, w_in_bf, b_in, w_bg, b_bg, w_a, w_b, w_o, g1, norm2_g, sh2, sc2, tm=512):
    b, l, d = x.shape
    row = lambda width: pl.BlockSpec((None, tm, width), lambda i, j: (i, j, 0))
    vec = pl.BlockSpec((None, 1, d), lambda i, j: (i, 0, 0))
    full = lambda a: pl.BlockSpec(a.shape, lambda i, j: (0,) * a.ndim)
    b_in = b_in.reshape(1, -1)
    b_bg = b_bg.reshape(1, -1)
    norm2_g = norm2_g.reshape(1, d)
    return pl.pallas_call(
        _merge_kernel,
        grid=(b, l // tm),
        in_specs=[row(d), row(d), row(d), row(d),
                  pl.BlockSpec((d, d), lambda i, j: (0, 1)), pl.BlockSpec((1, d), lambda i, j: (0, 1)),
                  full(w_bg), full(b_bg), full(w_a), full(w_b), full(w_o), vec, full(norm2_g), vec, vec],
        out_specs=[row(d), row(d)],
        out_shape=[jax.ShapeDtypeStruct((b, l, d), _f32), jax.ShapeDtypeStruct((b, l, d), _bf16)],
        compiler_params=_params(),
        name="merge",
    )(hx, h_sum, y_hy, x, w_in_bf, b_in, w_bg, b_bg, w_a, w_b, w_o, g1, norm2_g, sh2, sc2)


def _ffn_kernel(h_ref, x1_ref, wg_ref, wu_ref, wo_ref, g2_ref, fg_ref, o_ref):
    h = h_ref[...]
    g = _dot(h, wg_ref[...])
    u = _dot(h, wu_ref[...])
    a = (g * _sigmoid(g) * u).astype(_bf16)
    x2 = x1_ref[...] + g2_ref[...] * _dot(a, wo_ref[...])
    y = x2 * lax.rsqrt(jnp.mean(x2 * x2, axis=-1, keepdims=True) + RMS_EPS)
    o_ref[...] = y * fg_ref[...]


def _ffn(h2, x1, w_gu, w_o, g2, final_g, tm=512):
    b, l, d = x1.shape
    d_ff = w_o.shape[0]
    row = pl.BlockSpec((None, tm, d), lambda i, j: (i, j, 0))
    vec = pl.BlockSpec((None, 1, d), lambda i, j: (i, 0, 0))
    full = lambda a: _single(a.shape, lambda i, j: (0,) * a.ndim)
    final_g = final_g.reshape(1, d)
    return pl.pallas_call(
        _ffn_kernel,
        grid=(b, l // tm),
        in_specs=[row, row, _single((d, d_ff), lambda i, j: (0, 0)), _single((d, d_ff), lambda i, j: (0, 1)),
                  full(w_o), vec, full(final_g)],
        out_specs=row,
        out_shape=jax.ShapeDtypeStruct((b, l, d), _f32),
        compiler_params=_params(),
        name="ffn",
    )(h2, x1, w_gu, w_gu, w_o, g2, final_g)


def kernel(x, c, ctx, c_ctx, w_mod, b_mod, norm1_g, norm2_g, w_in, b_in, rnn_conv_w, rnn_conv_b, rg_wa, rg_ba,
           rg_wx, rg_bx, rg_lambda, hy_conv_w, hy_conv_b, hy_w1, hy_b1, hy_w2, hy_b2, hy_w3, hy_b3, hy_freq,
           hy_w4, hy_skip, w_a_out, w_b_out, w_out, w_ffn_in, w_ffn_out, final_g):
    assert w_mod.shape[0] == 1, "single-layer block"
    b, seq, d = x.shape
    ctx_len = ctx.shape[1]
    assert seq == FFT_H * FFT_N2 and d == D_MODEL
    assert b == SUBLANES, "time-major rows put the batch on the sublanes of one register"
    (w_mod, b_mod, norm1_g, norm2_g, w_in, b_in, rnn_conv_w, rnn_conv_b, rg_wa, rg_ba, rg_wx, rg_bx,
     rg_lambda, hy_conv_w, hy_conv_b, hy_w1, hy_b1, hy_w2, hy_b2, hy_w3, hy_b3, hy_freq, hy_w4, hy_skip,
     w_a_out, w_b_out, w_out, w_ffn_in, w_ffn_out) = [
        a[0] for a in (w_mod, b_mod, norm1_g, norm2_g, w_in, b_in, rnn_conv_w, rnn_conv_b, rg_wa, rg_ba,
                       rg_wx, rg_bx, rg_lambda, hy_conv_w, hy_conv_b, hy_w1, hy_b1, hy_w2, hy_b2, hy_w3,
                       hy_b3, hy_freq, hy_w4, hy_skip, w_a_out, w_b_out, w_out, w_ffn_in, w_ffn_out)]

    pad_rows = 2 * SUBLANES - b - 1
    c_all = jnp.concatenate([c, c_ctx[None, :], jnp.zeros((pad_rows, d), _f32)], axis=0)
    mod = _mod_vectors(c_all, w_mod, b_mod)
    sh1, sc1, g1, sh2, sc2, g2 = [m.reshape(b, 1, d) for m in jnp.split(mod[:b], N_MOD, axis=-1)]
    csh1, csc1 = [jnp.broadcast_to(m.reshape(1, 1, d), (b, 1, d))
                  for m in jnp.split(mod[b], N_MOD, axis=-1)[:2]]

    w_in_bf = w_in.astype(_bf16)
    w_rx, b_rx = w_in_bf[:, :d], b_in[:d]
    w_bg, b_bg = w_in_bf[:, 5 * d:], b_in[5 * d:]
    gate_w, gate_bias = zip(*[_gate_weights(rg_wa[i], rg_wx[i], rg_ba[i], rg_bx[i]) for i in range(2)])
    gate_w, gate_bias = jnp.stack(gate_w), jnp.stack(gate_bias)
    lam = rg_lambda.reshape(2, 1, d)
    zeros_state = jnp.zeros((b, d), _f32)

    hc_tb = _norm_mod(ctx, norm1_g, csh1, csc1)
    u_c = _proj_conv(hc_tb, w_rx, b_rx, rnn_conv_w, rnn_conv_b, pad_left=RNN_CONV_PAD_LEFT, period=ctx_len,
                     row_stride=b, tm=ctx_len * b, out_dtype=_bf16)
    _, cf, cb = _rglru(u_c, b, gate_w, gate_bias, lam, zeros_state, zeros_state)

    hx, u, q = _inproj(x, norm1_g, sh1, sc1, w_in_bf, b_in, rnn_conv_w, rnn_conv_b, hy_conv_w, hy_conv_b, GRID_W)
    h_sum, _, _ = _rglru(u, b, gate_w, gate_bias, lam, cf, cb)

    consts = [jnp.asarray(a).astype(_bf16) for a in _dft_constants()]
    hdn = _filter_features(seq, hy_w1, hy_b1, hy_w2, hy_b2, hy_w3, hy_b3, hy_freq)
    kf = _filter_spectra(hdn, hy_w4, consts)
    q4 = q.reshape(b, FFT_H, FFT_N2, 3 * d)
    n_tiles = d // LANE_TILE
    z1 = _hyena_order(q4, 0, q4, n_tiles, kf, 0, hy_skip[0:1], consts, _bf16)
    y_hy = _hyena_order(z1, 0, q4, 2 * n_tiles, kf, 1, hy_skip[1:2], consts, _bf16)
    y_hy = y_hy.reshape(b, seq, d)

    x1, h2 = _merge(hx, h_sum, y_hy, x, w_in_bf, b_in, w_bg, b_bg, w_a_out.astype(_bf16), w_b_out.astype(_bf16),
                    w_out.astype(_bf16), g1, norm2_g, sh2, sc2)
    w_ffn_bf = w_ffn_in.astype(_bf16)
    return _ffn(h2, x1, w_ffn_bf, w_ffn_out.astype(_bf16), g2, final_g)
```

```python
import functools
import math

import numpy as np
import jax
import jax.numpy as jnp
from jax import lax
from jax.experimental import pallas as pl
from jax.experimental.pallas import tpu as pltpu

D_MODEL = 1024
GRID_W = 64
N_MOD = 6
RMS_EPS = 1e-6
RNN_HEADS = 16
RNN_HEAD_DIM = D_MODEL // RNN_HEADS
RNN_CONV_PAD_LEFT = 2
HY_CONV_PAD_LEFT = 1
RG_C = 8.0
HY_ORDER = 2
HY_EMB = 33
HY_BANDS = (HY_EMB - 1) // 2
HY_FAST_DECAY = 0.3
HY_SLOW_DECAY = 1.5
HY_TARGET = 1e-2
D_FF = ((8 * D_MODEL // 3 + 255) // 256) * 256

SUBLANES = 8
LANE_TILE = 256
VMEM_LIMIT = 56 * 1024 * 1024

FFT_N1 = 64
FFT_N2 = 128
FFT_N = FFT_N1 * FFT_N2
FFT_H = FFT_N1 // 2

_f32 = jnp.float32
_bf16 = jnp.bfloat16


def _params(**kw):
    return pltpu.CompilerParams(vmem_limit_bytes=VMEM_LIMIT, **kw)


def _dot(a, b):
    return jnp.dot(a, b, preferred_element_type=_f32)


def _sigmoid(x):
    return 0.5 * jnp.tanh(0.5 * x) + 0.5


def _mod_kernel(c_ref, w_ref, b_ref, o_ref):
    c = c_ref[...]
    s = c * _sigmoid(c)
    s_hi = s.astype(_bf16)
    s_lo = (s - s_hi.astype(_f32)).astype(_bf16)
    w = w_ref[...]
    w_hi = w.astype(_bf16)
    w_lo = (w - w_hi.astype(_f32)).astype(_bf16)
    o_ref[...] = _dot(jnp.concatenate([s_hi, s_lo, s_hi], axis=1),
                      jnp.concatenate([w_hi, w_hi, w_lo], axis=0)) + b_ref[...]


def _mod_vectors(c_all, w_mod, b_mod):
    rows, d = c_all.shape
    n = w_mod.shape[1]
    tn = 1024
    return pl.pallas_call(
        _mod_kernel,
        grid=(n // tn,),
        in_specs=[pl.BlockSpec((rows, d), lambda j: (0, 0)),
                  pl.BlockSpec((d, tn), lambda j: (0, j)),
                  pl.BlockSpec((1, tn), lambda j: (0, j))],
        out_specs=pl.BlockSpec((rows, tn), lambda j: (0, j)),
        out_shape=jax.ShapeDtypeStruct((rows, n), _f32),
        compiler_params=_params(),
        name="mod_vectors",
    )(c_all, w_mod, b_mod.reshape(1, n))


LANES = 128
NORM_TT = 128


def _norm_mod_kernel(x_ref, g_ref, sh_ref, sc_ref, otb_ref, tb_scr):
    nb, tt, d = x_ref.shape
    for b in range(nb):
        x = x_ref[b]
        gain = g_ref[...] * (1.0 + sc_ref[b])
        y = x * lax.rsqrt(jnp.mean(x * x, axis=-1, keepdims=True) + RMS_EPS) * gain + sh_ref[b]
        for s in range(d // LANES):
            tb_scr.at[s][pl.ds(b, tt, stride=nb), :] = y[:, s * LANES:(s + 1) * LANES]
    for s in range(d // LANES):
        otb_ref[:, s * LANES:(s + 1) * LANES] = tb_scr[s].astype(otb_ref.dtype)


def _norm_mod(x, g, sh, sc):
    b, l, d = x.shape
    tt = NORM_TT
    vec = pl.BlockSpec((b, 1, d), lambda i: (0, 0, 0))
    return pl.pallas_call(
        _norm_mod_kernel,
        grid=(l // tt,),
        in_specs=[pl.BlockSpec((b, tt, d), lambda i: (0, i, 0)),
                  pl.BlockSpec((1, d), lambda i: (0, 0)), vec, vec],
        out_specs=pl.BlockSpec((tt * b, d), lambda i: (i, 0)),
        out_shape=jax.ShapeDtypeStruct((l * b, d), _bf16),
        scratch_shapes=[pltpu.VMEM((d // LANES, tt * b, LANES), _f32)],
        compiler_params=_params(),
        name="norm_mod",
    )(x, g.reshape(1, d), sh, sc)


CONV_CHUNK_STEPS = 8


def _short_conv_period(y, w, cb, pad_left, nb, store):
    rows, lanes = y.shape
    n_taps = w.shape[0]
    ypad = jnp.concatenate([jnp.zeros((pad_left * nb, lanes), y.dtype), y,
                            jnp.zeros(((n_taps - 1 - pad_left) * nb, lanes), y.dtype)], axis=0)
    chunk = CONV_CHUNK_STEPS * nb
    for r0 in range(0, rows, chunk):
        out = cb
        for k in range(n_taps):
            out = out + ypad[r0 + k * nb:r0 + k * nb + chunk] * w[k:k + 1, :]
        store(r0, r0 + chunk, out)


def _proj_conv_kernel(x_ref, w_ref, b_ref, cw_ref, cb_ref, o_ref, *, pad_left, period, row_stride):
    tm, tn = o_ref.shape
    prow = period * row_stride
    for j in range(tn // LANE_TILE):
        cols = slice(j * LANE_TILE, (j + 1) * LANE_TILE)
        y = _dot(x_ref[...], w_ref[:, cols]) + b_ref[:, cols]
        for p in range(tm // prow):
            def store(r0, r1, val, base=p * prow, cols=cols):
                o_ref[base + r0:base + r1, cols] = val.astype(o_ref.dtype)

            _short_conv_period(y[p * prow:(p + 1) * prow], cw_ref[:, cols], cb_ref[:, cols], pad_left,
                               row_stride, store)


def _proj_conv(x2d, w, bias, conv_w, conv_b, *, pad_left, period, row_stride, tm, tn=1024, out_dtype=_f32):
    assert tm % (period * row_stride) == 0
    m, k = x2d.shape
    n = w.shape[1]
    col = lambda rows: pl.BlockSpec((rows, tn), lambda j, i: (0, j))
    return pl.pallas_call(
        functools.partial(_proj_conv_kernel, pad_left=pad_left, period=period, row_stride=row_stride),
        grid=(n // tn, m // tm),
        in_specs=[pl.BlockSpec((tm, k), lambda j, i: (i, 0)), col(k), col(1), col(conv_w.shape[0]), col(1)],
        out_specs=pl.BlockSpec((tm, tn), lambda j, i: (i, j)),
        out_shape=jax.ShapeDtypeStruct((m, n), out_dtype),
        compiler_params=_params(),
        name="proj_conv",
    )(x2d, w, bias.reshape(1, n), conv_w, conv_b.reshape(1, n))


def _inproj_kernel(x_ref, g_ref, sh_ref, sc_ref, w_ref, b_ref, rcw_ref, rcb_ref, hcw_ref, hcb_ref,
                   hx_ref, u_ref, q_ref, xtb_scr, lhs_scr, qtb_scr, *, rnn_pad, hy_pad, hy_col0):
    nb, tt, d = x_ref.shape
    for b in range(nb):
        x = x_ref[b]
        gain = g_ref[...] * (1.0 + sc_ref[b])
        y = x * lax.rsqrt(jnp.mean(x * x, axis=-1, keepdims=True) + RMS_EPS) * gain + sh_ref[b]
        hx_ref[b] = y.astype(hx_ref.dtype)
        for s in range(d // LANES):
            xtb_scr.at[s][pl.ds(b, tt, stride=nb), :] = y[:, s * LANES:(s + 1) * LANES]
    for s in range(d // LANES):
        lhs_scr[:, s * LANES:(s + 1) * LANES] = xtb_scr[s].astype(lhs_scr.dtype)

    n_rx = u_ref.shape[1] // LANE_TILE
    n_hy = q_ref.shape[2] // LANE_TILE
    slabs = LANE_TILE // LANES
    for j in range(n_rx + n_hy):
        c0 = j * LANE_TILE if j < n_rx else hy_col0 + (j - n_rx) * LANE_TILE
        cols = slice(c0, c0 + LANE_TILE)
        y = _dot(lhs_scr[...], w_ref[:, cols]) + b_ref[:, cols]
        if j < n_rx:
            def store_u(r0, r1, val, cols=cols):
                u_ref[r0:r1, cols] = val.astype(u_ref.dtype)

            _short_conv_period(y, rcw_ref[:, cols], rcb_ref[:, cols], rnn_pad, nb, store_u)
        else:
            jh = j - n_rx
            oc = slice(jh * LANE_TILE, (jh + 1) * LANE_TILE)

            def store_q(r0, r1, val, jh=jh):
                for s in range(slabs):
                    qtb_scr[jh * slabs + s, r0:r1, :] = val[:, s * LANES:(s + 1) * LANES]

            _short_conv_period(y, hcw_ref[:, oc], hcb_ref[:, oc], hy_pad, nb, store_q)
            for s in range(slabs):
                slab = jh * slabs + s
                for bi in range(nb):
                    piece = qtb_scr.at[slab][pl.ds(bi, tt, stride=nb), :]
                    q_ref[bi, :, slab * LANES:(slab + 1) * LANES] = piece.astype(q_ref.dtype)


def _inproj(x, g, sh, sc, w_in_bf, b_in, rnn_conv_w, rnn_conv_b, hy_conv_w, hy_conv_b, period):
    b, l, d = x.shape
    n_all = w_in_bf.shape[1]
    n_hy = hy_conv_w.shape[1]
    tt = period
    vec = pl.BlockSpec((b, 1, d), lambda i: (0, 0, 0))
    full = lambda a: _single(a.shape, lambda i: (0,) * a.ndim)
    g2, b2 = g.reshape(1, d), b_in.reshape(1, n_all)
    rcb, hcb = rnn_conv_b.reshape(1, d), hy_conv_b.reshape(1, n_hy)
    return pl.pallas_call(
        functools.partial(_inproj_kernel, rnn_pad=RNN_CONV_PAD_LEFT, hy_pad=HY_CONV_PAD_LEFT, hy_col0=2 * d),
        grid=(l // tt,),
        in_specs=[pl.BlockSpec((b, tt, d), lambda i: (0, i, 0)), full(g2), vec, vec,
                  full(w_in_bf), full(b2), full(rnn_conv_w), full(rcb), full(hy_conv_w), full(hcb)],
        out_specs=[pl.BlockSpec((b, tt, d), lambda i: (0, i, 0)), pl.BlockSpec((tt * b, d), lambda i: (i, 0)),
                   pl.BlockSpec((b, tt, n_hy), lambda i: (0, i, 0))],
        out_shape=[jax.ShapeDtypeStruct((b, l, d), _bf16), jax.ShapeDtypeStruct((l * b, d), _bf16),
                   jax.ShapeDtypeStruct((b, l, n_hy), _bf16)],
        scratch_shapes=[pltpu.VMEM((d // LANES, tt * b, LANES), _f32),
                        pltpu.VMEM((tt * b, d), _bf16),
                        pltpu.VMEM((n_hy // LANES, tt * b, LANES), _f32)],
        compiler_params=_params(),
        name="inproj",
    )(x, g2, sh, sc, w_in_bf, b2, rnn_conv_w, rcb, hy_conv_w, hcb)


RNN_TT = 512
RNN_SUB_TT = 64


def _rnn_coeffs(u_half, w_ref, bias_ref, lam_ref):
    tl = u_half.shape[-1]
    half_decay = (-0.5 * RG_C) * jax.nn.softplus(-lam_ref[...])
    g = _dot(u_half.astype(_bf16), w_ref[...]) + bias_ref[...]
    log_a = half_decay * jnp.tanh(g[:, :tl]) + half_decay
    gated_u = (jnp.tanh(g[:, tl:]) + 1.0) * u_half.astype(_f32)
    a = jnp.exp(log_a)
    x = -jnp.tanh(log_a) * (a * a + 1.0)
    root = jnp.where(x > 0.0, x * lax.rsqrt(x), 0.0)
    return a, root * gated_u


def _rnn_scan(u_ref, w_ref, bias_ref, lam_ref, a_scr, b_scr, h_ref, h0, reverse):
    nb = h0.shape[0]
    rows = u_ref.shape[0]
    sub = RNN_SUB_TT * nb
    n_sub = rows // sub
    order = list(range(n_sub))[::-1] if reverse else list(range(n_sub))

    def coeffs(k):
        sl = slice(k * sub, (k + 1) * sub)
        a, b = _rnn_coeffs(u_ref[sl, :], w_ref, bias_ref, lam_ref)
        a_scr[sl, :] = a
        b_scr[sl, :] = b

    coeffs(order[0])
    h = h0
    for pos, k in enumerate(order):
        if pos + 1 < n_sub:
            coeffs(order[pos + 1])
        steps = list(range(k * sub, (k + 1) * sub, nb))
        for r0 in (steps[::-1] if reverse else steps):
            h = a_scr[r0:r0 + nb, :] * h + b_scr[r0:r0 + nb, :]
            h_ref[r0:r0 + nb, :] = h
    return h


def _rnn_fwd_kernel(u_ref, w_ref, bias_ref, lam_ref, h0_ref, hf_ref, hlast_ref, a_scr, b_scr, h_scr):
    @pl.when(pl.program_id(1) == 0)
    def _():
        h_scr[...] = h0_ref[...]

    h = _rnn_scan(u_ref, w_ref, bias_ref, lam_ref, a_scr, b_scr, hf_ref, h_scr[...], False)
    h_scr[...] = h
    hlast_ref[...] = h


def _rnn_bwd_kernel(u_ref, hf_ref, w_ref, bias_ref, lam_ref, h0_ref, y_ref, hfirst_ref,
                    a_scr, b_scr, h_scr, y_scr):
    @pl.when(pl.program_id(1) == 0)
    def _():
        h_scr[...] = h0_ref[...]

    nb, tt, tl = y_ref.shape
    h = _rnn_scan(u_ref, w_ref, bias_ref, lam_ref, a_scr, b_scr, b_scr, h_scr[...], True)
    h_scr[...] = h
    hfirst_ref[...] = h
    y = hf_ref[...] + b_scr[...]
    for s in range(tl // LANES):
        y_scr[s] = y[:, s * LANES:(s + 1) * LANES]
    for bi in range(nb):
        for s in range(tl // LANES):
            piece = y_scr.at[s][pl.ds(bi, tt, stride=nb), :]
            y_ref[bi, :, s * LANES:(s + 1) * LANES] = piece.astype(y_ref.dtype)


def _rglru(u_tb, nb, gate_w, gate_bias, lam, h0f, h0b):
    rows, d = u_tb.shape
    seq = rows // nb
    tl = LANE_TILE
    tt = min(RNN_TT, seq)
    n_chunks = seq // tt
    chunk = tt * nb
    state = pl.BlockSpec((nb, tl), lambda j, c: (0, j))
    scratch = [pltpu.VMEM((chunk, tl), _f32), pltpu.VMEM((chunk, tl), _f32), pltpu.VMEM((nb, tl), _f32)]

    def param_specs(direction):
        return [pl.BlockSpec((None, None, tl, 2 * tl), lambda j, c: (direction, j, 0, 0)),
                pl.BlockSpec((None, None, 1, 2 * tl), lambda j, c: (direction, j, 0, 0)),
                pl.BlockSpec((None, 1, tl), lambda j, c: (direction, 0, j))]

    fwd_rows = pl.BlockSpec((chunk, tl), lambda j, c: (c, j))
    hf_tb, hf_last = pl.pallas_call(
        _rnn_fwd_kernel,
        grid=(d // tl, n_chunks),
        in_specs=[fwd_rows] + param_specs(0) + [state],
        out_specs=[fwd_rows, state],
        out_shape=[jax.ShapeDtypeStruct((rows, d), _f32), jax.ShapeDtypeStruct((nb, d), _f32)],
        scratch_shapes=scratch,
        compiler_params=_params(),
        name="rglru_fwd",
    )(u_tb, gate_w, gate_bias, lam, h0f)

    bwd_rows = pl.BlockSpec((chunk, tl), lambda j, c: (n_chunks - 1 - c, j))
    y, hb_first = pl.pallas_call(
        _rnn_bwd_kernel,
        grid=(d // tl, n_chunks),
        in_specs=[bwd_rows, bwd_rows] + param_specs(1) + [state],
        out_specs=[pl.BlockSpec((nb, tt, tl), lambda j, c: (0, n_chunks - 1 - c, j)), state],
        out_shape=[jax.ShapeDtypeStruct((nb, seq, d), _bf16), jax.ShapeDtypeStruct((nb, d), _f32)],
        scratch_shapes=scratch + [pltpu.VMEM((tl // LANES, chunk, LANES), _f32)],
        compiler_params=_params(),
        name="rglru_bwd",
    )(u_tb, hf_tb, gate_w, gate_bias, lam, h0b)
    return y, hf_last, hb_first


def _gate_weights(wa, wx, ba, bx):
    heads_per_tile = LANE_TILE // RNN_HEAD_DIM
    n_tiles = RNN_HEADS // heads_per_tile

    def tile_blockdiag(w):
        w = w.reshape(n_tiles, heads_per_tile, RNN_HEAD_DIM, RNN_HEAD_DIM)
        eye = jnp.eye(heads_per_tile, dtype=w.dtype)
        full = jnp.einsum('thij,hg->thigj', w, eye)
        return full.reshape(n_tiles, LANE_TILE, LANE_TILE)

    w = jnp.concatenate([tile_blockdiag(wa), tile_blockdiag(wx)], axis=-1).astype(_bf16)
    bias = jnp.concatenate([ba.reshape(n_tiles, 1, LANE_TILE), bx.reshape(n_tiles, 1, LANE_TILE)], axis=-1)
    return w, (0.5 * bias).astype(_f32)


HY_UNROLL_A = 16
HY_UNROLL_B = 31
HY_EPILOGUE_SLABS = 4


def _dft_constants():
    n1 = np.arange(FFT_H)
    k1 = np.arange(FFT_H + 1)
    ang = 2.0 * np.pi * np.outer(k1, n1) / FFT_N1
    fa = np.concatenate([np.cos(ang), -np.sin(ang)[1:FFT_H]], axis=0)
    fa_k = np.kron(fa, np.eye(SUBLANES))
    weight = np.where((k1 == 0) | (k1 == FFT_H), 1.0, 2.0)[:, None] / FFT_N
    fai = np.concatenate([weight * np.cos(ang), (-2.0 / FFT_N) * np.sin(ang)[1:FFT_H]], axis=0).T
    fai_k = np.kron(fai, np.eye(SUBLANES))

    n2 = np.arange(FFT_N2)
    k2 = np.arange(FFT_N2)

    def cs(k1v):
        idx = (np.outer(k2, n2) * FFT_N1 + k1v * n2[None, :]) % FFT_N
        phi = 2.0 * np.pi * idx / FFT_N
        return np.cos(phi), np.sin(phi)

    fb, fbi = [], []
    for k1v in range(1, FFT_H):
        c, s = cs(k1v)
        fb.append(np.block([[c, s], [-s, c]]))
        fbi.append(np.block([[c.T, -s.T], [s.T, c.T]]))
    c0, s0 = cs(0)
    ch, sh = cs(FFT_H)
    z = np.zeros_like(c0)
    fbs = np.block([[c0, z], [-s0, z], [z, ch], [z, -sh]])
    fbsi = np.block([[c0.T, -s0.T, z, z], [z, z, ch.T, -sh.T]])
    as32 = lambda a: np.asarray(a, np.float32)
    return as32(fa_k), as32(fai_k), as32(np.stack(fb)), as32(np.stack(fbi)), as32(fbs), as32(fbsi)


def _hyena_kernel(z_ref, gate_ref, kf_ref, skip_ref, fa_ref, fai_ref, fb_ref, fbi_ref, fbs_ref, fbsi_ref,
                  o_ref, w_scr, conv_scr):
    tl = z_ref.shape[-1]
    half = FFT_N2
    pack = 2 * SUBLANES

    for p in range(FFT_N2 // pack):
        zz = z_ref[:, p * pack:(p + 1) * pack, :].astype(_f32)
        for hf in range(2):
            r0 = p * pack + hf * SUBLANES
            zin = zz[:, hf * SUBLANES:(hf + 1) * SUBLANES, :].reshape(FFT_H * SUBLANES, tl)
            y = _dot(fa_ref[...], zin.astype(_bf16))
            w_scr[:, r0:r0 + SUBLANES, :] = y.reshape(FFT_N1, SUBLANES, tl)

    yin = jnp.concatenate([w_scr[0], w_scr[FFT_H]], axis=0).astype(_bf16)
    x = _dot(fbs_ref[...], yin)
    pieces = []
    for q in range(2):
        xr = x[(2 * q) * half:(2 * q + 1) * half]
        xi = x[(2 * q + 1) * half:(2 * q + 2) * half]
        kr = kf_ref[pl.ds((2 * q) * half, half), :]
        ki = kf_ref[pl.ds((2 * q + 1) * half, half), :]
        pieces += [xr * kr - xi * ki, xr * ki + xi * kr]
    v = _dot(fbsi_ref[...], jnp.concatenate(pieces, axis=0).astype(_bf16))
    w_scr[0] = v[:half]
    w_scr[FFT_H] = v[half:]

    def stage_b(k1, carry):
        yin = jnp.concatenate([w_scr[k1], w_scr[FFT_H + k1]], axis=0).astype(_bf16)
        x = _dot(fb_ref[k1 - 1], yin)
        xr, xi = x[:half], x[half:]
        base = pl.multiple_of(2 * half * (k1 + 1), 2 * half)
        kr = kf_ref[pl.ds(base, half), :]
        ki = kf_ref[pl.ds(base + half, half), :]
        p = jnp.concatenate([xr * kr - xi * ki, xr * ki + xi * kr], axis=0).astype(_bf16)
        v = _dot(fbi_ref[k1 - 1], p)
        w_scr[k1] = v[:half]
        w_scr[FFT_H + k1] = v[half:]
        return carry

    lax.fori_loop(1, FFT_H, stage_b, 0, unroll=HY_UNROLL_B)

    for g in range(FFT_N2 // SUBLANES):
        r0 = g * SUBLANES
        vin = w_scr[:, r0:r0 + SUBLANES, :].reshape(FFT_N1 * SUBLANES, tl)
        conv_scr[:, r0:r0 + SUBLANES, :] = _dot(fai_ref[...], vin.astype(_bf16)).reshape(FFT_H, SUBLANES, tl)

    skip = skip_ref[...]
    for c in range(0, FFT_H, HY_EPILOGUE_SLABS):
        sl = slice(c, c + HY_EPILOGUE_SLABS)
        out = gate_ref[sl].astype(_f32) * (conv_scr[sl] + skip * z_ref[sl].astype(_f32))
        o_ref[sl] = out.astype(o_ref.dtype)


def _single(shape, index_map):
    return pl.BlockSpec(shape, index_map, pipeline_mode=pl.Buffered(1))


def _hyena_order(z_arr, z_col, gate_arr, gate_col, kf, order, skip, consts, out_dtype):
    fa_k, fai_k, fb, fbi, fbs, fbsi = consts
    b = z_arr.shape[0]
    tl = LANE_TILE
    d_out = kf.shape[-1]
    seq_block = (None, FFT_H, FFT_N2, tl)
    const2 = lambda shape: _single(shape, lambda j, i: (0, 0))
    const3 = lambda shape: _single(shape, lambda j, i: (0, 0, 0))
    return pl.pallas_call(
        _hyena_kernel,
        grid=(d_out // tl, b),
        in_specs=[pl.BlockSpec(seq_block, lambda j, i: (i, 0, 0, z_col + j)),
                  pl.BlockSpec(seq_block, lambda j, i: (i, 0, 0, gate_col + j)),
                  pl.BlockSpec((None, kf.shape[1], tl), lambda j, i: (order, 0, j)),
                  pl.BlockSpec((1, tl), lambda j, i: (0, j)),
                  const2(fa_k.shape), const2(fai_k.shape), const3(fb.shape), const3(fbi.shape),
                  const2(fbs.shape), const2(fbsi.shape)],
        out_specs=pl.BlockSpec(seq_block, lambda j, i: (i, 0, 0, j)),
        out_shape=jax.ShapeDtypeStruct((b, FFT_H, FFT_N2, d_out), out_dtype),
        scratch_shapes=[pltpu.VMEM((FFT_N1, FFT_N2, tl), _f32), pltpu.VMEM((FFT_H, FFT_N2, tl), _f32)],
        compiler_params=_params(),
        name="hyena_conv",
    )(z_arr, gate_arr, kf, skip, fa_k, fai_k, fb, fbi, fbs, fbsi)


def _filter_kernel(hdn_ref, w4f_ref, w4b_ref, delta_ref, fa_ref, fb_ref, fbs_ref, o_ref, sig_scr, w_scr):
    seq = hdn_ref.shape[0]
    tl = o_ref.shape[-1]
    half = FFT_N2
    def dot3(w):
        w_hi = w.astype(_bf16)
        w_lo = (w - w_hi.astype(_f32)).astype(_bf16)
        return _dot(hdn_ref[...], jnp.concatenate([w_hi, w_hi, w_lo], axis=0))

    row = lax.broadcasted_iota(jnp.int32, (seq, tl), 0)
    decay = jnp.exp(-(row.astype(_f32) * (1.0 / (seq - 1))) * delta_ref[...])
    f = dot3(w4f_ref[...]) * decay
    g = dot3(w4b_ref[...]) * decay
    g = jnp.where(row == 0, 0.0, g)
    norm = jnp.sum(jnp.abs(f), axis=0, keepdims=True) + jnp.sum(jnp.abs(g), axis=0, keepdims=True)
    inv_norm = 1.0 / norm
    f = f * inv_norm
    g = g * inv_norm

    for part, sig in enumerate((f + g, f - g)):
        sig_scr[...] = sig.reshape(FFT_H, FFT_N2, tl)

        def stage_a(gi, carry):
            r0 = pl.multiple_of(gi * SUBLANES, SUBLANES)
            zin = sig_scr[:, pl.ds(r0, SUBLANES), :].reshape(FFT_H * SUBLANES, tl)
            y = _dot(fa_ref[...], zin.astype(_bf16))
            w_scr[:, pl.ds(r0, SUBLANES), :] = y.reshape(FFT_N1, SUBLANES, tl)
            return carry

        lax.fori_loop(0, FFT_N2 // SUBLANES, stage_a, 0, unroll=HY_UNROLL_A)

        yin = jnp.concatenate([w_scr[0], w_scr[FFT_H]], axis=0).astype(_bf16)
        o_ref[pl.ds(part * half, half), :] = _dot(fbs_ref[pl.ds(part * half, half), :], yin)
        o_ref[pl.ds((2 + part) * half, half), :] = _dot(fbs_ref[pl.ds((2 + part) * half, half), :], yin)

        def stage_b(k1, carry):
            yin = jnp.concatenate([w_scr[k1], w_scr[FFT_H + k1]], axis=0).astype(_bf16)
            base = pl.multiple_of(2 * half * (k1 + 1) + part * half, half)
            o_ref[pl.ds(base, half), :] = _dot(fb_ref[k1 - 1, pl.ds(part * half, half), :], yin)
            return carry

        lax.fori_loop(1, FFT_H, stage_b, 0, unroll=HY_UNROLL_B)


def _filter_features(seq, hy_w1, hy_b1, hy_w2, hy_b2, hy_w3, hy_b3, hy_freq):
    t = jnp.linspace(0.0, 1.0, seq, dtype=_f32)[:, None]
    w = 2.0 * math.pi * jnp.arange(seq, dtype=_f32)[:, None] / seq
    f = jnp.linspace(1e-4, HY_BANDS - 1, HY_BANDS, dtype=_f32)[None, :]
    z = jnp.concatenate([t, jnp.cos(f * w), -jnp.sin(f * w)], axis=-1)
    hi = lax.Precision.HIGHEST
    hdn = jnp.sin(hy_freq * (jnp.dot(z, hy_w1, precision=hi) + hy_b1))
    hdn = jnp.sin(hy_freq * (jnp.dot(hdn, hy_w2, precision=hi) + hy_b2))
    hdn = jnp.sin(hy_freq * (jnp.dot(hdn, hy_w3, precision=hi) + hy_b3))
    hdn_hi = hdn.astype(_bf16)
    hdn_lo = (hdn - hdn_hi.astype(_f32)).astype(_bf16)
    return jnp.concatenate([hdn_hi, hdn_lo, hdn_hi], axis=1)


def _filter_spectra(hdn, hy_w4, consts):
    fa_k, _, fb, _, fbs, _ = consts
    width = hy_w4.shape[0]
    d = hy_w4.shape[1] // (2 * HY_ORDER)
    tl = LANE_TILE
    w4 = hy_w4.reshape(width, 2 * HY_ORDER, d).transpose(1, 0, 2)
    max_decay = math.log(HY_TARGET) / HY_FAST_DECAY
    min_decay = math.log(HY_TARGET) / HY_SLOW_DECAY
    deltas = jnp.abs(jnp.linspace(min_decay, max_decay, d, dtype=_f32)).reshape(1, d)
    n_rows = (FFT_H + 1) * 2 * FFT_N2
    const2 = lambda shape: _single(shape, lambda o, j: (0, 0))
    return pl.pallas_call(
        _filter_kernel,
        grid=(HY_ORDER, d // tl),
        in_specs=[const2(hdn.shape),
                  pl.BlockSpec((None, width, tl), lambda o, j: (2 * o, 0, j)),
                  pl.BlockSpec((None, width, tl), lambda o, j: (2 * o + 1, 0, j)),
                  pl.BlockSpec((1, tl), lambda o, j: (0, j)),
                  const2(fa_k.shape), _single(fb.shape, lambda o, j: (0, 0, 0)), const2(fbs.shape)],
        out_specs=pl.BlockSpec((None, n_rows, tl), lambda o, j: (o, 0, j)),
        out_shape=jax.ShapeDtypeStruct((HY_ORDER, n_rows, d), _f32),
        scratch_shapes=[pltpu.VMEM((FFT_H, FFT_N2, tl), _f32), pltpu.VMEM((FFT_N1, FFT_N2, tl), _f32)],
        compiler_params=_params(),
        name="filter_spectra",
    )(hdn, w4, w4, deltas, fa_k, fb, fbs)


def _merge_kernel(hx_ref, hsum_ref, yh_ref, x_ref, wrg_ref, brg_ref, wbg_ref, bbg_ref, wa_ref, wb_ref, wo_ref,
                  g1_ref, g_ref, sh_ref, sc_ref, x1_ref, h2_ref):
    d = x_ref.shape[-1]
    gates = _sigmoid(_dot(hx_ref[...], wbg_ref[...]) + bbg_ref[...])
    y_rnn = hsum_ref[...].astype(_f32) * jax.nn.gelu(_dot(hx_ref[...], wrg_ref[...]) + brg_ref[...])
    ya = _dot(y_rnn.astype(_bf16), wa_ref[...])
    yb = _dot(yh_ref[...].astype(_bf16), wb_ref[...])
    merged = gates[:, :d] * ya + gates[:, d:] * yb
    x1 = x_ref[...] + g1_ref[...] * _dot(merged.astype(_bf16), wo_ref[...])
    x1_ref[...] = x1
    y = x1 * lax.rsqrt(jnp.mean(x1 * x1, axis=-1, keepdims=True) + RMS_EPS) * g_ref[...]
    h2_ref[...] = (y * (1.0 + sc_ref[...]) + sh_ref[...]).astype(h2_ref.dtype)


def _merge(hx, h_sum, y_hy, x, w_in_bf, b_in, w_bg, b_bg, w_a, w_b, w_o, g1, norm2_g, sh2, sc2, tm=512):
    b, l, d = x.shape
    row = lambda width: pl.BlockSpec((None, tm, width), lambda i, j: (i, j, 0))
    vec = pl.BlockSpec((None, 1, d), lambda i, j: (i, 0, 0))
    full = lambda a: pl.BlockSpec(a.shape, lambda i, j: (0,) * a.ndim)
    b_in = b_in.reshape(1, -1)
    b_bg = b_bg.reshape(1, -1)
    norm2_g = norm2_g.reshape(1, d)
    return pl.pallas_call(
        _merge_kernel,
        grid=(b, l // tm),
        in_specs=[row(d), row(d), row(d), row(d),
                  pl.BlockSpec((d, d), lambda i, j: (0, 1)), pl.BlockSpec((1, d), lambda i, j: (0, 1)),
                  full(w_bg), full(b_bg), full(w_a), full(w_b), full(w_o), vec, full(norm2_g), vec, vec],
        out_specs=[row(d), row(d)],
        out_shape=[jax.ShapeDtypeStruct((b, l, d), _f32), jax.ShapeDtypeStruct((b, l, d), _bf16)],
        compiler_params=_params(),
        name="merge",
    )(hx, h_sum, y_hy, x, w_in_bf, b_in, w_bg, b_bg, w_a, w_b, w_o, g1, norm2_g, sh2, sc2)


def _ffn_kernel(h_ref, x1_ref, wg_ref, wu_ref, wo_ref, g2_ref, fg_ref, o_ref):
    h = h_ref[...]
    g = _dot(h, wg_ref[...])
    u = _dot(h, wu_ref[...])
    a = (g * _sigmoid(g) * u).astype(_bf16)
    x2 = x1_ref[...] + g2_ref[...] * _dot(a, wo_ref[...])
    y = x2 * lax.rsqrt(jnp.mean(x2 * x2, axis=-1, keepdims=True) + RMS_EPS)
    o_ref[...] = y * fg_ref[...]


def _ffn(h2, x1, w_gu, w_o, g2, final_g, tm=512):
    b, l, d = x1.shape
    d_ff = w_o.shape[0]
    row = pl.BlockSpec((None, tm, d), lambda i, j: (i, j, 0))
    vec = pl.BlockSpec((None, 1, d), lambda i, j: (i, 0, 0))
    full = lambda a: _single(a.shape, lambda i, j: (0,) * a.ndim)
    final_g = final_g.reshape(1, d)
    return pl.pallas_call(
        _ffn_kernel,
        grid=(b, l // tm),
        in_specs=[row, row, _single((d, d_ff), lambda i, j: (0, 0)), _single((d, d_ff), lambda i, j: (0, 1)),
                  full(w_o), vec, full(final_g)],
        out_specs=row,
        out_shape=jax.ShapeDtypeStruct((b, l, d), _f32),
        compiler_params=_params(),
        name="ffn",
    )(h2, x1, w_gu, w_gu, w_o, g2, final_g)


def kernel(x, c, ctx, c_ctx, w_mod, b_mod, norm1_g, norm2_g, w_in, b_in, rnn_conv_w, rnn_conv_b, rg_wa, rg_ba,
           rg_wx, rg_bx, rg_lambda, hy_conv_w, hy_conv_b, hy_w1, hy_b1, hy_w2, hy_b2, hy_w3, hy_b3, hy_freq,
           hy_w4, hy_skip, w_a_out, w_b_out, w_out, w_ffn_in, w_ffn_out, final_g):
    assert w_mod.shape[0] == 1, "single-layer block"
    b, seq, d = x.shape
    ctx_len = ctx.shape[1]
    assert seq == FFT_H * FFT_N2 and d == D_MODEL
    assert b == SUBLANES, "time-major rows put the batch on the sublanes of one register"
    (w_mod, b_mod, norm1_g, norm2_g, w_in, b_in, rnn_conv_w, rnn_conv_b, rg_wa, rg_ba, rg_wx, rg_bx,
     rg_lambda, hy_conv_w, hy_conv_b, hy_w1, hy_b1, hy_w2, hy_b2, hy_w3, hy_b3, hy_freq, hy_w4, hy_skip,
     w_a_out, w_b_out, w_out, w_ffn_in, w_ffn_out) = [
        a[0] for a in (w_mod, b_mod, norm1_g, norm2_g, w_in, b_in, rnn_conv_w, rnn_conv_b, rg_wa, rg_ba,
                       rg_wx, rg_bx, rg_lambda, hy_conv_w, hy_conv_b, hy_w1, hy_b1, hy_w2, hy_b2, hy_w3,
                       hy_b3, hy_freq, hy_w4, hy_skip, w_a_out, w_b_out, w_out, w_ffn_in, w_ffn_out)]

    pad_rows = 2 * SUBLANES - b - 1
    c_all = jnp.concatenate([c, c_ctx[None, :], jnp.zeros((pad_rows, d), _f32)], axis=0)
    mod = _mod_vectors(c_all, w_mod, b_mod)
    sh1, sc1, g1, sh2, sc2, g2 = [m.reshape(b, 1, d) for m in jnp.split(mod[:b], N_MOD, axis=-1)]
    csh1, csc1 = [jnp.broadcast_to(m.reshape(1, 1, d), (b, 1, d))
                  for m in jnp.split(mod[b], N_MOD, axis=-1)[:2]]

    w_in_bf = w_in.astype(_bf16)
    w_rx, b_rx = w_in_bf[:, :d], b_in[:d]
    w_bg, b_bg = w_in_bf[:, 5 * d:], b_in[5 * d:]
    gate_w, gate_bias = zip(*[_gate_weights(rg_wa[i], rg_wx[i], rg_ba[i], rg_bx[i]) for i in range(2)])
    gate_w, gate_bias = jnp.stack(gate_w), jnp.stack(gate_bias)
    lam = rg_lambda.reshape(2, 1, d)
    zeros_state = jnp.zeros((b, d), _f32)
    rnn_conv_w, rnn_conv_b = 0.5 * rnn_conv_w, 0.5 * rnn_conv_b

    hc_tb = _norm_mod(ctx, norm1_g, csh1, csc1)
    u_c = _proj_conv(hc_tb, w_rx, b_rx, rnn_conv_w, rnn_conv_b, pad_left=RNN_CONV_PAD_LEFT, period=ctx_len,
                     row_stride=b, tm=ctx_len * b, out_dtype=_bf16)
    _, cf, cb = _rglru(u_c, b, gate_w, gate_bias, lam, zeros_state, zeros_state)

    hx, u, q = _inproj(x, norm1_g, sh1, sc1, w_in_bf, b_in, rnn_conv_w, rnn_conv_b, hy_conv_w, hy_conv_b, GRID_W)
    h_sum, _, _ = _rglru(u, b, gate_w, gate_bias, lam, cf, cb)

    consts = [jnp.asarray(a).astype(_bf16) for a in _dft_constants()]
    hdn = _filter_features(seq, hy_w1, hy_b1, hy_w2, hy_b2, hy_w3, hy_b3, hy_freq)
    kf = _filter_spectra(hdn, hy_w4, consts)
    q4 = q.reshape(b, FFT_H, FFT_N2, 3 * d)
    n_tiles = d // LANE_TILE
    z1 = _hyena_order(q4, 0, q4, n_tiles, kf, 0, hy_skip[0:1], consts, _bf16)
    y_hy = _hyena_order(z1, 0, q4, 2 * n_tiles, kf, 1, hy_skip[1:2], consts, _bf16)
    y_hy = y_hy.reshape(b, seq, d)

    x1, h2 = _merge(hx, h_sum, y_hy, x, w_in_bf, b_in, w_bg, b_bg, w_a_out.astype(_bf16), w_b_out.astype(_bf16),
                    w_out.astype(_bf16), g1, norm2_g, sh2, sc2)
    w_ffn_bf = w_ffn_in.astype(_bf16)
    return _ffn(h2, x1, w_ffn_bf, w_ffn_out.astype(_bf16), g2, final_g)
```

```python
import functools
import math

import numpy as np
import jax
import jax.numpy as jnp
from jax import lax
from jax.experimental import pallas as pl
from jax.experimental.pallas import tpu as pltpu

D_MODEL = 1024
GRID_W = 64
N_MOD = 6
RMS_EPS = 1e-6
RNN_HEADS = 16
RNN_HEAD_DIM = D_MODEL // RNN_HEADS
RNN_CONV_PAD_LEFT = 2
HY_CONV_PAD_LEFT = 1
RG_C = 8.0
HY_ORDER = 2
HY_EMB = 33
HY_BANDS = (HY_EMB - 1) // 2
HY_FAST_DECAY = 0.3
HY_SLOW_DECAY = 1.5
HY_TARGET = 1e-2
D_FF = ((8 * D_MODEL // 3 + 255) // 256) * 256

SUBLANES = 8
LANE_TILE = 256
VMEM_LIMIT = 56 * 1024 * 1024
MERGE_VMEM_LIMIT = 60 * 1024 * 1024

FFT_N1 = 64
FFT_N2 = 128
FFT_N = FFT_N1 * FFT_N2
FFT_H = FFT_N1 // 2

_f32 = jnp.float32
_bf16 = jnp.bfloat16


def _params(**kw):
    return pltpu.CompilerParams(vmem_limit_bytes=VMEM_LIMIT, **kw)


def _dot(a, b):
    return jnp.dot(a, b, preferred_element_type=_f32)


def _sigmoid(x):
    return 0.5 * jnp.tanh(0.5 * x) + 0.5


def _mod_kernel(c_ref, w_ref, b_ref, o_ref):
    c = c_ref[...]
    s = c * _sigmoid(c)
    s_hi = s.astype(_bf16)
    s_lo = (s - s_hi.astype(_f32)).astype(_bf16)
    w = w_ref[...]
    w_hi = w.astype(_bf16)
    w_lo = (w - w_hi.astype(_f32)).astype(_bf16)
    o_ref[...] = _dot(jnp.concatenate([s_hi, s_lo, s_hi], axis=1),
                      jnp.concatenate([w_hi, w_hi, w_lo], axis=0)) + b_ref[...]


def _mod_vectors(c_all, w_mod, b_mod):
    rows, d = c_all.shape
    n = w_mod.shape[1]
    tn = 1024
    return pl.pallas_call(
        _mod_kernel,
        grid=(n // tn,),
        in_specs=[pl.BlockSpec((rows, d), lambda j: (0, 0)),
                  pl.BlockSpec((d, tn), lambda j: (0, j)),
                  pl.BlockSpec((1, tn), lambda j: (0, j))],
        out_specs=pl.BlockSpec((rows, tn), lambda j: (0, j)),
        out_shape=jax.ShapeDtypeStruct((rows, n), _f32),
        compiler_params=_params(),
        name="mod_vectors",
    )(c_all, w_mod, b_mod.reshape(1, n))


LANES = 128
NORM_TT = 128


def _norm_mod_kernel(x_ref, g_ref, sh_ref, sc_ref, otb_ref, tb_scr):
    nb, tt, d = x_ref.shape
    for b in range(nb):
        x = x_ref[b]
        gain = g_ref[...] * (1.0 + sc_ref[b])
        y = x * lax.rsqrt(jnp.mean(x * x, axis=-1, keepdims=True) + RMS_EPS) * gain + sh_ref[b]
        for s in range(d // LANES):
            tb_scr.at[s][pl.ds(b, tt, stride=nb), :] = y[:, s * LANES:(s + 1) * LANES]
    for s in range(d // LANES):
        otb_ref[:, s * LANES:(s + 1) * LANES] = tb_scr[s].astype(otb_ref.dtype)


def _norm_mod(x, g, sh, sc):
    b, l, d = x.shape
    tt = NORM_TT
    vec = pl.BlockSpec((b, 1, d), lambda i: (0, 0, 0))
    return pl.pallas_call(
        _norm_mod_kernel,
        grid=(l // tt,),
        in_specs=[pl.BlockSpec((b, tt, d), lambda i: (0, i, 0)),
                  pl.BlockSpec((1, d), lambda i: (0, 0)), vec, vec],
        out_specs=pl.BlockSpec((tt * b, d), lambda i: (i, 0)),
        out_shape=jax.ShapeDtypeStruct((l * b, d), _bf16),
        scratch_shapes=[pltpu.VMEM((d // LANES, tt * b, LANES), _f32)],
        compiler_params=_params(),
        name="norm_mod",
    )(x, g.reshape(1, d), sh, sc)


CONV_CHUNK_STEPS = 8


def _short_conv_period(y, w, cb, pad_left, nb, store):
    rows, lanes = y.shape
    n_taps = w.shape[0]
    ypad = jnp.concatenate([jnp.zeros((pad_left * nb, lanes), y.dtype), y,
                            jnp.zeros(((n_taps - 1 - pad_left) * nb, lanes), y.dtype)], axis=0)
    chunk = CONV_CHUNK_STEPS * nb
    for r0 in range(0, rows, chunk):
        out = cb
        for k in range(n_taps):
            out = out + ypad[r0 + k * nb:r0 + k * nb + chunk] * w[k:k + 1, :]
        store(r0, r0 + chunk, out)


def _proj_conv_kernel(x_ref, w_ref, b_ref, cw_ref, cb_ref, o_ref, *, pad_left, period, row_stride):
    tm, tn = o_ref.shape
    prow = period * row_stride
    for j in range(tn // LANE_TILE):
        cols = slice(j * LANE_TILE, (j + 1) * LANE_TILE)
        y = _dot(x_ref[...], w_ref[:, cols]) + b_ref[:, cols]
        for p in range(tm // prow):
            def store(r0, r1, val, base=p * prow, cols=cols):
                o_ref[base + r0:base + r1, cols] = val.astype(o_ref.dtype)

            _short_conv_period(y[p * prow:(p + 1) * prow], cw_ref[:, cols], cb_ref[:, cols], pad_left,
                               row_stride, store)


def _proj_conv(x2d, w, bias, conv_w, conv_b, *, pad_left, period, row_stride, tm, tn=1024, out_dtype=_f32):
    assert tm % (period * row_stride) == 0
    m, k = x2d.shape
    n = w.shape[1]
    col = lambda rows: pl.BlockSpec((rows, tn), lambda j, i: (0, j))
    return pl.pallas_call(
        functools.partial(_proj_conv_kernel, pad_left=pad_left, period=period, row_stride=row_stride),
        grid=(n // tn, m // tm),
        in_specs=[pl.BlockSpec((tm, k), lambda j, i: (i, 0)), col(k), col(1), col(conv_w.shape[0]), col(1)],
        out_specs=pl.BlockSpec((tm, tn), lambda j, i: (i, j)),
        out_shape=jax.ShapeDtypeStruct((m, n), out_dtype),
        compiler_params=_params(),
        name="proj_conv",
    )(x2d, w, bias.reshape(1, n), conv_w, conv_b.reshape(1, n))


def _inproj_kernel(x_ref, g_ref, sh_ref, sc_ref, w_ref, b_ref, rcw_ref, rcb_ref, hcw_ref, hcb_ref,
                   hx_ref, u_ref, q_ref, xtb_scr, lhs_scr, qtb_scr, *, rnn_pad, hy_pad, hy_col0):
    nb, tt, d = x_ref.shape
    for b in range(nb):
        x = x_ref[b]
        gain = g_ref[...] * (1.0 + sc_ref[b])
        y = x * lax.rsqrt(jnp.mean(x * x, axis=-1, keepdims=True) + RMS_EPS) * gain + sh_ref[b]
        hx_ref[b] = y.astype(hx_ref.dtype)
        for s in range(d // LANES):
            xtb_scr.at[s][pl.ds(b, tt, stride=nb), :] = y[:, s * LANES:(s + 1) * LANES]
    for s in range(d // LANES):
        lhs_scr[:, s * LANES:(s + 1) * LANES] = xtb_scr[s].astype(lhs_scr.dtype)

    n_rx = u_ref.shape[1] // LANE_TILE
    n_hy = q_ref.shape[2] // LANE_TILE
    slabs = LANE_TILE // LANES
    for j in range(n_rx + n_hy):
        c0 = j * LANE_TILE if j < n_rx else hy_col0 + (j - n_rx) * LANE_TILE
        cols = slice(c0, c0 + LANE_TILE)
        y = _dot(lhs_scr[...], w_ref[:, cols]) + b_ref[:, cols]
        if j < n_rx:
            def store_u(r0, r1, val, cols=cols):
                u_ref[r0:r1, cols] = val.astype(u_ref.dtype)

            _short_conv_period(y, rcw_ref[:, cols], rcb_ref[:, cols], rnn_pad, nb, store_u)
        else:
            jh = j - n_rx
            oc = slice(jh * LANE_TILE, (jh + 1) * LANE_TILE)

            def store_q(r0, r1, val, jh=jh):
                for s in range(slabs):
                    qtb_scr[jh * slabs + s, r0:r1, :] = val[:, s * LANES:(s + 1) * LANES]

            _short_conv_period(y, hcw_ref[:, oc], hcb_ref[:, oc], hy_pad, nb, store_q)
            for s in range(slabs):
                slab = jh * slabs + s
                for bi in range(nb):
                    piece = qtb_scr.at[slab][pl.ds(bi, tt, stride=nb), :]
                    q_ref[bi, :, slab * LANES:(slab + 1) * LANES] = piece.astype(q_ref.dtype)


def _inproj(x, g, sh, sc, w_in_bf, b_in, rnn_conv_w, rnn_conv_b, hy_conv_w, hy_conv_b, period):
    b, l, d = x.shape
    n_all = w_in_bf.shape[1]
    n_hy = hy_conv_w.shape[1]
    tt = period
    vec = pl.BlockSpec((b, 1, d), lambda i: (0, 0, 0))
    full = lambda a: _single(a.shape, lambda i: (0,) * a.ndim)
    g2, b2 = g.reshape(1, d), b_in.reshape(1, n_all)
    rcb, hcb = rnn_conv_b.reshape(1, d), hy_conv_b.reshape(1, n_hy)
    return pl.pallas_call(
        functools.partial(_inproj_kernel, rnn_pad=RNN_CONV_PAD_LEFT, hy_pad=HY_CONV_PAD_LEFT, hy_col0=2 * d),
        grid=(l // tt,),
        in_specs=[pl.BlockSpec((b, tt, d), lambda i: (0, i, 0)), full(g2), vec, vec,
                  full(w_in_bf), full(b2), full(rnn_conv_w), full(rcb), full(hy_conv_w), full(hcb)],
        out_specs=[pl.BlockSpec((b, tt, d), lambda i: (0, i, 0)), pl.BlockSpec((tt * b, d), lambda i: (i, 0)),
                   pl.BlockSpec((b, tt, n_hy), lambda i: (0, i, 0))],
        out_shape=[jax.ShapeDtypeStruct((b, l, d), _bf16), jax.ShapeDtypeStruct((l * b, d), _bf16),
                   jax.ShapeDtypeStruct((b, l, n_hy), _bf16)],
        scratch_shapes=[pltpu.VMEM((d // LANES, tt * b, LANES), _f32),
                        pltpu.VMEM((tt * b, d), _bf16),
                        pltpu.VMEM((n_hy // LANES, tt * b, LANES), _f32)],
        compiler_params=_params(),
        name="inproj",
    )(x, g2, sh, sc, w_in_bf, b2, rnn_conv_w, rcb, hy_conv_w, hcb)


RNN_TT = 512
RNN_SUB_TT = 64


def _rnn_coeffs(u_half, w_ref, bias_ref, lam_ref):
    tl = u_half.shape[-1]
    half_decay = (-0.5 * RG_C) * jax.nn.softplus(-lam_ref[...])
    g = _dot(u_half.astype(_bf16), w_ref[...]) + bias_ref[...]
    log_a = half_decay * jnp.tanh(g[:, :tl]) + half_decay
    gated_u = (jnp.tanh(g[:, tl:]) + 1.0) * u_half.astype(_f32)
    a = jnp.exp(log_a)
    x = -jnp.tanh(log_a) * (a * a + 1.0)
    root = jnp.where(x > 0.0, x * lax.rsqrt(x), 0.0)
    return a, root * gated_u


def _rnn_scan(u_ref, w_ref, bias_ref, lam_ref, a_scr, b_scr, h_ref, h0, reverse):
    nb = h0.shape[0]
    rows = u_ref.shape[0]
    sub = RNN_SUB_TT * nb
    n_sub = rows // sub
    order = list(range(n_sub))[::-1] if reverse else list(range(n_sub))

    def coeffs(k):
        sl = slice(k * sub, (k + 1) * sub)
        a, b = _rnn_coeffs(u_ref[sl, :], w_ref, bias_ref, lam_ref)
        a_scr[sl, :] = a
        b_scr[sl, :] = b

    coeffs(order[0])
    h = h0
    for pos, k in enumerate(order):
        if pos + 1 < n_sub:
            coeffs(order[pos + 1])
        steps = list(range(k * sub, (k + 1) * sub, nb))
        for r0 in (steps[::-1] if reverse else steps):
            h = a_scr[r0:r0 + nb, :] * h + b_scr[r0:r0 + nb, :]
            h_ref[r0:r0 + nb, :] = h
    return h


def _rnn_fwd_kernel(u_ref, w_ref, bias_ref, lam_ref, h0_ref, hf_ref, hlast_ref, a_scr, b_scr, h_scr):
    @pl.when(pl.program_id(1) == 0)
    def _():
        h_scr[...] = h0_ref[...]

    h = _rnn_scan(u_ref, w_ref, bias_ref, lam_ref, a_scr, b_scr, hf_ref, h_scr[...], False)
    h_scr[...] = h
    hlast_ref[...] = h


def _rnn_bwd_kernel(u_ref, hf_ref, w_ref, bias_ref, lam_ref, h0_ref, y_ref, hfirst_ref,
                    a_scr, b_scr, h_scr, y_scr):
    @pl.when(pl.program_id(1) == 0)
    def _():
        h_scr[...] = h0_ref[...]

    nb, tt, tl = y_ref.shape
    h = _rnn_scan(u_ref, w_ref, bias_ref, lam_ref, a_scr, b_scr, b_scr, h_scr[...], True)
    h_scr[...] = h
    hfirst_ref[...] = h
    y = hf_ref[...] + b_scr[...]
    for s in range(tl // LANES):
        y_scr[s] = y[:, s * LANES:(s + 1) * LANES]
    for bi in range(nb):
        for s in range(tl // LANES):
            piece = y_scr.at[s][pl.ds(bi, tt, stride=nb), :]
            y_ref[bi, :, s * LANES:(s + 1) * LANES] = piece.astype(y_ref.dtype)


def _rglru(u_tb, nb, gate_w, gate_bias, lam, h0f, h0b):
    rows, d = u_tb.shape
    seq = rows // nb
    tl = LANE_TILE
    tt = min(RNN_TT, seq)
    n_chunks = seq // tt
    chunk = tt * nb
    state = pl.BlockSpec((nb, tl), lambda j, c: (0, j))
    scratch = [pltpu.VMEM((chunk, tl), _f32), pltpu.VMEM((chunk, tl), _f32), pltpu.VMEM((nb, tl), _f32)]

    def param_specs(direction):
        return [pl.BlockSpec((None, None, tl, 2 * tl), lambda j, c: (direction, j, 0, 0)),
                pl.BlockSpec((None, None, 1, 2 * tl), lambda j, c: (direction, j, 0, 0)),
                pl.BlockSpec((None, 1, tl), lambda j, c: (direction, 0, j))]

    fwd_rows = pl.BlockSpec((chunk, tl), lambda j, c: (c, j))
    hf_tb, hf_last = pl.pallas_call(
        _rnn_fwd_kernel,
        grid=(d // tl, n_chunks),
        in_specs=[fwd_rows] + param_specs(0) + [state],
        out_specs=[fwd_rows, state],
        out_shape=[jax.ShapeDtypeStruct((rows, d), _f32), jax.ShapeDtypeStruct((nb, d), _f32)],
        scratch_shapes=scratch,
        compiler_params=_params(),
        name="rglru_fwd",
    )(u_tb, gate_w, gate_bias, lam, h0f)

    bwd_rows = pl.BlockSpec((chunk, tl), lambda j, c: (n_chunks - 1 - c, j))
    y, hb_first = pl.pallas_call(
        _rnn_bwd_kernel,
        grid=(d // tl, n_chunks),
        in_specs=[bwd_rows, bwd_rows] + param_specs(1) + [state],
        out_specs=[pl.BlockSpec((nb, tt, tl), lambda j, c: (0, n_chunks - 1 - c, j)), state],
        out_shape=[jax.ShapeDtypeStruct((nb, seq, d), _bf16), jax.ShapeDtypeStruct((nb, d), _f32)],
        scratch_shapes=scratch + [pltpu.VMEM((tl // LANES, chunk, LANES), _f32)],
        compiler_params=_params(),
        name="rglru_bwd",
    )(u_tb, hf_tb, gate_w, gate_bias, lam, h0b)
    return y, hf_last, hb_first


def _gate_weights(wa, wx, ba, bx):
    heads_per_tile = LANE_TILE // RNN_HEAD_DIM
    n_tiles = RNN_HEADS // heads_per_tile

    def tile_blockdiag(w):
        w = w.reshape(n_tiles, heads_per_tile, RNN_HEAD_DIM, RNN_HEAD_DIM)
        eye = jnp.eye(heads_per_tile, dtype=w.dtype)
        full = jnp.einsum('thij,hg->thigj', w, eye)
        return full.reshape(n_tiles, LANE_TILE, LANE_TILE)

    w = jnp.concatenate([tile_blockdiag(wa), tile_blockdiag(wx)], axis=-1).astype(_bf16)
    bias = jnp.concatenate([ba.reshape(n_tiles, 1, LANE_TILE), bx.reshape(n_tiles, 1, LANE_TILE)], axis=-1)
    return w, (0.5 * bias).astype(_f32)


HY_UNROLL_A = 16
HY_UNROLL_B = 31
HY_EPILOGUE_SLABS = 4


def _dft_constants():
    n1 = np.arange(FFT_H)
    k1 = np.arange(FFT_H + 1)
    ang = 2.0 * np.pi * np.outer(k1, n1) / FFT_N1
    fa = np.concatenate([np.cos(ang), -np.sin(ang)[1:FFT_H]], axis=0)
    fa_k = np.kron(fa, np.eye(SUBLANES))
    weight = np.where((k1 == 0) | (k1 == FFT_H), 1.0, 2.0)[:, None] / FFT_N
    fai = np.concatenate([weight * np.cos(ang), (-2.0 / FFT_N) * np.sin(ang)[1:FFT_H]], axis=0).T
    fai_k = np.kron(fai, np.eye(SUBLANES))

    n2 = np.arange(FFT_N2)
    k2 = np.arange(FFT_N2)

    def cs(k1v):
        idx = (np.outer(k2, n2) * FFT_N1 + k1v * n2[None, :]) % FFT_N
        phi = 2.0 * np.pi * idx / FFT_N
        return np.cos(phi), np.sin(phi)

    fb, fbi = [], []
    for k1v in range(1, FFT_H):
        c, s = cs(k1v)
        fb.append(np.block([[c, s], [-s, c]]))
        fbi.append(np.block([[c.T, -s.T], [s.T, c.T]]))
    c0, s0 = cs(0)
    ch, sh = cs(FFT_H)
    z = np.zeros_like(c0)
    fbs = np.block([[c0, z], [-s0, z], [z, ch], [z, -sh]])
    fbsi = np.block([[c0.T, -s0.T, z, z], [z, z, ch.T, -sh.T]])
    as32 = lambda a: np.asarray(a, np.float32)
    return as32(fa_k), as32(fai_k), as32(np.stack(fb)), as32(np.stack(fbi)), as32(fbs), as32(fbsi)


def _hyena_kernel(z_ref, gate_ref, kf_ref, skip_ref, fa_ref, fai_ref, fb_ref, fbi_ref, fbs_ref, fbsi_ref,
                  o_ref, w_scr, conv_scr):
    tl = z_ref.shape[-1]
    half = FFT_N2
    pack = 2 * SUBLANES

    for p in range(FFT_N2 // pack):
        zz = z_ref[:, p * pack:(p + 1) * pack, :].astype(_f32)
        for hf in range(2):
            r0 = p * pack + hf * SUBLANES
            zin = zz[:, hf * SUBLANES:(hf + 1) * SUBLANES, :].reshape(FFT_H * SUBLANES, tl)
            y = _dot(fa_ref[...], zin.astype(_bf16))
            w_scr[:, r0:r0 + SUBLANES, :] = y.reshape(FFT_N1, SUBLANES, tl)

    yin = jnp.concatenate([w_scr[0], w_scr[FFT_H]], axis=0).astype(_bf16)
    x = _dot(fbs_ref[...], yin)
    pieces = []
    for q in range(2):
        xr = x[(2 * q) * half:(2 * q + 1) * half]
        xi = x[(2 * q + 1) * half:(2 * q + 2) * half]
        kr = kf_ref[pl.ds((2 * q) * half, half), :]
        ki = kf_ref[pl.ds((2 * q + 1) * half, half), :]
        pieces += [xr * kr - xi * ki, xr * ki + xi * kr]
    v = _dot(fbsi_ref[...], jnp.concatenate(pieces, axis=0).astype(_bf16))
    w_scr[0] = v[:half]
    w_scr[FFT_H] = v[half:]

    def stage_b(k1, carry):
        yin = jnp.concatenate([w_scr[k1], w_scr[FFT_H + k1]], axis=0).astype(_bf16)
        x = _dot(fb_ref[k1 - 1], yin)
        xr, xi = x[:half], x[half:]
        base = pl.multiple_of(2 * half * (k1 + 1), 2 * half)
        kr = kf_ref[pl.ds(base, half), :]
        ki = kf_ref[pl.ds(base + half, half), :]
        p = jnp.concatenate([xr * kr - xi * ki, xr * ki + xi * kr], axis=0).astype(_bf16)
        v = _dot(fbi_ref[k1 - 1], p)
        w_scr[k1] = v[:half]
        w_scr[FFT_H + k1] = v[half:]
        return carry

    lax.fori_loop(1, FFT_H, stage_b, 0, unroll=HY_UNROLL_B)

    for g in range(FFT_N2 // SUBLANES):
        r0 = g * SUBLANES
        vin = w_scr[:, r0:r0 + SUBLANES, :].reshape(FFT_N1 * SUBLANES, tl)
        conv_scr[:, r0:r0 + SUBLANES, :] = _dot(fai_ref[...], vin.astype(_bf16)).reshape(FFT_H, SUBLANES, tl)

    skip = skip_ref[...]
    for c in range(0, FFT_H, HY_EPILOGUE_SLABS):
        sl = slice(c, c + HY_EPILOGUE_SLABS)
        out = gate_ref[sl].astype(_f32) * (conv_scr[sl] + skip * z_ref[sl].astype(_f32))
        o_ref[sl] = out.astype(o_ref.dtype)


def _single(shape, index_map):
    return pl.BlockSpec(shape, index_map, pipeline_mode=pl.Buffered(1))


def _hyena_order(z_arr, z_col, gate_arr, gate_col, kf, order, skip, consts, out_dtype):
    fa_k, fai_k, fb, fbi, fbs, fbsi = consts
    b = z_arr.shape[0]
    tl = LANE_TILE
    d_out = kf.shape[-1]
    seq_block = (None, FFT_H, FFT_N2, tl)
    const2 = lambda shape: _single(shape, lambda j, i: (0, 0))
    const3 = lambda shape: _single(shape, lambda j, i: (0, 0, 0))
    return pl.pallas_call(
        _hyena_kernel,
        grid=(d_out // tl, b),
        in_specs=[pl.BlockSpec(seq_block, lambda j, i: (i, 0, 0, z_col + j)),
                  pl.BlockSpec(seq_block, lambda j, i: (i, 0, 0, gate_col + j)),
                  pl.BlockSpec((None, kf.shape[1], tl), lambda j, i: (order, 0, j)),
                  pl.BlockSpec((1, tl), lambda j, i: (0, j)),
                  const2(fa_k.shape), const2(fai_k.shape), const3(fb.shape), const3(fbi.shape),
                  const2(fbs.shape), const2(fbsi.shape)],
        out_specs=pl.BlockSpec(seq_block, lambda j, i: (i, 0, 0, j)),
        out_shape=jax.ShapeDtypeStruct((b, FFT_H, FFT_N2, d_out), out_dtype),
        scratch_shapes=[pltpu.VMEM((FFT_N1, FFT_N2, tl), _f32), pltpu.VMEM((FFT_H, FFT_N2, tl), _f32)],
        compiler_params=_params(),
        name="hyena_conv",
    )(z_arr, gate_arr, kf, skip, fa_k, fai_k, fb, fbi, fbs, fbsi)


def _filter_kernel(hdn_ref, w4f_ref, w4b_ref, delta_ref, fa_ref, fb_ref, fbs_ref, o_ref, sig_scr, w_scr):
    seq = hdn_ref.shape[0]
    tl = o_ref.shape[-1]
    half = FFT_N2
    def dot3(w):
        w_hi = w.astype(_bf16)
        w_lo = (w - w_hi.astype(_f32)).astype(_bf16)
        return _dot(hdn_ref[...], jnp.concatenate([w_hi, w_hi, w_lo], axis=0))

    row = lax.broadcasted_iota(jnp.int32, (seq, tl), 0)
    decay = jnp.exp(-(row.astype(_f32) * (1.0 / (seq - 1))) * delta_ref[...])
    f = dot3(w4f_ref[...]) * decay
    g = dot3(w4b_ref[...]) * decay
    g = jnp.where(row == 0, 0.0, g)
    norm = jnp.sum(jnp.abs(f), axis=0, keepdims=True) + jnp.sum(jnp.abs(g), axis=0, keepdims=True)
    inv_norm = 1.0 / norm
    f = f * inv_norm
    g = g * inv_norm

    for part, sig in enumerate((f + g, f - g)):
        sig_scr[...] = sig.reshape(FFT_H, FFT_N2, tl)

        def stage_a(gi, carry):
            r0 = pl.multiple_of(gi * SUBLANES, SUBLANES)
            zin = sig_scr[:, pl.ds(r0, SUBLANES), :].reshape(FFT_H * SUBLANES, tl)
            y = _dot(fa_ref[...], zin.astype(_bf16))
            w_scr[:, pl.ds(r0, SUBLANES), :] = y.reshape(FFT_N1, SUBLANES, tl)
            return carry

        lax.fori_loop(0, FFT_N2 // SUBLANES, stage_a, 0, unroll=HY_UNROLL_A)

        yin = jnp.concatenate([w_scr[0], w_scr[FFT_H]], axis=0).astype(_bf16)
        o_ref[pl.ds(part * half, half), :] = _dot(fbs_ref[pl.ds(part * half, half), :], yin)
        o_ref[pl.ds((2 + part) * half, half), :] = _dot(fbs_ref[pl.ds((2 + part) * half, half), :], yin)

        def stage_b(k1, carry):
            yin = jnp.concatenate([w_scr[k1], w_scr[FFT_H + k1]], axis=0).astype(_bf16)
            base = pl.multiple_of(2 * half * (k1 + 1) + part * half, half)
            o_ref[pl.ds(base, half), :] = _dot(fb_ref[k1 - 1, pl.ds(part * half, half), :], yin)
            return carry

        lax.fori_loop(1, FFT_H, stage_b, 0, unroll=HY_UNROLL_B)


def _filter_features(seq, hy_w1, hy_b1, hy_w2, hy_b2, hy_w3, hy_b3, hy_freq):
    t = jnp.linspace(0.0, 1.0, seq, dtype=_f32)[:, None]
    w = 2.0 * math.pi * jnp.arange(seq, dtype=_f32)[:, None] / seq
    f = jnp.linspace(1e-4, HY_BANDS - 1, HY_BANDS, dtype=_f32)[None, :]
    z = jnp.concatenate([t, jnp.cos(f * w), -jnp.sin(f * w)], axis=-1)
    hi = lax.Precision.HIGHEST
    hdn = jnp.sin(hy_freq * (jnp.dot(z, hy_w1, precision=hi) + hy_b1))
    hdn = jnp.sin(hy_freq * (jnp.dot(hdn, hy_w2, precision=hi) + hy_b2))
    hdn = jnp.sin(hy_freq * (jnp.dot(hdn, hy_w3, precision=hi) + hy_b3))
    hdn_hi = hdn.astype(_bf16)
    hdn_lo = (hdn - hdn_hi.astype(_f32)).astype(_bf16)
    return jnp.concatenate([hdn_hi, hdn_lo, hdn_hi], axis=1)


def _filter_spectra(hdn, hy_w4, consts):
    fa_k, _, fb, _, fbs, _ = consts
    width = hy_w4.shape[0]
    d = hy_w4.shape[1] // (2 * HY_ORDER)
    tl = LANE_TILE
    w4 = hy_w4.reshape(width, 2 * HY_ORDER, d).transpose(1, 0, 2)
    max_decay = math.log(HY_TARGET) / HY_FAST_DECAY
    min_decay = math.log(HY_TARGET) / HY_SLOW_DECAY
    deltas = jnp.abs(jnp.linspace(min_decay, max_decay, d, dtype=_f32)).reshape(1, d)
    n_rows = (FFT_H + 1) * 2 * FFT_N2
    const2 = lambda shape: _single(shape, lambda o, j: (0, 0))
    return pl.pallas_call(
        _filter_kernel,
        grid=(HY_ORDER, d // tl),
        in_specs=[const2(hdn.shape),
                  pl.BlockSpec((None, width, tl), lambda o, j: (2 * o, 0, j)),
                  pl.BlockSpec((None, width, tl), lambda o, j: (2 * o + 1, 0, j)),
                  pl.BlockSpec((1, tl), lambda o, j: (0, j)),
                  const2(fa_k.shape), _single(fb.shape, lambda o, j: (0, 0, 0)), const2(fbs.shape)],
        out_specs=pl.BlockSpec((None, n_rows, tl), lambda o, j: (o, 0, j)),
        out_shape=jax.ShapeDtypeStruct((HY_ORDER, n_rows, d), _f32),
        scratch_shapes=[pltpu.VMEM((FFT_H, FFT_N2, tl), _f32), pltpu.VMEM((FFT_N1, FFT_N2, tl), _f32)],
        compiler_params=_params(),
        name="filter_spectra",
    )(hdn, w4, w4, deltas, fa_k, fb, fbs)


def _merge_kernel(hx_ref, hsum_ref, yh_ref, x_ref, wrg_ref, brg_ref, wbg_ref, bbg_ref, wa_ref, wb_ref, wo_ref,
                  g1_ref, g_ref, sh_ref, sc_ref, x1_ref, h2_ref):
    d = x_ref.shape[-1]
    gates = _sigmoid(_dot(hx_ref[...], wbg_ref[...]) + bbg_ref[...])
    y_rnn = hsum_ref[...].astype(_f32) * jax.nn.gelu(_dot(hx_ref[...], wrg_ref[...]) + brg_ref[...])
    ya = _dot(y_rnn.astype(_bf16), wa_ref[...])
    yb = _dot(yh_ref[...].astype(_bf16), wb_ref[...])
    merged = gates[:, :d] * ya + gates[:, d:] * yb
    x1 = x_ref[...] + g1_ref[...] * _dot(merged.astype(_bf16), wo_ref[...])
    x1_ref[...] = x1
    y = x1 * lax.rsqrt(jnp.mean(x1 * x1, axis=-1, keepdims=True) + RMS_EPS) * g_ref[...]
    h2_ref[...] = (y * (1.0 + sc_ref[...]) + sh_ref[...]).astype(h2_ref.dtype)


def _merge(hx, h_sum, y_hy, x, w_in_bf, b_in, w_bg, b_bg, w_a, w_b, w_o, g1, norm2_g, sh2, sc2, tm=1024):
    b, l, d = x.shape
    row = lambda width: pl.BlockSpec((None, tm, width), lambda i, j: (i, j, 0))
    vec = pl.BlockSpec((None, 1, d), lambda i, j: (i, 0, 0))
    full = lambda a: _single(a.shape, lambda i, j: (0,) * a.ndim)
    b_in = b_in.reshape(1, -1)
    b_bg = b_bg.reshape(1, -1)
    norm2_g = norm2_g.reshape(1, d)
    return pl.pallas_call(
        _merge_kernel,
        grid=(b, l // tm),
        in_specs=[row(d), row(d), row(d), row(d),
                  _single((d, d), lambda i, j: (0, 1)), _single((1, d), lambda i, j: (0, 1)),
                  full(w_bg), full(b_bg), full(w_a), full(w_b), full(w_o), vec, full(norm2_g), vec, vec],
        out_specs=[row(d), row(d)],
        out_shape=[jax.ShapeDtypeStruct((b, l, d), _f32), jax.ShapeDtypeStruct((b, l, d), _bf16)],
        compiler_params=pltpu.CompilerParams(vmem_limit_bytes=MERGE_VMEM_LIMIT),
        name="merge",
    )(hx, h_sum, y_hy, x, w_in_bf, b_in, w_bg, b_bg, w_a, w_b, w_o, g1, norm2_g, sh2, sc2)


def _ffn_kernel(h_ref, x1_ref, wg_ref, wu_ref, wo_ref, g2_ref, fg_ref, o_ref):
    h = h_ref[...]
    g = _dot(h, wg_ref[...])
    u = _dot(h, wu_ref[...])
    a = (g * _sigmoid(g) * u).astype(_bf16)
    x2 = x1_ref[...] + g2_ref[...] * _dot(a, wo_ref[...])
    y = x2 * lax.rsqrt(jnp.mean(x2 * x2, axis=-1, keepdims=True) + RMS_EPS)
    o_ref[...] = y * fg_ref[...]


def _ffn(h2, x1, w_gu, w_o, g2, final_g, tm=512):
    b, l, d = x1.shape
    d_ff = w_o.shape[0]
    row = pl.BlockSpec((None, tm, d), lambda i, j: (i, j, 0))
    vec = pl.BlockSpec((None, 1, d), lambda i, j: (i, 0, 0))
    full = lambda a: _single(a.shape, lambda i, j: (0,) * a.ndim)
    final_g = final_g.reshape(1, d)
    return pl.pallas_call(
        _ffn_kernel,
        grid=(b, l // tm),
        in_specs=[row, row, _single((d, d_ff), lambda i, j: (0, 0)), _single((d, d_ff), lambda i, j: (0, 1)),
                  full(w_o), vec, full(final_g)],
        out_specs=row,
        out_shape=jax.ShapeDtypeStruct((b, l, d), _f32),
        compiler_params=_params(),
        name="ffn",
    )(h2, x1, w_gu, w_gu, w_o, g2, final_g)


def kernel(x, c, ctx, c_ctx, w_mod, b_mod, norm1_g, norm2_g, w_in, b_in, rnn_conv_w, rnn_conv_b, rg_wa, rg_ba,
           rg_wx, rg_bx, rg_lambda, hy_conv_w, hy_conv_b, hy_w1, hy_b1, hy_w2, hy_b2, hy_w3, hy_b3, hy_freq,
           hy_w4, hy_skip, w_a_out, w_b_out, w_out, w_ffn_in, w_ffn_out, final_g):
    assert w_mod.shape[0] == 1, "single-layer block"
    b, seq, d = x.shape
    ctx_len = ctx.shape[1]
    assert seq == FFT_H * FFT_N2 and d == D_MODEL
    assert b == SUBLANES, "time-major rows put the batch on the sublanes of one register"
    (w_mod, b_mod, norm1_g, norm2_g, w_in, b_in, rnn_conv_w, rnn_conv_b, rg_wa, rg_ba, rg_wx, rg_bx,
     rg_lambda, hy_conv_w, hy_conv_b, hy_w1, hy_b1, hy_w2, hy_b2, hy_w3, hy_b3, hy_freq, hy_w4, hy_skip,
     w_a_out, w_b_out, w_out, w_ffn_in, w_ffn_out) = [
        a[0] for a in (w_mod, b_mod, norm1_g, norm2_g, w_in, b_in, rnn_conv_w, rnn_conv_b, rg_wa, rg_ba,
                       rg_wx, rg_bx, rg_lambda, hy_conv_w, hy_conv_b, hy_w1, hy_b1, hy_w2, hy_b2, hy_w3,
                       hy_b3, hy_freq, hy_w4, hy_skip, w_a_out, w_b_out, w_out, w_ffn_in, w_ffn_out)]

    pad_rows = 2 * SUBLANES - b - 1
    c_all = jnp.concatenate([c, c_ctx[None, :], jnp.zeros((pad_rows, d), _f32)], axis=0)
    mod = _mod_vectors(c_all, w_mod, b_mod)
    sh1, sc1, g1, sh2, sc2, g2 = [m.reshape(b, 1, d) for m in jnp.split(mod[:b], N_MOD, axis=-1)]
    csh1, csc1 = [jnp.broadcast_to(m.reshape(1, 1, d), (b, 1, d))
                  for m in jnp.split(mod[b], N_MOD, axis=-1)[:2]]

    w_in_bf = w_in.astype(_bf16)
    w_rx, b_rx = w_in_bf[:, :d], b_in[:d]
    w_bg, b_bg = w_in_bf[:, 5 * d:], b_in[5 * d:]
    gate_w, gate_bias = zip(*[_gate_weights(rg_wa[i], rg_wx[i], rg_ba[i], rg_bx[i]) for i in range(2)])
    gate_w, gate_bias = jnp.stack(gate_w), jnp.stack(gate_bias)
    lam = rg_lambda.reshape(2, 1, d)
    zeros_state = jnp.zeros((b, d), _f32)
    rnn_conv_w, rnn_conv_b = 0.5 * rnn_conv_w, 0.5 * rnn_conv_b

    hc_tb = _norm_mod(ctx, norm1_g, csh1, csc1)
    u_c = _proj_conv(hc_tb, w_rx, b_rx, rnn_conv_w, rnn_conv_b, pad_left=RNN_CONV_PAD_LEFT, period=ctx_len,
                     row_stride=b, tm=ctx_len * b, out_dtype=_bf16)
    _, cf, cb = _rglru(u_c, b, gate_w, gate_bias, lam, zeros_state, zeros_state)

    hx, u, q = _inproj(x, norm1_g, sh1, sc1, w_in_bf, b_in, rnn_conv_w, rnn_conv_b, hy_conv_w, hy_conv_b, GRID_W)
    h_sum, _, _ = _rglru(u, b, gate_w, gate_bias, lam, cf, cb)

    consts = [jnp.asarray(a).astype(_bf16) for a in _dft_constants()]
    hdn = _filter_features(seq, hy_w1, hy_b1, hy_w2, hy_b2, hy_w3, hy_b3, hy_freq)
    kf = _filter_spectra(hdn, hy_w4, consts)
    q4 = q.reshape(b, FFT_H, FFT_N2, 3 * d)
    n_tiles = d // LANE_TILE
    z1 = _hyena_order(q4, 0, q4, n_tiles, kf, 0, hy_skip[0:1], consts, _bf16)
    y_hy = _hyena_order(z1, 0, q4, 2 * n_tiles, kf, 1, hy_skip[1:2], consts, _bf16)
    y_hy = y_hy.reshape(b, seq, d)

    x1, h2 = _merge(hx, h_sum, y_hy, x, w_in_bf, b_in, w_bg, b_bg, w_a_out.astype(_bf16), w_b_out.astype(_bf16),
                    w_out.astype(_bf16), g1, norm2_g, sh2, sc2)
    w_ffn_bf = w_ffn_in.astype(_bf16)
    return _ffn(h2, x1, w_ffn_bf, w_ffn_out.astype(_bf16), g2, final_g)
```
